```python
import math
import numpy as np
import jax
import jax.numpy as jnp
from jax import lax

D_MODEL = 2048
BATCH = 1
SEQ = 16384
DEPTH = 2

GRID_W = 64
CTX_LEN = 256
N_BRANCH = 4
BRANCH_W = D_MODEL // N_BRANCH
POOL_WINDOWS = (2, 4, 8, 16)
POOL_GROUP_W = BRANCH_W // len(POOL_WINDOWS)
DN_DK = 128
DN_DV = 128
DN_HEADS = BRANCH_W // DN_DV
DN_CONV = 5
DN_CHUNK = 64
SWA_HD = 64
SWA_Q_HEADS = BRANCH_W // SWA_HD
SWA_KV_HEADS = SWA_Q_HEADS // 4
SWA_GROUP = SWA_Q_HEADS // SWA_KV_HEADS
SWA_WINDOW = 128
SWA_BLOCK = 128
ROPE_BASE = 10000.0
SGU_CHUNK = 128
SGU_GROUPS = 4
SGU_GROUP_W = BRANCH_W // SGU_GROUPS
N_EXPERTS = 16
EC_FACTOR = 2
EXPERT_FF = D_MODEL // 2
RMS_EPS = 1e-6

IN_SPLITS = (BRANCH_W,
             DN_HEADS * DN_DK, DN_HEADS * DN_DK,
             DN_HEADS * DN_DV, DN_HEADS * DN_DV,
             2 * DN_HEADS, 2 * DN_HEADS,
             SWA_Q_HEADS * SWA_HD, SWA_KV_HEADS * SWA_HD, SWA_KV_HEADS * SWA_HD,
             BRANCH_W, BRANCH_W)
IN_W = sum(IN_SPLITS)
IN_OFFSETS = tuple(int(o) for o in np.cumsum(IN_SPLITS)[:-1])

kernel_name = "hybrid_parallel_diffusion_block"


def rmsnorm(x, g):
    xf = x.astype(jnp.float32)
    y = xf * lax.rsqrt(jnp.mean(xf * xf, axis=-1, keepdims=True) + RMS_EPS)
    return (y * g.astype(jnp.float32)).astype(x.dtype)


def modulate(h, shift, scale):
    return h * (1 + scale) + shift


def l2norm(x):
    return x * lax.rsqrt(jnp.sum(x * x, axis=-1, keepdims=True) + 1e-6)


def pool_mixer(p, pool_w, pool_scale):
    B, T, _ = p.shape
    pf = p.astype(jnp.float32)
    t = jnp.arange(T)
    outs = []
    for gi, w in enumerate(POOL_WINDOWS):
        pg = pf[..., gi * POOL_GROUP_W:(gi + 1) * POOL_GROUP_W]
        cs = jnp.concatenate([jnp.zeros_like(pg[:, :1]), jnp.cumsum(pg, axis=1)], axis=1)
        lo = jnp.clip(t - w // 2, 0, T - 1)
        hi = jnp.clip(t + w // 2 - 1, 0, T - 1)
        mean = (cs[:, hi + 1] - cs[:, lo]) / (hi - lo + 1).astype(jnp.float32)[None, :, None]
        outs.append(mean - pg)
    y = jnp.stack(outs, axis=2).astype(p.dtype)
    y = jnp.einsum('btgc,gce->btge', y, pool_w).reshape(B, T, BRANCH_W)
    return y * pool_scale


def short_conv(x, w):
    pad = w.shape[0] // 2
    return lax.conv_general_dilated(x, w[:, None, :], window_strides=(1,), padding=[(pad, pad)],
                                    dimension_numbers=('NWC', 'WIO', 'NWC'),
                                    feature_group_count=x.shape[-1])


def dn_prepare(q, k, v, a, b, conv_w, a_log, dt_bias):
    B, T, _ = q.shape
    qkv = jax.nn.silu(short_conv(jnp.concatenate([q, k, v], axis=-1), conv_w))
    q, k, v = jnp.split(qkv, [DN_HEADS * DN_DK, 2 * DN_HEADS * DN_DK], axis=-1)
    q = l2norm(q.reshape(B, T, DN_HEADS, DN_DK).astype(jnp.float32)) * DN_DK ** -0.5
    k = l2norm(k.reshape(B, T, DN_HEADS, DN_DK).astype(jnp.float32))
    v = v.reshape(B, T, DN_HEADS, DN_DV).astype(jnp.float32)
    a = a.astype(jnp.float32).reshape(B, T, 2, DN_HEADS)
    b = b.astype(jnp.float32).reshape(B, T, 2, DN_HEADS)
    g = -jnp.exp(a_log.astype(jnp.float32)) * jax.nn.softplus(a + dt_bias.astype(jnp.float32))
    beta = jax.nn.sigmoid(b)
    return q, k, v, g, beta


def gated_delta_chunked(q, k, v, g, beta, s0):
    B, T, H, _ = q.shape
    dv = v.shape[-1]
    L = DN_CHUNK
    N = T // L

    def chunks(t):
        return t.reshape(B, N, L, H, -1).transpose(0, 1, 3, 2, 4)

    q, k, v = chunks(q), chunks(k), chunks(v)
    g = g.reshape(B, N, L, H).transpose(0, 1, 3, 2)
    beta = beta.reshape(B, N, L, H).transpose(0, 1, 3, 2)
    gc = jnp.cumsum(g, axis=-1)
    incl = jnp.tril(jnp.ones((L, L), dtype=bool))
    strict = jnp.tril(jnp.ones((L, L), dtype=bool), k=-1)
    decay = jnp.exp(jnp.where(incl, gc[..., :, None] - gc[..., None, :], -jnp.inf))
    kb = k * beta[..., None]
    m = jnp.where(strict, jnp.einsum('bnhid,bnhjd->bnhij', kb, k) * decay, 0.0)
    a_mat = m + jnp.eye(L, dtype=m.dtype)
    rhs = jnp.concatenate([v * beta[..., None], kb * jnp.exp(gc)[..., None]], axis=-1)
    sol = lax.linalg.triangular_solve(a_mat, rhs, left_side=True, lower=True)
    u, w = sol[..., :dv], sol[..., dv:]
    qk = jnp.einsum('bnhid,bnhjd->bnhij', q, k) * decay
    q_dec = q * jnp.exp(gc)[..., None]
    k_dec = k * jnp.exp(gc[..., -1:] - gc)[..., None]
    g_last = jnp.exp(gc[..., -1])

    def step(s, xs):
        q_i, k_i, u_i, w_i, qk_i, gl_i = xs
        v_new = u_i - jnp.einsum('bhld,bhde->bhle', w_i, s)
        o_i = jnp.einsum('bhld,bhde->bhle', q_i, s) + jnp.einsum('bhlm,bhme->bhle', qk_i, v_new)
        s = s * gl_i[..., None, None] + jnp.einsum('bhld,bhle->bhde', k_i, v_new)
        return s, o_i

    xs = tuple(jnp.moveaxis(t, 1, 0) for t in (q_dec, k_dec, u, w, qk, g_last))
    s_fin, o = lax.scan(step, s0, xs)
    o = o.transpose(1, 0, 3, 2, 4).reshape(B, T, H, dv)
    return o, s_fin


def dn_gated_out(o, z, norm_g):
    B, T = z.shape[:2]
    zh = z.reshape(B, T, DN_HEADS, DN_DV).astype(jnp.float32)
    y = rmsnorm(o, norm_g) * jax.nn.silu(zh)
    return y.reshape(B, T, DN_HEADS * DN_DV).astype(z.dtype)


def deltanet_mixer(q, k, v, z, a, b, qc, kc, vc, zc, ac, bc, conv_w, a_log, dt_bias, norm_g, want_ctx):
    ql, kl, vl, gl, bl = dn_prepare(q, k, v, a, b, conv_w, a_log, dt_bias)
    qx, kx, vx, gx, bx = dn_prepare(qc, kc, vc, ac, bc, conv_w, a_log, dt_bias)
    s0 = jnp.zeros((q.shape[0], DN_HEADS, DN_DK, DN_DV), jnp.float32)

    def flip(t):
        return jnp.flip(t, axis=1)

    oc_f, s_f = gated_delta_chunked(qx, kx, vx, gx[:, :, 0], bx[:, :, 0], s0)
    o_f, _ = gated_delta_chunked(ql, kl, vl, gl[:, :, 0], bl[:, :, 0], s_f)
    oc_b, s_b = gated_delta_chunked(flip(qx), flip(kx), flip(vx), flip(gx[:, :, 1]), flip(bx[:, :, 1]), s0)
    o_b, _ = gated_delta_chunked(flip(ql), flip(kl), flip(vl), flip(gl[:, :, 1]), flip(bl[:, :, 1]), s_b)
    y = dn_gated_out(o_f + flip(o_b), z, norm_g)
    yc = dn_gated_out(oc_f + flip(oc_b), zc, norm_g) if want_ctx else None
    return y, yc


def rope_1d(x, pos):
    d = x.shape[-1]
    half = d // 2
    inv = ROPE_BASE ** (-jnp.arange(half, dtype=jnp.float32) / half)
    ang = pos.astype(jnp.float32)[:, None] * inv
    cos = jnp.cos(ang)[None, :, None, :]
    sin = jnp.sin(ang)[None, :, None, :]
    x1, x2 = x[..., :half], x[..., half:]
    return jnp.concatenate([x1 * cos - x2 * sin, x2 * cos + x1 * sin], axis=-1).astype(x.dtype)


def rope_2d(x, pos_rows, pos_cols):
    h = x.shape[-1] // 2
    return jnp.concatenate([rope_1d(x[..., :h], pos_rows), rope_1d(x[..., h:], pos_cols)], axis=-1)


def banded(t):
    B, T = t.shape[:2]
    NB = T // SWA_BLOCK
    tp = jnp.pad(t, ((0, 0), (SWA_BLOCK, SWA_BLOCK), (0, 0), (0, 0)))
    tp = tp.reshape(B, NB + 2, SWA_BLOCK, *t.shape[2:])
    return jnp.concatenate([tp[:, :-2], tp[:, 1:-1], tp[:, 2:]], axis=2)


def swa_mixer(q, k, v, qc, kc, vc, sink, pos_rows, pos_cols, want_ctx):
    B, T, _ = q.shape
    Lc = kc.shape[1]
    BLK = SWA_BLOCK
    NB = T // BLK
    scale = SWA_HD ** -0.5
    q = rope_2d(q.reshape(B, T, SWA_Q_HEADS, SWA_HD), pos_rows, pos_cols)
    k = rope_2d(k.reshape(B, T, SWA_KV_HEADS, SWA_HD), pos_rows, pos_cols)
    v = v.reshape(B, T, SWA_KV_HEADS, SWA_HD)
    kc = kc.reshape(B, Lc, SWA_KV_HEADS, SWA_HD)
    vc = vc.reshape(B, Lc, SWA_KV_HEADS, SWA_HD)
    sink_f = sink.astype(jnp.float32).reshape(SWA_KV_HEADS, SWA_GROUP)

    qb = q.reshape(B, NB, BLK, SWA_KV_HEADS, SWA_GROUP, SWA_HD) * scale
    kb, vb = banded(k), banded(v)
    s_lat = jnp.einsum('bnqkgd,bnskd->bnkgqs', qb, kb).astype(jnp.float32)
    i = jnp.arange(BLK)[:, None]
    j = jnp.arange(3 * BLK)[None, :]
    band = jnp.abs(j - BLK - i) <= SWA_WINDOW
    kpos = jnp.arange(NB)[:, None] * BLK - BLK + jnp.arange(3 * BLK)[None, :]
    inb = (kpos >= 0) & (kpos < T)
    mask = band[None] & inb[:, None, :]
    s_lat = jnp.where(mask[None, :, None, None], s_lat, -jnp.inf)
    s_ctx = jnp.einsum('bnqkgd,bckd->bnkgqc', qb, kc).astype(jnp.float32)
    s_sink = jnp.broadcast_to(sink_f[None, None, :, :, None, None], s_lat.shape[:-1] + (1,))
    p = jax.nn.softmax(jnp.concatenate([s_lat, s_ctx, s_sink], axis=-1), axis=-1)
    o = (jnp.einsum('bnkgqs,bnskd->bnqkgd', p[..., :3 * BLK].astype(v.dtype), vb)
         + jnp.einsum('bnkgqc,bckd->bnqkgd', p[..., 3 * BLK:3 * BLK + Lc].astype(v.dtype), vc))
    y = o.reshape(B, T, SWA_Q_HEADS * SWA_HD)
    if not want_ctx:
        return y, None
    qcb = qc.reshape(B, Lc, SWA_KV_HEADS, SWA_GROUP, SWA_HD) * scale
    sc = jnp.einsum('bqkgd,bckd->bkgqc', qcb, kc).astype(jnp.float32)
    sc_sink = jnp.broadcast_to(sink_f[None, :, :, None, None], sc.shape[:-1] + (1,))
    pc = jax.nn.softmax(jnp.concatenate([sc, sc_sink], axis=-1), axis=-1)[..., :Lc]
    oc = jnp.einsum('bkgqc,bckd->bqkgd', pc.astype(vc.dtype), vc).reshape(B, Lc, SWA_Q_HEADS * SWA_HD)
    return y, oc


def sgu_mixer(u, v, norm_g, w_s, b_s):
    B, T, _ = u.shape
    N = T // SGU_CHUNK
    u = jax.nn.gelu(u, approximate=False)
    v = rmsnorm(jax.nn.gelu(v, approximate=False), norm_g)
    vg = v.reshape(B, N, SGU_CHUNK, SGU_GROUPS, SGU_GROUP_W)
    mixed = jnp.einsum('gpq,bnqgc->bnpgc', w_s, vg) + b_s.T[:, :, None]
    return (u.reshape(vg.shape) * mixed).reshape(B, T, BRANCH_W)


def merge_branches(h, ys, w_gate, w_branch, w_out):
    acc = jax.nn.sigmoid(h @ w_gate[0]) * (ys[0] @ w_branch[0])
    for bi in range(1, N_BRANCH):
        acc = acc + jax.nn.sigmoid(h @ w_gate[bi]) * (ys[bi] @ w_branch[bi])
    return acc @ w_out


def token_mixer(h, hc, pos_rows, pos_cols, w_in, pool_w, pool_scale, dn_conv_w, dn_a_log, dn_dt_bias,
                dn_norm_g, swa_sink, sgu_norm_g, sgu_w, sgu_b, w_gate, w_branch, w_out, want_ctx):
    (p, dq, dk, dv, dz, da, db, sq, sk, sv, gu, gv) = jnp.split(h @ w_in, IN_OFFSETS, axis=-1)
    (pc, dqc, dkc, dvc, dzc, dac, dbc, sqc, skc, svc, guc, gvc) = jnp.split(hc @ w_in, IN_OFFSETS, axis=-1)
    y_pool = pool_mixer(p, pool_w, pool_scale)
    y_dn, yc_dn = deltanet_mixer(dq, dk, dv, dz, da, db, dqc, dkc, dvc, dzc, dac, dbc,
                                 dn_conv_w, dn_a_log, dn_dt_bias, dn_norm_g, want_ctx)
    y_swa, yc_swa = swa_mixer(sq, sk, sv, sqc, skc, svc, swa_sink, pos_rows, pos_cols, want_ctx)
    y_sgu = sgu_mixer(gu, gv, sgu_norm_g, sgu_w, sgu_b)
    y = merge_branches(h, (y_pool, y_dn, y_swa, y_sgu), w_gate, w_branch, w_out)
    if not want_ctx:
        return y, None
    yc = merge_branches(hc, (pool_mixer(pc, pool_w, pool_scale), yc_dn, yc_swa,
                             sgu_mixer(guc, gvc, sgu_norm_g, sgu_w, sgu_b)), w_gate, w_branch, w_out)
    return y, yc


def expert_choice_ffn(h, w_router, b_router, w_e_gate, w_e_up, w_e_down):
    B, n, D = h.shape
    cap = EC_FACTOR * n // N_EXPERTS
    logits = jnp.einsum('bnd,de->bne', h, w_router).astype(jnp.float32) + b_router.astype(jnp.float32)
    aff = jax.nn.softmax(logits, axis=-1)
    gate, idx = lax.top_k(jnp.swapaxes(aff, 1, 2), cap)
    xe = jax.vmap(lambda hb, ib: hb[ib])(h, idx)
    hid = jax.nn.silu(jnp.einsum('becd,edf->becf', xe, w_e_gate)) * jnp.einsum('becd,edf->becf', xe, w_e_up)
    out = jnp.einsum('becf,efd->becd', hid, w_e_down) * gate[..., None].astype(h.dtype)
    return jax.vmap(lambda ib, ob: jnp.zeros((n, D), h.dtype).at[ib].add(ob))(idx, out)


def setup_inputs(seed: int = 0) -> dict:
    key = jax.random.key(seed)
    ks = jax.random.split(key, 28)
    L, D, f32 = DEPTH, D_MODEL, jnp.float32

    def nrm(k, shape, scale):
        return jax.random.normal(k, shape, f32) * scale

    def gain(k, shape):
        return 1.0 + 0.02 * jax.random.normal(k, shape, f32)

    dt = jnp.exp(jax.random.uniform(ks[13], (L, 2, DN_HEADS), f32, math.log(1e-3), math.log(1e-1)))
    return {
        "x": nrm(ks[0], (BATCH, SEQ, D), 1.0),
        "c": nrm(ks[1], (BATCH, D), 1.0),
        "ctx": nrm(ks[2], (BATCH, CTX_LEN, D), 1.0),
        "c_ctx": nrm(ks[3], (D,), 1.0),
        "w_ada": nrm(ks[4], (L, D, 6 * D), D ** -0.5),
        "b_ada": nrm(ks[5], (L, 6 * D), 0.02),
        "norm1_g": gain(ks[6], (L, D)),
        "norm2_g": gain(ks[7], (L, D)),
        "w_in": nrm(ks[8], (L, D, IN_W), D ** -0.5),
        "pool_w": nrm(ks[9], (L, len(POOL_WINDOWS), POOL_GROUP_W, POOL_GROUP_W), POOL_GROUP_W ** -0.5),
        "pool_scale": 1.0 + 0.1 * jax.random.normal(ks[10], (L, BRANCH_W), f32),
        "dn_conv_w": nrm(ks[11], (L, DN_CONV, 2 * DN_HEADS * DN_DK + DN_HEADS * DN_DV), DN_CONV ** -0.5),
        "dn_a_log": jnp.log(jax.random.uniform(ks[12], (L, 2, DN_HEADS), f32, 1.0, 16.0)),
        "dn_dt_bias": dt + jnp.log(-jnp.expm1(-dt)),
        "dn_norm_g": gain(ks[14], (L, DN_DV)),
        "swa_sink": nrm(ks[15], (L, SWA_Q_HEADS), 1.0),
        "sgu_norm_g": gain(ks[16], (L, BRANCH_W)),
        "sgu_w": nrm(ks[17], (L, SGU_GROUPS, SGU_CHUNK, SGU_CHUNK), SGU_CHUNK ** -0.5),
        "sgu_b": 1.0 + 0.1 * jax.random.normal(ks[18], (L, SGU_GROUPS, SGU_CHUNK), f32),
        "w_gate": nrm(ks[19], (L, N_BRANCH, D, D), D ** -0.5),
        "w_branch": nrm(ks[20], (L, N_BRANCH, BRANCH_W, D), BRANCH_W ** -0.5),
        "w_out": nrm(ks[21], (L, D, D), D ** -0.5),
        "w_router": nrm(ks[22], (L, D, N_EXPERTS), D ** -0.5),
        "b_router": nrm(ks[23], (L, N_EXPERTS), 0.01),
        "w_e_gate": nrm(ks[24], (L, N_EXPERTS, D, EXPERT_FF), D ** -0.5),
        "w_e_up": nrm(ks[25], (L, N_EXPERTS, D, EXPERT_FF), D ** -0.5),
        "w_e_down": nrm(ks[26], (L, N_EXPERTS, EXPERT_FF, D), EXPERT_FF ** -0.5),
        "final_g": gain(ks[27], (D,)),
    }


def reference(x, c, ctx, c_ctx, w_ada, b_ada, norm1_g, norm2_g, w_in, pool_w, pool_scale, dn_conv_w,
              dn_a_log, dn_dt_bias, dn_norm_g, swa_sink, sgu_norm_g, sgu_w, sgu_b, w_gate, w_branch,
              w_out, w_router, b_router, w_e_gate, w_e_up, w_e_down, final_g):
    B, T, _ = x.shape
    n_rows = T // GRID_W
    pos_rows = jnp.repeat(jnp.arange(n_rows, dtype=jnp.int32), GRID_W)
    pos_cols = jnp.tile(jnp.arange(GRID_W, dtype=jnp.int32), n_rows)
    cond_lat = jax.nn.silu(c)
    cond_ctx = jax.nn.silu(c_ctx)[None]
    for i in range(DEPTH):
        want_ctx = i < DEPTH - 1
        mod = (cond_lat @ w_ada[i] + b_ada[i])[:, None, :]
        mod_c = (cond_ctx @ w_ada[i] + b_ada[i])[:, None, :]
        sh1, sc1, g1, sh2, sc2, g2 = jnp.split(mod, 6, axis=-1)
        csh1, csc1, cg1, csh2, csc2, cg2 = jnp.split(mod_c, 6, axis=-1)
        h = modulate(rmsnorm(x, norm1_g[i]), sh1, sc1)
        hc = modulate(rmsnorm(ctx, norm1_g[i]), csh1, csc1)
        y, yc = token_mixer(h, hc, pos_rows, pos_cols, w_in[i], pool_w[i], pool_scale[i], dn_conv_w[i],
                            dn_a_log[i], dn_dt_bias[i], dn_norm_g[i], swa_sink[i], sgu_norm_g[i], sgu_w[i],
                            sgu_b[i], w_gate[i], w_branch[i], w_out[i], want_ctx)
        x = x + g1 * y
        h2 = modulate(rmsnorm(x, norm2_g[i]), sh2, sc2)
        x = x + g2 * expert_choice_ffn(h2, w_router[i], b_router[i], w_e_gate[i], w_e_up[i], w_e_down[i])
        if want_ctx:
            ctx = ctx + cg1 * yc
            hc2 = modulate(rmsnorm(ctx, norm2_g[i]), csh2, csc2)
            ctx = ctx + cg2 * expert_choice_ffn(hc2, w_router[i], b_router[i], w_e_gate[i], w_e_up[i], w_e_down[i])
    return rmsnorm(x, final_g)
```

```python
import functools
import math

import numpy as np
import jax
import jax.numpy as jnp
from jax import lax
from jax.experimental import pallas as pl
from jax.experimental.pallas import tpu as pltpu

F32 = jnp.float32
BF16 = jnp.bfloat16

D_MODEL = 2048
GRID_W = 64
N_BRANCH = 4
BRANCH_W = D_MODEL // N_BRANCH
POOL_WINDOWS = (2, 4, 8, 16)
POOL_GROUP_W = BRANCH_W // len(POOL_WINDOWS)
DN_DK = 128
DN_DV = 128
DN_HEADS = BRANCH_W // DN_DV
DN_CHUNK = 64
SWA_HD = 64
SWA_Q_HEADS = BRANCH_W // SWA_HD
SWA_KV_HEADS = SWA_Q_HEADS // 4
SWA_GROUP = SWA_Q_HEADS // SWA_KV_HEADS
SWA_WINDOW = 128
SWA_BLOCK = 128
ROPE_BASE = 10000.0
SGU_CHUNK = 128
SGU_GROUPS = 4
SGU_GROUP_W = BRANCH_W // SGU_GROUPS
N_EXPERTS = 16
EC_FACTOR = 2
RMS_EPS = 1e-6

IN_SPLITS = (BRANCH_W,
             DN_HEADS * DN_DK, DN_HEADS * DN_DK,
             DN_HEADS * DN_DV, DN_HEADS * DN_DV,
             2 * DN_HEADS, 2 * DN_HEADS,
             SWA_Q_HEADS * SWA_HD, SWA_KV_HEADS * SWA_HD, SWA_KV_HEADS * SWA_HD,
             BRANCH_W, BRANCH_W)
IN_OFFSETS = tuple(int(o) for o in np.cumsum(IN_SPLITS)[:-1])
AB_LO = IN_OFFSETS[4]
AB_W = 4 * DN_HEADS
MAIN_W = sum(IN_SPLITS) - AB_W
LANES = 128
MIB = 2 ** 20


def _cparams(sem, vmem_mib):
    return pltpu.CompilerParams(dimension_semantics=sem, vmem_limit_bytes=vmem_mib * MIB)


def _ada_kernel(c_ref, w_ref, b_ref, o_ref):
    c = c_ref[...]
    s = (c * jax.nn.sigmoid(c)).astype(BF16)
    o_ref[0] = jnp.dot(s, w_ref[0].astype(BF16), preferred_element_type=F32) + b_ref[0]


def _ada(cond8, w_ada, b_ada):
    L, D, N = w_ada.shape
    tn = 1024
    return pl.pallas_call(
        _ada_kernel,
        out_shape=jax.ShapeDtypeStruct((L, 8, N), F32),
        grid=(L, N // tn),
        in_specs=[pl.BlockSpec((8, D), lambda l, n: (0, 0)),
                  pl.BlockSpec((1, D, tn), lambda l, n: (l, 0, n)),
                  pl.BlockSpec((1, 1, tn), lambda l, n: (l, 0, n))],
        out_specs=pl.BlockSpec((1, 8, tn), lambda l, n: (l, 0, n)),
        compiler_params=_cparams(("parallel", "parallel"), 32),
        name="ada_mod",
    )(cond8, w_ada, b_ada.reshape(L, 1, N))


def _inproj_kernel(x_ref, g_ref, sh_ref, sc_ref, w_ref, wab_ref, proj_ref, ab_ref, h_ref, hs_ref):
    @pl.when(pl.program_id(1) == 0)
    def _():
        x = x_ref[...]
        y = x * lax.rsqrt(jnp.mean(x * x, axis=-1, keepdims=True) + RMS_EPS) * g_ref[...]
        hb = (y * (1.0 + sc_ref[...]) + sh_ref[...]).astype(BF16)
        hs_ref[...] = hb
        h_ref[...] = hb
        ab_ref[...] = jnp.dot(hb, wab_ref[...], preferred_element_type=F32)

    proj_ref[...] = jnp.dot(hs_ref[...], w_ref[...], preferred_element_type=F32).astype(proj_ref.dtype)


def _inproj(x, g, sh, sc, w_main, w_ab, tm):
    M, D = x.shape
    N = w_main.shape[1]
    tn = N // 2
    vec = pl.BlockSpec((1, D), lambda m, n: (0, 0))
    return pl.pallas_call(
        _inproj_kernel,
        out_shape=(jax.ShapeDtypeStruct((M, N), BF16),
                   jax.ShapeDtypeStruct((M, LANES), F32),
                   jax.ShapeDtypeStruct((M, D), BF16)),
        grid=(M // tm, N // tn),
        in_specs=[pl.BlockSpec((tm, D), lambda m, n: (m, 0)), vec, vec, vec,
                  pl.BlockSpec((D, tn), lambda m, n: (0, n)),
                  pl.BlockSpec((D, LANES), lambda m, n: (0, 0))],
        out_specs=(pl.BlockSpec((tm, tn), lambda m, n: (m, n)),
                   pl.BlockSpec((tm, LANES), lambda m, n: (m, 0)),
                   pl.BlockSpec((tm, D), lambda m, n: (m, 0))),
        scratch_shapes=[pltpu.VMEM((tm, D), BF16)],
        compiler_params=_cparams(("parallel", "arbitrary"), 48),
        name="inproj",
    )(x, g, sh, sc, w_main, w_ab)


def _merge_kernel(h_ref, y_ref, wg_ref, wb_ref, o_ref):
    h = h_ref[...]
    acc = None
    for i in range(N_BRANCH):
        gate = jnp.dot(h, wg_ref[i], preferred_element_type=F32)
        br = jnp.dot(y_ref[:, i * BRANCH_W:(i + 1) * BRANCH_W], wb_ref[i], preferred_element_type=F32)
        t = jax.nn.sigmoid(gate) * br
        acc = t if acc is None else acc + t
    o_ref[...] = acc.astype(o_ref.dtype)


def _merge(h, y, w_gate, w_branch, tm):
    M, D = h.shape
    tn = 512
    return pl.pallas_call(
        _merge_kernel,
        out_shape=jax.ShapeDtypeStruct((M, D), BF16),
        grid=(M // tm, D // tn),
        in_specs=[pl.BlockSpec((tm, D), lambda m, n: (m, 0)),
                  pl.BlockSpec((tm, D), lambda m, n: (m, 0)),
                  pl.BlockSpec((N_BRANCH, D, tn), lambda m, n: (0, 0, n)),
                  pl.BlockSpec((N_BRANCH, BRANCH_W, tn), lambda m, n: (0, 0, n))],
        out_specs=pl.BlockSpec((tm, tn), lambda m, n: (m, n)),
        compiler_params=_cparams(("parallel", "arbitrary"), 48),
        name="merge",
    )(h, y, w_gate, w_branch)


def _outproj_kernel(acc_ref, x_ref, w_ref, g1_ref, ng_ref, sh_ref, sc_ref, wr_ref, br_ref,
                    xo_ref, h2_ref, aff_ref):
    out = jnp.dot(acc_ref[...], w_ref[...], preferred_element_type=F32)
    xn = x_ref[...] + g1_ref[...] * out
    xo_ref[...] = xn
    y = xn * lax.rsqrt(jnp.mean(xn * xn, axis=-1, keepdims=True) + RMS_EPS) * ng_ref[...]
    h2 = (y * (1.0 + sc_ref[...]) + sh_ref[...]).astype(BF16)
    h2_ref[...] = h2
    logits = lax.dot_general(wr_ref[...], h2, (((1,), (1,)), ((), ())), preferred_element_type=F32)
    logits = logits + br_ref[...]
    e = jnp.exp(logits - jnp.max(logits, axis=0, keepdims=True))
    aff_ref[...] = e / jnp.sum(e, axis=0, keepdims=True)


def _outproj(acc, x, w_out, g1, ng, sh, sc, w_router_t, b_router, tm):
    M, D = x.shape
    E = w_router_t.shape[0]
    vec = pl.BlockSpec((1, D), lambda m: (0, 0))
    return pl.pallas_call(
        _outproj_kernel,
        out_shape=(jax.ShapeDtypeStruct((M, D), F32),
                   jax.ShapeDtypeStruct((M, D), BF16),
                   jax.ShapeDtypeStruct((E, M), F32)),
        grid=(M // tm,),
        in_specs=[pl.BlockSpec((tm, D), lambda m: (m, 0)),
                  pl.BlockSpec((tm, D), lambda m: (m, 0)),
                  pl.BlockSpec((D, D), lambda m: (0, 0)),
                  vec, vec, vec, vec,
                  pl.BlockSpec((E, D), lambda m: (0, 0)),
                  pl.BlockSpec((E, 1), lambda m: (0, 0))],
        out_specs=(pl.BlockSpec((tm, D), lambda m: (m, 0)),
                   pl.BlockSpec((tm, D), lambda m: (m, 0)),
                   pl.BlockSpec((E, tm), lambda m: (0, m))),
        compiler_params=_cparams(("parallel",), 48),
        name="outproj",
    )(acc, x, w_out, g1, ng, sh, sc, w_router_t, b_router)


def _ffn_kernel(x_ref, wg_ref, wu_ref, wd_ref, o_ref):
    x = x_ref[0]
    a = jnp.dot(x, wg_ref[0], preferred_element_type=F32)
    b = jnp.dot(x, wu_ref[0], preferred_element_type=F32)
    hid = (a * jax.nn.sigmoid(a) * b).astype(BF16)
    o_ref[0] = jnp.dot(hid, wd_ref[0], preferred_element_type=F32)


def _ffn(xe, wg, wu, wd):
    E, cap, D = xe.shape
    F = wg.shape[2]
    tm = min(cap, 512)
    return pl.pallas_call(
        _ffn_kernel,
        out_shape=jax.ShapeDtypeStruct((E, cap, D), F32),
        grid=(E, cap // tm),
        in_specs=[pl.BlockSpec((1, tm, D), lambda e, m: (e, m, 0)),
                  pl.BlockSpec((1, D, F), lambda e, m: (e, 0, 0)),
                  pl.BlockSpec((1, D, F), lambda e, m: (e, 0, 0)),
                  pl.BlockSpec((1, F, D), lambda e, m: (e, 0, 0))],
        out_specs=pl.BlockSpec((1, tm, D), lambda e, m: (e, m, 0)),
        compiler_params=_cparams(("parallel", "arbitrary"), 52),
        name="expert_ffn",
    )(xe, wg, wu, wd)


def _final_norm_kernel(x_ref, g_ref, o_ref):
    x = x_ref[...]
    o_ref[...] = x * lax.rsqrt(jnp.mean(x * x, axis=-1, keepdims=True) + RMS_EPS) * g_ref[...]


def _final_norm(x, g):
    M, D = x.shape
    tm = 512
    return pl.pallas_call(
        _final_norm_kernel,
        out_shape=jax.ShapeDtypeStruct((M, D), F32),
        grid=(M // tm,),
        in_specs=[pl.BlockSpec((tm, D), lambda m: (m, 0)), pl.BlockSpec((1, D), lambda m: (0, 0))],
        out_specs=pl.BlockSpec((tm, D), lambda m: (m, 0)),
        compiler_params=_cparams(("parallel",), 32),
        name="final_norm",
    )(x, g)


def _rmsnorm(x, g):
    xf = x.astype(F32)
    y = xf * lax.rsqrt(jnp.mean(xf * xf, axis=-1, keepdims=True) + RMS_EPS)
    return (y * g.astype(F32)).astype(x.dtype)


def _l2norm(x):
    return x * lax.rsqrt(jnp.sum(x * x, axis=-1, keepdims=True) + 1e-6)


def _pool_mixer(p, pool_w, pool_scale):
    B, T, _ = p.shape
    pf = p.astype(F32)
    t = jnp.arange(T)
    outs = []
    for gi, w in enumerate(POOL_WINDOWS):
        pg = pf[..., gi * POOL_GROUP_W:(gi + 1) * POOL_GROUP_W]
        cs = jnp.concatenate([jnp.zeros_like(pg[:, :1]), jnp.cumsum(pg, axis=1)], axis=1)
        lo = jnp.clip(t - w // 2, 0, T - 1)
        hi = jnp.clip(t + w // 2 - 1, 0, T - 1)
        mean = (cs[:, hi + 1] - cs[:, lo]) / (hi - lo + 1).astype(F32)[None, :, None]
        outs.append(mean - pg)
    y = jnp.stack(outs, axis=2).astype(p.dtype)
    y = jnp.einsum('btgc,gce->btge', y, pool_w).reshape(B, T, BRANCH_W)
    return y * pool_scale


def _short_conv(x, w):
    pad = w.shape[0] // 2
    return lax.conv_general_dilated(x, w[:, None, :], window_strides=(1,), padding=[(pad, pad)],
                                    dimension_numbers=('NWC', 'WIO', 'NWC'),
                                    feature_group_count=x.shape[-1])


def _dn_prepare(q, k, v, a, b, conv_w, a_log, dt_bias):
    B, T, _ = q.shape
    qkv = jax.nn.silu(_short_conv(jnp.concatenate([q, k, v], axis=-1), conv_w))
    q, k, v = jnp.split(qkv, [DN_HEADS * DN_DK, 2 * DN_HEADS * DN_DK], axis=-1)
    q = _l2norm(q.reshape(B, T, DN_HEADS, DN_DK).astype(F32)) * DN_DK ** -0.5
    k = _l2norm(k.reshape(B, T, DN_HEADS, DN_DK).astype(F32))
    v = v.reshape(B, T, DN_HEADS, DN_DV).astype(F32)
    a = a.astype(F32).reshape(B, T, 2, DN_HEADS)
    b = b.astype(F32).reshape(B, T, 2, DN_HEADS)
    g = -jnp.exp(a_log.astype(F32)) * jax.nn.softplus(a + dt_bias.astype(F32))
    beta = jax.nn.sigmoid(b)
    return q, k, v, g, beta


def _gated_delta_chunked(q, k, v, g, beta, s0):
    B, T, H, _ = q.shape
    dv = v.shape[-1]
    L = DN_CHUNK
    N = T // L

    def chunks(t):
        return t.reshape(B, N, L, H, -1).transpose(0, 1, 3, 2, 4)

    q, k, v = chunks(q), chunks(k), chunks(v)
    g = g.reshape(B, N, L, H).transpose(0, 1, 3, 2)
    beta = beta.reshape(B, N, L, H).transpose(0, 1, 3, 2)
    gc = jnp.cumsum(g, axis=-1)
    incl = jnp.tril(jnp.ones((L, L), dtype=bool))
    strict = jnp.tril(jnp.ones((L, L), dtype=bool), k=-1)
    decay = jnp.exp(jnp.where(incl, gc[..., :, None] - gc[..., None, :], -jnp.inf))
    kb = k * beta[..., None]
    m = jnp.where(strict, jnp.einsum('bnhid,bnhjd->bnhij', kb, k) * decay, 0.0)
    a_mat = m + jnp.eye(L, dtype=m.dtype)
    rhs = jnp.concatenate([v * beta[..., None], kb * jnp.exp(gc)[..., None]], axis=-1)
    sol = lax.linalg.triangular_solve(a_mat, rhs, left_side=True, lower=True)
    u, w = sol[..., :dv], sol[..., dv:]
    qk = jnp.einsum('bnhid,bnhjd->bnhij', q, k) * decay
    q_dec = q * jnp.exp(gc)[..., None]
    k_dec = k * jnp.exp(gc[..., -1:] - gc)[..., None]
    g_last = jnp.exp(gc[..., -1])

    def step(s, xs):
        q_i, k_i, u_i, w_i, qk_i, gl_i = xs
        v_new = u_i - jnp.einsum('bhld,bhde->bhle', w_i, s)
        o_i = jnp.einsum('bhld,bhde->bhle', q_i, s) + jnp.einsum('bhlm,bhme->bhle', qk_i, v_new)
        s = s * gl_i[..., None, None] + jnp.einsum('bhld,bhle->bhde', k_i, v_new)
        return s, o_i

    xs = tuple(jnp.moveaxis(t, 1, 0) for t in (q_dec, k_dec, u, w, qk, g_last))
    s_fin, o = lax.scan(step, s0, xs)
    o = o.transpose(1, 0, 3, 2, 4).reshape(B, T, H, dv)
    return o, s_fin


def _dn_gated_out(o, z, norm_g):
    B, T = z.shape[:2]
    zh = z.reshape(B, T, DN_HEADS, DN_DV).astype(F32)
    y = _rmsnorm(o, norm_g) * jax.nn.silu(zh)
    return y.reshape(B, T, DN_HEADS * DN_DV).astype(z.dtype)


def _deltanet_mixer(q, k, v, z, a, b, qc, kc, vc, zc, ac, bc, conv_w, a_log, dt_bias, norm_g, want_ctx):
    ql, kl, vl, gl, bl = _dn_prepare(q, k, v, a, b, conv_w, a_log, dt_bias)
    qx, kx, vx, gx, bx = _dn_prepare(qc, kc, vc, ac, bc, conv_w, a_log, dt_bias)
    s0 = jnp.zeros((q.shape[0], DN_HEADS, DN_DK, DN_DV), F32)

    def flip(t):
        return jnp.flip(t, axis=1)

    oc_f, s_f = _gated_delta_chunked(qx, kx, vx, gx[:, :, 0], bx[:, :, 0], s0)
    o_f, _ = _gated_delta_chunked(ql, kl, vl, gl[:, :, 0], bl[:, :, 0], s_f)
    oc_b, s_b = _gated_delta_chunked(flip(qx), flip(kx), flip(vx), flip(gx[:, :, 1]), flip(bx[:, :, 1]), s0)
    o_b, _ = _gated_delta_chunked(flip(ql), flip(kl), flip(vl), flip(gl[:, :, 1]), flip(bl[:, :, 1]), s_b)
    y = _dn_gated_out(o_f + flip(o_b), z, norm_g)
    yc = _dn_gated_out(oc_f + flip(oc_b), zc, norm_g) if want_ctx else None
    return y, yc


def _rope_1d(x, pos):
    d = x.shape[-1]
    half = d // 2
    inv = ROPE_BASE ** (-jnp.arange(half, dtype=F32) / half)
    ang = pos.astype(F32)[:, None] * inv
    cos = jnp.cos(ang)[None, :, None, :]
    sin = jnp.sin(ang)[None, :, None, :]
    x1, x2 = x[..., :half], x[..., half:]
    return jnp.concatenate([x1 * cos - x2 * sin, x2 * cos + x1 * sin], axis=-1).astype(x.dtype)


def _rope_2d(x, pos_rows, pos_cols):
    h = x.shape[-1] // 2
    return jnp.concatenate([_rope_1d(x[..., :h], pos_rows), _rope_1d(x[..., h:], pos_cols)], axis=-1)


def _banded(t):
    B, T = t.shape[:2]
    NB = T // SWA_BLOCK
    tp = jnp.pad(t, ((0, 0), (SWA_BLOCK, SWA_BLOCK), (0, 0), (0, 0)))
    tp = tp.reshape(B, NB + 2, SWA_BLOCK, *t.shape[2:])
    return jnp.concatenate([tp[:, :-2], tp[:, 1:-1], tp[:, 2:]], axis=2)


def _swa_mixer(q, k, v, qc, kc, vc, sink, pos_rows, pos_cols, want_ctx):
    B, T, _ = q.shape
    Lc = kc.shape[1]
    BLK = SWA_BLOCK
    NB = T // BLK
    scale = SWA_HD ** -0.5
    q = _rope_2d(q.reshape(B, T, SWA_Q_HEADS, SWA_HD), pos_rows, pos_cols)
    k = _rope_2d(k.reshape(B, T, SWA_KV_HEADS, SWA_HD), pos_rows, pos_cols)
    v = v.reshape(B, T, SWA_KV_HEADS, SWA_HD)
    kc = kc.reshape(B, Lc, SWA_KV_HEADS, SWA_HD)
    vc = vc.reshape(B, Lc, SWA_KV_HEADS, SWA_HD)
    sink_f = sink.astype(F32).reshape(SWA_KV_HEADS, SWA_GROUP)
    qb = q.reshape(B, NB, BLK, SWA_KV_HEADS, SWA_GROUP, SWA_HD) * scale
    kb, vb = _banded(k), _banded(v)
    s_lat = jnp.einsum('bnqkgd,bnskd->bnkgqs', qb, kb).astype(F32)
    i = jnp.arange(BLK)[:, None]
    j = jnp.arange(3 * BLK)[None, :]
    band = jnp.abs(j - BLK - i) <= SWA_WINDOW
    kpos = jnp.arange(NB)[:, None] * BLK - BLK + jnp.arange(3 * BLK)[None, :]
    inb = (kpos >= 0) & (kpos < T)
    mask = band[None] & inb[:, None, :]
    s_lat = jnp.where(mask[None, :, None, None], s_lat, -jnp.inf)
    s_ctx = jnp.einsum('bnqkgd,bckd->bnkgqc', qb, kc).astype(F32)
    s_sink = jnp.broadcast_to(sink_f[None, None, :, :, None, None], s_lat.shape[:-1] + (1,))
    p = jax.nn.softmax(jnp.concatenate([s_lat, s_ctx, s_sink], axis=-1), axis=-1)
    o = (jnp.einsum('bnkgqs,bnskd->bnqkgd', p[..., :3 * BLK].astype(v.dtype), vb)
         + jnp.einsum('bnkgqc,bckd->bnqkgd', p[..., 3 * BLK:3 * BLK + Lc].astype(v.dtype), vc))
    y = o.reshape(B, T, SWA_Q_HEADS * SWA_HD)
    if not want_ctx:
        return y, None
    qcb = qc.reshape(B, Lc, SWA_KV_HEADS, SWA_GROUP, SWA_HD) * scale
    sc = jnp.einsum('bqkgd,bckd->bkgqc', qcb, kc).astype(F32)
    sc_sink = jnp.broadcast_to(sink_f[None, :, :, None, None], sc.shape[:-1] + (1,))
    pc = jax.nn.softmax(jnp.concatenate([sc, sc_sink], axis=-1), axis=-1)[..., :Lc]
    oc = jnp.einsum('bkgqc,bckd->bqkgd', pc.astype(vc.dtype), vc).reshape(B, Lc, SWA_Q_HEADS * SWA_HD)
    return y, oc


def _sgu_mixer(u, v, norm_g, w_s, b_s):
    B, T, _ = u.shape
    N = T // SGU_CHUNK
    u = jax.nn.gelu(u, approximate=False)
    v = _rmsnorm(jax.nn.gelu(v, approximate=False), norm_g)
    vg = v.reshape(B, N, SGU_CHUNK, SGU_GROUPS, SGU_GROUP_W)
    mixed = jnp.einsum('gpq,bnqgc->bnpgc', w_s, vg) + b_s.T[:, :, None]
    return (u.reshape(vg.shape) * mixed).reshape(B, T, BRANCH_W)


def _split_proj(proj, ab):
    pf = proj.astype(F32)[None]
    sizes = [s for j, s in enumerate(IN_SPLITS) if j not in (5, 6)]
    offs = np.cumsum([0] + sizes)
    parts = [pf[..., offs[j]:offs[j + 1]] for j in range(len(sizes))]
    p, dq, dk, dv, dz, sq, sk, sv, gu, gv = parts
    da = ab[None, :, :2 * DN_HEADS]
    db = ab[None, :, 2 * DN_HEADS:4 * DN_HEADS]
    return p, dq, dk, dv, dz, da, db, sq, sk, sv, gu, gv


def _moe(h2, aff_t, wg, wu, wd):
    n, D = h2.shape
    E = aff_t.shape[0]
    cap = EC_FACTOR * n // E
    gate, idx = lax.top_k(aff_t, cap)
    xe = h2[idx]
    out = _ffn(xe, wg, wu, wd) * gate[..., None]
    return jnp.zeros((n, D), F32).at[idx.reshape(-1)].add(out.reshape(-1, D))


def kernel(x, c, ctx, c_ctx, w_ada, b_ada, norm1_g, norm2_g, w_in, pool_w, pool_scale, dn_conv_w, dn_a_log, dn_dt_bias, dn_norm_g, swa_sink, sgu_norm_g, sgu_w, sgu_b, w_gate, w_branch, w_out, w_router, b_router, w_e_gate, w_e_up, w_e_down, final_g):
    B, T, D = x.shape
    assert B == 1 and D == D_MODEL
    Lc = ctx.shape[1]
    depth = w_ada.shape[0]
    n_rows = T // GRID_W
    pos_rows = jnp.repeat(jnp.arange(n_rows, dtype=jnp.int32), GRID_W)
    pos_cols = jnp.tile(jnp.arange(GRID_W, dtype=jnp.int32), n_rows)

    cond8 = jnp.zeros((8, D), F32).at[0].set(c[0]).at[1].set(c_ctx)
    mods = _ada(cond8, w_ada, b_ada)

    xl = x[0]
    xc = ctx[0]
    tm_l = 512
    tm_c = Lc
    for i in range(depth):
        want_ctx = i < depth - 1
        ml = [mods[i, 0:1, j * D:(j + 1) * D] for j in range(6)]
        mc = [mods[i, 1:2, j * D:(j + 1) * D] for j in range(6)]
        n1 = norm1_g[i][None]
        n2 = norm2_g[i][None]
        wi = w_in[i]
        w_main = jnp.concatenate([wi[:, :AB_LO], wi[:, AB_LO + AB_W:]], axis=1).astype(BF16)
        w_ab = jnp.pad(wi[:, AB_LO:AB_LO + AB_W], ((0, 0), (0, LANES - AB_W))).astype(BF16)
        wgt = w_gate[i].astype(BF16)
        wbr = w_branch[i].astype(BF16)
        wo = w_out[i].astype(BF16)
        wr_t = w_router[i].T.astype(BF16)
        br = b_router[i][:, None]
        weg = w_e_gate[i].astype(BF16)
        weu = w_e_up[i].astype(BF16)
        wed = w_e_down[i].astype(BF16)

        proj, ab, h = _inproj(xl, n1, ml[0], ml[1], w_main, w_ab, tm_l)
        projc, abc, hc = _inproj(xc, n1, mc[0], mc[1], w_main, w_ab, tm_c)
        (p, dq, dk, dv, dz, da, db, sq, sk, sv, gu, gv) = _split_proj(proj, ab)
        (pc, dqc, dkc, dvc, dzc, dac, dbc, sqc, skc, svc, guc, gvc) = _split_proj(projc, abc)

        y_pool = _pool_mixer(p, pool_w[i], pool_scale[i])
        y_dn, yc_dn = _deltanet_mixer(dq, dk, dv, dz, da, db, dqc, dkc, dvc, dzc, dac, dbc,
                                      dn_conv_w[i], dn_a_log[i], dn_dt_bias[i], dn_norm_g[i], want_ctx)
        y_swa, yc_swa = _swa_mixer(sq, sk, sv, sqc, skc, svc, swa_sink[i], pos_rows, pos_cols, want_ctx)
        y_sgu = _sgu_mixer(gu, gv, sgu_norm_g[i], sgu_w[i], sgu_b[i])
        y = jnp.concatenate([y_pool, y_dn, y_swa, y_sgu], axis=-1)[0].astype(BF16)

        acc = _merge(h, y, wgt, wbr, tm_l)
        xl, h2, aff_t = _outproj(acc, xl, wo, ml[2], n2, ml[3], ml[4], wr_t, br, tm_l)
        xl = xl + ml[5] * _moe(h2, aff_t, weg, weu, wed)

        if want_ctx:
            yc_pool = _pool_mixer(pc, pool_w[i], pool_scale[i])
            yc_sgu = _sgu_mixer(guc, gvc, sgu_norm_g[i], sgu_w[i], sgu_b[i])
            yc = jnp.concatenate([yc_pool, yc_dn, yc_swa, yc_sgu], axis=-1)[0].astype(BF16)
            accc = _merge(hc, yc, wgt, wbr, tm_c)
            xc, hc2, affc_t = _outproj(accc, xc, wo, mc[2], n2, mc[3], mc[4], wr_t, br, tm_c)
            xc = xc + mc[5] * _moe(hc2, affc_t, weg, weu, wed)

    return _final_norm(xl, final_g[None])[None]
```

```python
import functools
import math

import numpy as np
import jax
import jax.numpy as jnp
from jax import lax
from jax.experimental import pallas as pl
from jax.experimental.pallas import tpu as pltpu

F32 = jnp.float32
BF16 = jnp.bfloat16

D_MODEL = 2048
GRID_W = 64
N_BRANCH = 4
BRANCH_W = D_MODEL // N_BRANCH
POOL_WINDOWS = (2, 4, 8, 16)
POOL_GROUP_W = BRANCH_W // len(POOL_WINDOWS)
DN_DK = 128
DN_DV = 128
DN_HEADS = BRANCH_W // DN_DV
DN_CHUNK = 64
SWA_HD = 64
SWA_Q_HEADS = BRANCH_W // SWA_HD
SWA_KV_HEADS = SWA_Q_HEADS // 4
SWA_GROUP = SWA_Q_HEADS // SWA_KV_HEADS
SWA_WINDOW = 128
SWA_BLOCK = 128
ROPE_BASE = 10000.0
SGU_CHUNK = 128
SGU_GROUPS = 4
SGU_GROUP_W = BRANCH_W // SGU_GROUPS
N_EXPERTS = 16
EC_FACTOR = 2
RMS_EPS = 1e-6

IN_SPLITS = (BRANCH_W,
             DN_HEADS * DN_DK, DN_HEADS * DN_DK,
             DN_HEADS * DN_DV, DN_HEADS * DN_DV,
             2 * DN_HEADS, 2 * DN_HEADS,
             SWA_Q_HEADS * SWA_HD, SWA_KV_HEADS * SWA_HD, SWA_KV_HEADS * SWA_HD,
             BRANCH_W, BRANCH_W)
IN_OFFSETS = tuple(int(o) for o in np.cumsum(IN_SPLITS)[:-1])
AB_LO = IN_OFFSETS[4]
AB_W = 4 * DN_HEADS
MAIN_W = sum(IN_SPLITS) - AB_W
COL_DZ, COL_P, COL_SQ, COL_GU, COL_GV, COL_SK, COL_SV = 1536, 2048, 2560, 3072, 3584, 4096, 4224
LANES = 128
MOE_TILE = 256
MOE_ALIGN = 16
MOE_WIN = 80
MIB = 2 ** 20


def _cparams(sem, vmem_mib):
    return pltpu.CompilerParams(dimension_semantics=sem, vmem_limit_bytes=vmem_mib * MIB)


def _ada_kernel(c_ref, w_ref, b_ref, o_ref):
    c = c_ref[...]
    s = (c * jax.nn.sigmoid(c)).astype(BF16)
    o_ref[0] = jnp.dot(s, w_ref[0].astype(BF16), preferred_element_type=F32) + b_ref[0]


def _ada(cond8, w_ada, b_ada):
    L, D, N = w_ada.shape
    tn = 1024
    return pl.pallas_call(
        _ada_kernel,
        out_shape=jax.ShapeDtypeStruct((L, 8, N), F32),
        grid=(L, N // tn),
        in_specs=[pl.BlockSpec((8, D), lambda l, n: (0, 0)),
                  pl.BlockSpec((1, D, tn), lambda l, n: (l, 0, n)),
                  pl.BlockSpec((1, 1, tn), lambda l, n: (l, 0, n))],
        out_specs=pl.BlockSpec((1, 8, tn), lambda l, n: (l, 0, n)),
        compiler_params=_cparams(("parallel", "parallel"), 32),
        name="ada_mod",
    )(cond8, w_ada, b_ada.reshape(L, 1, N))


def _inproj_kernel(x_ref, g_ref, sh_ref, sc_ref, w_ref, wab_ref, proj_ref, ab_ref, h_ref, hs_ref):
    @pl.when(pl.program_id(1) == 0)
    def _():
        x = x_ref[...]
        y = x * lax.rsqrt(jnp.mean(x * x, axis=-1, keepdims=True) + RMS_EPS) * g_ref[...]
        hb = (y * (1.0 + sc_ref[...]) + sh_ref[...]).astype(BF16)
        hs_ref[...] = hb
        h_ref[...] = hb
        ab_ref[...] = jnp.dot(hb, wab_ref[...], preferred_element_type=F32)

    proj_ref[...] = jnp.dot(hs_ref[...], w_ref[...], preferred_element_type=F32).astype(proj_ref.dtype)


def _inproj(x, g, sh, sc, w_main, w_ab, tm):
    M, D = x.shape
    N = w_main.shape[1]
    tn = N // 2
    vec = pl.BlockSpec((1, D), lambda m, n: (0, 0))
    return pl.pallas_call(
        _inproj_kernel,
        out_shape=(jax.ShapeDtypeStruct((M, N), BF16),
                   jax.ShapeDtypeStruct((M, LANES), F32),
                   jax.ShapeDtypeStruct((M, D), BF16)),
        grid=(M // tm, N // tn),
        in_specs=[pl.BlockSpec((tm, D), lambda m, n: (m, 0)), vec, vec, vec,
                  pl.BlockSpec((D, tn), lambda m, n: (0, n)),
                  pl.BlockSpec((D, LANES), lambda m, n: (0, 0))],
        out_specs=(pl.BlockSpec((tm, tn), lambda m, n: (m, n)),
                   pl.BlockSpec((tm, LANES), lambda m, n: (m, 0)),
                   pl.BlockSpec((tm, D), lambda m, n: (m, 0))),
        scratch_shapes=[pltpu.VMEM((tm, D), BF16)],
        compiler_params=_cparams(("parallel", "arbitrary"), 48),
        name="inproj",
    )(x, g, sh, sc, w_main, w_ab)


def _merge_kernel(h_ref, y0_ref, y1_ref, y2_ref, y3_ref, wg_ref, wb_ref, o_ref):
    h = h_ref[...]
    acc = None
    for i, y_ref in enumerate((y0_ref, y1_ref, y2_ref, y3_ref)):
        gate = jnp.dot(h, wg_ref[i], preferred_element_type=F32)
        br = jnp.dot(y_ref[...], wb_ref[i], preferred_element_type=F32)
        t = jax.nn.sigmoid(gate) * br
        acc = t if acc is None else acc + t
    o_ref[...] = acc.astype(o_ref.dtype)


def _merge(h, ys, w_gate, w_branch, tm):
    M, D = h.shape
    tn = 512
    ybs = pl.BlockSpec((tm, BRANCH_W), lambda m, n: (m, 0))
    return pl.pallas_call(
        _merge_kernel,
        out_shape=jax.ShapeDtypeStruct((M, D), BF16),
        grid=(M // tm, D // tn),
        in_specs=[pl.BlockSpec((tm, D), lambda m, n: (m, 0)), ybs, ybs, ybs, ybs,
                  pl.BlockSpec((N_BRANCH, D, tn), lambda m, n: (0, 0, n)),
                  pl.BlockSpec((N_BRANCH, BRANCH_W, tn), lambda m, n: (0, 0, n))],
        out_specs=pl.BlockSpec((tm, tn), lambda m, n: (m, n)),
        compiler_params=_cparams(("parallel", "arbitrary"), 48),
        name="merge",
    )(h, *ys, w_gate, w_branch)


def _outproj_kernel(acc_ref, x_ref, w_ref, g1_ref, ng_ref, sh_ref, sc_ref, wr_ref, br_ref,
                    xo_ref, h2_ref, aff_ref):
    out = jnp.dot(acc_ref[...], w_ref[...], preferred_element_type=F32)
    xn = x_ref[...] + g1_ref[...] * out
    xo_ref[...] = xn
    y = xn * lax.rsqrt(jnp.mean(xn * xn, axis=-1, keepdims=True) + RMS_EPS) * ng_ref[...]
    h2 = (y * (1.0 + sc_ref[...]) + sh_ref[...]).astype(BF16)
    h2_ref[...] = h2
    logits = lax.dot_general(wr_ref[...], h2, (((1,), (1,)), ((), ())), preferred_element_type=F32)
    logits = logits + br_ref[...]
    e = jnp.exp(logits - jnp.max(logits, axis=0, keepdims=True))
    aff = e / jnp.sum(e, axis=0, keepdims=True)
    for j in range(aff_ref.shape[0]):
        aff_ref[j] = aff[:, j * MOE_TILE:(j + 1) * MOE_TILE]


def _outproj(acc, x, w_out, g1, ng, sh, sc, w_router_t, b_router, tm):
    M, D = x.shape
    E = w_router_t.shape[0]
    vec = pl.BlockSpec((1, D), lambda m: (0, 0))
    return pl.pallas_call(
        _outproj_kernel,
        out_shape=(jax.ShapeDtypeStruct((M, D), F32),
                   jax.ShapeDtypeStruct((M, D), BF16),
                   jax.ShapeDtypeStruct((M // MOE_TILE, E, MOE_TILE), F32)),
        grid=(M // tm,),
        in_specs=[pl.BlockSpec((tm, D), lambda m: (m, 0)),
                  pl.BlockSpec((tm, D), lambda m: (m, 0)),
                  pl.BlockSpec((D, D), lambda m: (0, 0)),
                  vec, vec, vec, vec,
                  pl.BlockSpec((E, D), lambda m: (0, 0)),
                  pl.BlockSpec((E, 1), lambda m: (0, 0))],
        out_specs=(pl.BlockSpec((tm, D), lambda m: (m, 0)),
                   pl.BlockSpec((tm, D), lambda m: (m, 0)),
                   pl.BlockSpec((tm // MOE_TILE, E, MOE_TILE), lambda m: (m, 0, 0))),
        compiler_params=_cparams(("parallel",), 48),
        name="outproj",
    )(acc, x, w_out, g1, ng, sh, sc, w_router_t, b_router)


def _route_kernel(aff_ref, u_ref, l_ref, rank_ref, base_ref, tot_ref, *, cap):
    aff = aff_ref[...]
    nt, E, W = aff.shape

    def count(mask):
        return jnp.sum(jnp.sum(mask.astype(jnp.int32), axis=0), axis=1, keepdims=True)

    def search(it, thr):
        cand = thr | lax.shift_left(jnp.int32(1), 30 - it)
        c = count(aff >= lax.bitcast_convert_type(cand, F32)[None])
        return jnp.where(c >= cap, cand, thr)

    thr = lax.bitcast_convert_type(lax.fori_loop(0, 31, search, jnp.zeros((E, 1), jnp.int32)), F32)[None]
    gt = aff > thr
    eq = aff == thr
    need = (cap - count(gt)).astype(F32)[None]

    def prefix(mask):
        m2 = jnp.where(mask, 1.0, 0.0).astype(BF16).reshape(nt * E, W)
        incl = jnp.dot(m2, u_ref[...], preferred_element_type=F32)
        tot = jnp.broadcast_to(incl[:, W - 1:W], (nt * E, LANES))
        base = jnp.dot(l_ref[...], tot.astype(BF16), preferred_element_type=F32)
        return (incl - m2.astype(F32) + base[:, 0:1]).reshape(nt, E, W), base, tot

    eq_rank, _, _ = prefix(eq)
    sel = gt | (eq & (eq_rank < need))
    rank, base, tot = prefix(sel)
    rank_ref[...] = jnp.where(sel, rank.astype(jnp.int32), -1)
    base_ref[...] = base.astype(jnp.int32)
    tot_ref[...] = tot.astype(jnp.int32)


def _route(aff_tm, cap):
    nt, E, W = aff_tm.shape
    upper = np.triu(np.ones((W, W), np.float32))
    idx = np.arange(nt * E)
    lower = ((idx[:, None] % E) == (idx[None, :] % E)) & ((idx[None, :] // E) < (idx[:, None] // E))
    full = lambda a: pl.BlockSpec(a.shape, lambda: (0,) * a.ndim)
    consts = (jnp.asarray(upper, BF16), jnp.asarray(lower, BF16))
    return pl.pallas_call(
        functools.partial(_route_kernel, cap=cap),
        out_shape=(jax.ShapeDtypeStruct((nt, E, W), jnp.int32), jax.ShapeDtypeStruct((nt * E, LANES), jnp.int32),
                   jax.ShapeDtypeStruct((nt * E, LANES), jnp.int32)),
        in_specs=[full(aff_tm), full(consts[0]), full(consts[1])],
        out_specs=(pl.BlockSpec((nt, E, W), lambda: (0, 0, 0)), pl.BlockSpec((nt * E, LANES), lambda: (0, 0)),
                   pl.BlockSpec((nt * E, LANES), lambda: (0, 0))),
        compiler_params=pltpu.CompilerParams(vmem_limit_bytes=40 * MIB),
        name="moe_route",
    )(aff_tm, *consts)


def _slot_onehots(rank, cur, lo_ref, cnt_ref, b, R, clamp_hi):
    E = rank.shape[0]
    riota = lax.broadcasted_iota(jnp.int32, (R, rank.shape[1]), 0)
    ohs, starts, news = [], [], []
    for e in range(E):
        c = cur[e]
        end = lo_ref[b * E + e] + cnt_ref[b * E + e]
        w = (c // MOE_ALIGN) * MOE_ALIGN
        if clamp_hi is not None:
            w = jnp.minimum(w, clamp_hi)
        rk = rank[e:e + 1, :]
        ohs.append(((rk - w) == riota) & (rk >= c))
        starts.append(w)
        news.append(jnp.minimum(end, w + R))
    return ohs, starts, news


def _gather_kernel(lo_ref, cnt_ref, h_ref, rank_ref, aff_ref, xe_ref, xg_ref,
                   stage, gstage, carry, gcarry, cur, npass, sem, *, R, cap):
    b = pl.program_id(0)
    E = rank_ref.shape[1]
    A = MOE_ALIGN

    @pl.when(b == 0)
    def _():
        carry[...] = jnp.zeros_like(carry)
        gcarry[...] = jnp.zeros_like(gcarry)
        npass[0] = 0
        stage[1, 0:R, :] = jnp.zeros((R, stage.shape[2]), stage.dtype)
        gstage[1, 0:R, :] = jnp.zeros((R, LANES), F32)
        pads = []
        for e in range(E):
            pads.append(pltpu.make_async_copy(stage.at[1, pl.ds(0, R)], xe_ref.at[e, pl.ds(cap, R)], sem.at[1]))
            pads.append(pltpu.make_async_copy(gstage.at[1, pl.ds(0, R)], xg_ref.at[e, pl.ds(cap, R)], sem.at[1]))
        for cp in pads:
            cp.start()
        for cp in pads:
            cp.wait()

    for e in range(E):
        cur[e] = lo_ref[b * E + e]
    rank = rank_ref[0]
    aff = aff_ref[0]
    h = h_ref[...]
    srow = lax.broadcasted_iota(jnp.int32, (A, 1), 0)

    def copies(slot, starts):
        cps = []
        for e in range(E):
            w = pl.multiple_of(starts[e], A)
            cps.append(pltpu.make_async_copy(stage.at[slot, pl.ds(e * R, R)], xe_ref.at[e, pl.ds(w, R)], sem.at[slot]))
            cps.append(pltpu.make_async_copy(gstage.at[slot, pl.ds(e * R, R)], xg_ref.at[e, pl.ds(w, R)], sem.at[slot]))
        return cps

    def one_pass(more):
        g = npass[0]
        slot = g % 2
        ohs, starts, news = _slot_onehots(rank, cur, lo_ref, cnt_ref, b, R, None)
        rows = jnp.dot(jnp.concatenate([jnp.where(oh, 1.0, 0.0).astype(BF16) for oh in ohs], axis=0), h,
                       preferred_element_type=F32)
        more = jnp.int32(0)
        for e in range(E):
            gate = jnp.sum(jnp.where(ohs[e], aff[e:e + 1, :], 0.0), axis=1, keepdims=True)
            gate = jnp.broadcast_to(gate, (R, LANES))
            stage[slot, e * R:e * R + A, :] = (rows[e * R:e * R + A] + carry[e]).astype(stage.dtype)
            stage[slot, e * R + A:(e + 1) * R, :] = rows[e * R + A:(e + 1) * R].astype(stage.dtype)
            gstage[slot, e * R:e * R + A, :] = gate[:A] + gcarry[e]
            gstage[slot, e * R + A:(e + 1) * R, :] = gate[A:]
            new = news[e]
            grp = jnp.minimum((new - starts[e]) // A, R // A - 1)
            keep = srow < (new - (new // A) * A)
            off = pl.multiple_of(e * R + grp * A, A)
            carry[e] = jnp.where(keep, stage[slot, pl.ds(off, A), :].astype(F32), 0.0)
            gcarry[e] = jnp.where(keep, gstage[slot, pl.ds(off, A), :], 0.0)
            cur[e] = new
            more = jnp.maximum(more, (new < lo_ref[b * E + e] + cnt_ref[b * E + e]).astype(jnp.int32))

        @pl.when(g > 0)
        def _():
            for cp in copies(1 - slot, [0] * E):
                cp.wait()

        for cp in copies(slot, starts):
            cp.start()
        npass[0] = g + 1
        return more

    lax.while_loop(lambda more: more > 0, one_pass, jnp.int32(1))

    @pl.when(b == pl.num_programs(0) - 1)
    def _():
        for cp in copies((npass[0] - 1) % 2, [0] * E):
            cp.wait()


def _gather(h2, rank, aff_tm, lo, cnt, cap, R):
    n, D = h2.shape
    nt, E, W = rank.shape
    grid_spec = pltpu.PrefetchScalarGridSpec(
        num_scalar_prefetch=2,
        grid=(nt,),
        in_specs=[pl.BlockSpec((W, D), lambda b, lo, cnt: (b, 0)),
                  pl.BlockSpec((1, E, W), lambda b, lo, cnt: (b, 0, 0)),
                  pl.BlockSpec((1, E, W), lambda b, lo, cnt: (b, 0, 0))],
        out_specs=(pl.BlockSpec(memory_space=pl.ANY), pl.BlockSpec(memory_space=pl.ANY)),
        scratch_shapes=[pltpu.VMEM((2, E * R, D), BF16), pltpu.VMEM((2, E * R, LANES), F32),
                        pltpu.VMEM((E, MOE_ALIGN, D), F32), pltpu.VMEM((E, MOE_ALIGN, LANES), F32),
                        pltpu.SMEM((E,), jnp.int32), pltpu.SMEM((1,), jnp.int32),
                        pltpu.SemaphoreType.DMA((2,))])
    return pl.pallas_call(
        functools.partial(_gather_kernel, R=R, cap=cap),
        out_shape=(jax.ShapeDtypeStruct((E, cap + R, D), BF16), jax.ShapeDtypeStruct((E, cap + R, LANES), F32)),
        grid_spec=grid_spec,
        compiler_params=_cparams(("arbitrary",), 48),
        name="moe_gather",
    )(lo, cnt, h2, rank, aff_tm)


def _ffn_kernel(x_ref, g_ref, wg_ref, wu_ref, wd_ref, o_ref):
    x = x_ref[0]
    a = jnp.dot(x, wg_ref[0], preferred_element_type=F32)
    b = jnp.dot(x, wu_ref[0], preferred_element_type=F32)
    hid = (a * jax.nn.sigmoid(a) * b).astype(BF16)
    o_ref[0] = (jnp.dot(hid, wd_ref[0], preferred_element_type=F32) * g_ref[0][:, 0:1]).astype(o_ref.dtype)


def _ffn(xe, xg, wg, wu, wd, cap):
    E, _, D = xe.shape
    F = wg.shape[2]
    tm = min(cap, 512)
    return pl.pallas_call(
        _ffn_kernel,
        out_shape=jax.ShapeDtypeStruct((E, cap, D), BF16),
        grid=(E, cap // tm),
        in_specs=[pl.BlockSpec((1, tm, D), lambda e, m: (e, m, 0)),
                  pl.BlockSpec((1, tm, LANES), lambda e, m: (e, m, 0)),
                  pl.BlockSpec((1, D, F), lambda e, m: (e, 0, 0)),
                  pl.BlockSpec((1, D, F), lambda e, m: (e, 0, 0)),
                  pl.BlockSpec((1, F, D), lambda e, m: (e, 0, 0))],
        out_specs=pl.BlockSpec((1, tm, D), lambda e, m: (e, m, 0)),
        compiler_params=_cparams(("parallel", "arbitrary"), 52),
        name="expert_ffn",
    )(xe, xg, wg, wu, wd)


def _combine_kernel(lo_ref, cnt_ref, rank_ref, x_ref, g2_ref, y_ref, o_ref, ystage, yovf, acc, cur, sem, osem,
                    *, R, cap):
    b = pl.program_id(0)
    nb = pl.num_programs(0)
    E = rank_ref.shape[1]
    A = MOE_ALIGN

    def window(c):
        return pl.multiple_of(jnp.minimum((c // A) * A, cap - R), A)

    def main_copies(tile, slot):
        return [pltpu.make_async_copy(y_ref.at[e, pl.ds(window(lo_ref[tile * E + e]), R)],
                                      ystage.at[slot, pl.ds(e * R, R)], sem.at[slot]) for e in range(E)]

    @pl.when(b == 0)
    def _():
        for cp in main_copies(0, 0):
            cp.start()

    @pl.when(b + 1 < nb)
    def _():
        for cp in main_copies(b + 1, (b + 1) % 2):
            cp.start()

    for e in range(E):
        cur[e] = lo_ref[b * E + e]
    rank = rank_ref[0]

    def expand(y_rows):
        ohs, _, news = _slot_onehots(rank, cur, lo_ref, cnt_ref, b, R, cap - R)
        more = jnp.int32(0)
        for e in range(E):
            cur[e] = news[e]
            more = jnp.maximum(more, (news[e] < lo_ref[b * E + e] + cnt_ref[b * E + e]).astype(jnp.int32))
        oh = jnp.concatenate([jnp.where(o, 1.0, 0.0).astype(BF16) for o in ohs], axis=0)
        return lax.dot_general(oh, y_rows, (((0,), (0,)), ((), ())), preferred_element_type=F32), more

    for cp in main_copies(b, b % 2):
        cp.wait()
    first, more = expand(ystage[b % 2])
    acc[...] = first

    def extra_pass(more):
        cps = [pltpu.make_async_copy(y_ref.at[e, pl.ds(window(cur[e]), R)], yovf.at[pl.ds(e * R, R)], osem.at[0])
               for e in range(E)]
        for cp in cps:
            cp.start()
        for cp in cps:
            cp.wait()
        part, more = expand(yovf[...])
        acc[...] += part
        return more

    lax.while_loop(lambda more: more > 0, extra_pass, more)
    o_ref[...] = x_ref[...] + g2_ref[...] * acc[...]


def _combine(y, rank, lo, cnt, x, g2, cap, R):
    n, D = x.shape
    nt, E, W = rank.shape
    grid_spec = pltpu.PrefetchScalarGridSpec(
        num_scalar_prefetch=2,
        grid=(nt,),
        in_specs=[pl.BlockSpec((1, E, W), lambda b, lo, cnt: (b, 0, 0)),
                  pl.BlockSpec((W, D), lambda b, lo, cnt: (b, 0)),
                  pl.BlockSpec((1, D), lambda b, lo, cnt: (0, 0)),
                  pl.BlockSpec(memory_space=pl.ANY)],
        out_specs=pl.BlockSpec((W, D), lambda b, lo, cnt: (b, 0)),
        scratch_shapes=[pltpu.VMEM((2, E * R, D), BF16), pltpu.VMEM((E * R, D), BF16), pltpu.VMEM((W, D), F32),
                        pltpu.SMEM((E,), jnp.int32), pltpu.SemaphoreType.DMA((2,)), pltpu.SemaphoreType.DMA((1,))])
    return pl.pallas_call(
        functools.partial(_combine_kernel, R=R, cap=cap),
        out_shape=jax.ShapeDtypeStruct((n, D), F32),
        grid_spec=grid_spec,
        compiler_params=_cparams(("arbitrary",), 48),
        name="moe_combine",
    )(lo, cnt, rank, x, g2, y)


def _moe(x, h2, aff_tm, g2, wg, wu, wd):
    n = h2.shape[0]
    E = aff_tm.shape[1]
    cap = EC_FACTOR * n // E
    R = min(MOE_WIN, cap)
    rank, base, tot = _route(aff_tm, cap)
    lo, cnt = base[:, 0], tot[:, 0]
    xe, xg = _gather(h2, rank, aff_tm, lo, cnt, cap, R)
    y = _ffn(xe, xg, wg, wu, wd, cap)
    return _combine(y, rank, lo, cnt, x, g2, cap, R)


def _final_norm_kernel(x_ref, g_ref, o_ref):
    x = x_ref[...]
    o_ref[...] = x * lax.rsqrt(jnp.mean(x * x, axis=-1, keepdims=True) + RMS_EPS) * g_ref[...]


def _final_norm(x, g):
    M, D = x.shape
    tm = 512
    return pl.pallas_call(
        _final_norm_kernel,
        out_shape=jax.ShapeDtypeStruct((M, D), F32),
        grid=(M // tm,),
        in_specs=[pl.BlockSpec((tm, D), lambda m: (m, 0)), pl.BlockSpec((1, D), lambda m: (0, 0))],
        out_specs=pl.BlockSpec((tm, D), lambda m: (m, 0)),
        compiler_params=_cparams(("parallel",), 32),
        name="final_norm",
    )(x, g)


POOL_HALO = 16


def _pool_kernel(cur_ref, prev_ref, next_ref, w_ref, sc_ref, y_ref, scr, *, seq_len):
    m = pl.program_id(0)
    tm = cur_ref.shape[0]
    H = POOL_HALO
    scr[0:H, :] = jnp.where(m > 0, prev_ref[...].astype(F32), 0.0)
    scr[H:H + tm, :] = cur_ref[...].astype(F32)
    scr[H + tm:2 * H + tm, :] = jnp.where(m < pl.num_programs(0) - 1, next_ref[...].astype(F32), 0.0)
    t = m * tm + lax.broadcasted_iota(jnp.int32, (tm, 1), 0)
    for gi, w in enumerate(POOL_WINDOWS):
        cols = slice(gi * POOL_GROUP_W, (gi + 1) * POOL_GROUP_W)
        total = None
        for k in range(-(w // 2), w // 2):
            term = scr[H + k:H + k + tm, cols]
            total = term if total is None else total + term
        n = jnp.minimum(t + (w // 2 - 1), seq_len - 1) - jnp.maximum(t - w // 2, 0) + 1
        centred = total / n.astype(F32) - scr[H:H + tm, cols]
        y = jnp.dot(centred.astype(BF16), w_ref[gi].astype(BF16), preferred_element_type=F32)
        y_ref[:, cols] = (y * sc_ref[:, cols]).astype(y_ref.dtype)


def _pool(proj, pool_w, pool_scale, tm):
    M = proj.shape[0]
    hb = tm // POOL_HALO
    nhb = M // POOL_HALO
    cb = COL_P // BRANCH_W
    return pl.pallas_call(
        functools.partial(_pool_kernel, seq_len=M),
        out_shape=jax.ShapeDtypeStruct((M, BRANCH_W), BF16),
        grid=(M // tm,),
        in_specs=[pl.BlockSpec((tm, BRANCH_W), lambda m: (m, cb)),
                  pl.BlockSpec((POOL_HALO, BRANCH_W), lambda m: (jnp.maximum(m * hb - 1, 0), cb)),
                  pl.BlockSpec((POOL_HALO, BRANCH_W), lambda m: (jnp.minimum((m + 1) * hb, nhb - 1), cb)),
                  pl.BlockSpec(pool_w.shape, lambda m: (0, 0, 0)),
                  pl.BlockSpec((1, BRANCH_W), lambda m: (0, 0))],
        out_specs=pl.BlockSpec((tm, BRANCH_W), lambda m: (m, 0)),
        scratch_shapes=[pltpu.VMEM((tm + 2 * POOL_HALO, BRANCH_W), F32)],
        compiler_params=_cparams(("parallel",), 32),
        name="pool_mix",
    )(proj, proj, proj, pool_w, pool_scale)


def _gelu(x):
    return 0.5 * x * (1.0 + lax.erf(x * (2.0 ** -0.5)))


def _sgu_kernel(u_ref, v_ref, g_ref, w_ref, b_ref, y_ref):
    tm = u_ref.shape[0]
    v = _gelu(v_ref[...].astype(F32))
    v = (v * lax.rsqrt(jnp.mean(v * v, axis=-1, keepdims=True) + RMS_EPS) * g_ref[...]).astype(BF16)
    u = _gelu(u_ref[...].astype(F32))
    for c in range(tm // SGU_CHUNK):
        rows = slice(c * SGU_CHUNK, (c + 1) * SGU_CHUNK)
        for g in range(SGU_GROUPS):
            cols = slice(g * SGU_GROUP_W, (g + 1) * SGU_GROUP_W)
            mixed = jnp.dot(w_ref[g].astype(BF16), v[rows, cols], preferred_element_type=F32) + b_ref[:, g:g + 1]
            y_ref[rows, cols] = (u[rows, cols] * mixed).astype(y_ref.dtype)


def _sgu(proj, norm_g, w_s, b_s, tm):
    M = proj.shape[0]
    bt = jnp.pad(b_s.T, ((0, 0), (0, LANES - SGU_GROUPS)))
    return pl.pallas_call(
        _sgu_kernel,
        out_shape=jax.ShapeDtypeStruct((M, BRANCH_W), BF16),
        grid=(M // tm,),
        in_specs=[pl.BlockSpec((tm, BRANCH_W), lambda m: (m, COL_GU // BRANCH_W)),
                  pl.BlockSpec((tm, BRANCH_W), lambda m: (m, COL_GV // BRANCH_W)),
                  pl.BlockSpec((1, BRANCH_W), lambda m: (0, 0)),
                  pl.BlockSpec(w_s.shape, lambda m: (0, 0, 0)),
                  pl.BlockSpec((SGU_CHUNK, LANES), lambda m: (0, 0))],
        out_specs=pl.BlockSpec((tm, BRANCH_W), lambda m: (m, 0)),
        compiler_params=_cparams(("parallel",), 32),
        name="sgu_mix",
    )(proj, proj, norm_g, w_s, bt)


DN_CONV = 5
DN_HALO = 16
DN_QKV_W = 3 * BRANCH_W
DN_STATES = 2 * DN_HEADS


def _dn_prep_kernel(cur_ref, prev_ref, next_ref, ab_ref, cw_ref, alog_ref, dtb_ref,
                    q_ref, k_ref, v_ref, aux_ref, scr):
    m = pl.program_id(0)
    tm = cur_ref.shape[0]
    scr[0:DN_HALO, :] = jnp.where(m > 0, prev_ref[...].astype(F32), 0.0)
    scr[DN_HALO:DN_HALO + tm, :] = cur_ref[...].astype(F32)
    scr[DN_HALO + tm:2 * DN_HALO + tm, :] = jnp.where(m < pl.num_programs(0) - 1, next_ref[...].astype(F32), 0.0)
    acc = None
    for t in range(DN_CONV):
        lo = DN_HALO - DN_CONV // 2 + t
        term = scr[lo:lo + tm, :] * cw_ref[t:t + 1, :]
        acc = term if acc is None else acc + term
    y = acc * jax.nn.sigmoid(acc)
    for h in range(DN_HEADS):
        qh = y[:, h * DN_DK:(h + 1) * DN_DK]
        q_ref[:, h * DN_DK:(h + 1) * DN_DK] = (
            qh * lax.rsqrt(jnp.sum(qh * qh, axis=-1, keepdims=True) + 1e-6) * DN_DK ** -0.5)
        kh = y[:, BRANCH_W + h * DN_DK:BRANCH_W + (h + 1) * DN_DK]
        k_ref[:, h * DN_DK:(h + 1) * DN_DK] = kh * lax.rsqrt(jnp.sum(kh * kh, axis=-1, keepdims=True) + 1e-6)
    v_ref[...] = y[:, 2 * BRANCH_W:]

    ab = ab_ref[...]
    lane = lax.broadcasted_iota(jnp.int32, ab.shape, 1)
    sp = ab + dtb_ref[...]
    softplus = jnp.maximum(sp, 0.0) + jnp.log(1.0 + jnp.exp(-jnp.abs(sp)))
    g = jnp.where(lane < DN_STATES, -jnp.exp(alog_ref[...]) * softplus, 0.0)
    beta = jax.nn.sigmoid(ab)
    g1 = g.astype(BF16)
    r1 = g - g1.astype(F32)
    g2 = r1.astype(BF16)
    g3 = (r1 - g2.astype(F32)).astype(BF16)
    gs = jnp.concatenate([g1, g2, g3], axis=1)
    row = lax.broadcasted_iota(jnp.int32, (tm, tm), 0)
    col = lax.broadcasted_iota(jnp.int32, (tm, tm), 1)
    same = (row // DN_CHUNK) == (col // DN_CHUNK)
    tri_f = jnp.where(same & (col <= row), 1.0, 0.0).astype(BF16)
    tri_b = jnp.where(same & (col >= row), 1.0, 0.0).astype(BF16)
    cf = jnp.dot(tri_f, gs, preferred_element_type=F32)
    cb = jnp.dot(tri_b, gs, preferred_element_type=F32)
    cf = cf[:, :LANES] + cf[:, LANES:2 * LANES] + cf[:, 2 * LANES:]
    cb = cb[:, :LANES] + cb[:, LANES:2 * LANES] + cb[:, 2 * LANES:]
    gc = jnp.where(lane < DN_HEADS, cf, cb)
    aux_ref[...] = jnp.where(lane < DN_STATES, gc, beta)


def _dn_prep(proj, ab, cw, alog, dtb, tm):
    M = proj.shape[0]
    hb = tm // DN_HALO
    nhb = M // DN_HALO
    row = pl.BlockSpec((1, LANES), lambda m: (0, 0))
    return pl.pallas_call(
        _dn_prep_kernel,
        out_shape=(jax.ShapeDtypeStruct((M, BRANCH_W), F32),) * 3 + (jax.ShapeDtypeStruct((M, LANES), F32),),
        grid=(M // tm,),
        in_specs=[pl.BlockSpec((tm, DN_QKV_W), lambda m: (m, 0)),
                  pl.BlockSpec((DN_HALO, DN_QKV_W), lambda m: (jnp.maximum(m * hb - 1, 0), 0)),
                  pl.BlockSpec((DN_HALO, DN_QKV_W), lambda m: (jnp.minimum((m + 1) * hb, nhb - 1), 0)),
                  pl.BlockSpec((tm, LANES), lambda m: (m, 0)),
                  pl.BlockSpec((8, DN_QKV_W), lambda m: (0, 0)), row, row],
        out_specs=(pl.BlockSpec((tm, BRANCH_W), lambda m: (m, 0)),) * 3 + (pl.BlockSpec((tm, LANES), lambda m: (m, 0)),),
        scratch_shapes=[pltpu.VMEM((tm + 2 * DN_HALO, DN_QKV_W), F32)],
        compiler_params=_cparams(("parallel",), 40),
        name="dn_prep",
    )(proj, proj, proj, ab, cw, alog, dtb)


def _dn_constants():
    L, H = DN_CHUNK, DN_HEADS
    WW, FW = H * L, H * DN_DK
    r = np.arange(L)[:, None]
    c = np.arange(WW)[None, :] % L
    cm = np.zeros((2, 9, L, WW), np.float32)
    for d in range(2):
        lower = d == 0
        cm[d, 0] = r == c
        cm[d, 1] = np.where((r >= c) if lower else (r <= c), 0.0, -np.inf)
        cm[d, 2] = (r > c) if lower else (r < c)
        for j in range(6):
            s = 2 ** j
            hi, lo = (r, c) if lower else (c, r)
            cm[d, 3 + j] = ((r // (2 * s)) == (c // (2 * s))) & (hi % (2 * s) >= s) & (lo % (2 * s) < s)
    rw = np.arange(WW)[:, None]
    rf = np.arange(FW)[:, None]
    m_ww = rw // L == np.arange(WW)[None] // L
    m_wf = rw // L == np.arange(FW)[None] // DN_DK
    m_w2f = rw // L == (np.arange(2 * FW)[None] % FW) // DN_DK
    m_ff = rf // DN_DK == np.arange(FW)[None] // DN_DK
    return (jnp.asarray(cm),) + tuple(jnp.asarray(m, BF16) for m in (m_ww, m_wf, m_w2f, m_ff))


def _dn_scan_kernel(qf_ref, kf_ref, vf_ref, af_ref, qb_ref, kb_ref, vb_ref, ab_ref, s0_ref,
                    cm_ref, mww_ref, mwf_ref, mw2f_ref, mff_ref, of_ref, ob_ref, sfin_ref, s_scr):
    i = pl.program_id(0)
    L, H = DN_CHUNK, DN_HEADS
    FW = H * DN_DK

    @pl.when(i == 0)
    def _():
        s_scr[...] = s0_ref[...]

    lane = lax.broadcasted_iota(jnp.int32, (L, LANES), 1)
    rowblk = lax.broadcasted_iota(jnp.int32, (FW, DN_DV), 0) // DN_DK
    m_ww, m_wf, m_w2f, m_ff = mww_ref[...], mwf_ref[...], mw2f_ref[...], mff_ref[...]

    def tile_rows(a):
        return jnp.concatenate([a] * H, axis=0)

    def col(arr, c):
        return jnp.broadcast_to(arr[:, c:c + 1], (arr.shape[0], LANES))

    def feat(arr, base):
        return jnp.concatenate([col(arr, base + h) for h in range(H)], axis=1)

    def wide(arr, base):
        return jnp.concatenate([jnp.where(lane < L, col(arr, base + 2 * p), col(arr, base + 2 * p + 1))
                                for p in range(H // 2)], axis=1)

    results = []
    for d, (q_ref, k_ref, v_ref, a_ref, o_ref) in enumerate(((qf_ref, kf_ref, vf_ref, af_ref, of_ref),
                                                              (qb_ref, kb_ref, vb_ref, ab_ref, ob_ref))):
        aux = a_ref[...]
        last = aux[L - 1:L, :] if d == 0 else aux[0:1, :]
        e_gc = jnp.exp(aux)
        e_rem = jnp.exp(last - aux)
        e_last = jnp.exp(last)
        c0 = d * H
        beta_f = feat(aux, DN_STATES + c0)
        eg_f = feat(e_gc, c0)
        er_f = feat(e_rem, c0)
        gcw = wide(aux, c0)
        eye = cm_ref[d, 0]
        gc_row = jnp.sum(eye * gcw, axis=0, keepdims=True)
        decay = jnp.exp(gcw - gc_row + cm_ref[d, 1])
        q = q_ref[...]
        k = k_ref[...]
        v = v_ref[...]
        kbeta = k * beta_f
        kq = lax.dot_general(jnp.concatenate([kbeta, q], axis=0).astype(BF16), tile_rows(k.astype(BF16)) * m_wf,
                             (((1,), (1,)), ((), ())), preferred_element_type=F32)
        mm = kq[:L] * decay * cm_ref[d, 2]
        qk = (kq[L:] * decay).astype(BF16)
        x = eye - mm * cm_ref[d, 3]
        for j in range(4, 9):
            xb = x.astype(BF16)
            t = jnp.dot(xb, tile_rows((mm * cm_ref[d, j]).astype(BF16)) * m_ww, preferred_element_type=F32)
            x = x - jnp.dot(t.astype(BF16), tile_rows(xb) * m_ww, preferred_element_type=F32)
        rhs = jnp.concatenate([v * beta_f, kbeta * eg_f], axis=1)
        sol = rhs + jnp.dot((x - eye).astype(BF16), tile_rows(rhs.astype(BF16)) * m_w2f,
                            preferred_element_type=F32)
        u = sol[:, :FW]
        w = sol[:, FW:]
        s_old = s_scr[d]
        s_bd = jnp.concatenate([s_old.astype(BF16)] * H, axis=1) * m_ff
        ws_qs = jnp.dot(jnp.concatenate([w, q * eg_f], axis=0).astype(BF16), s_bd, preferred_element_type=F32)
        v_new = (u - ws_qs[:L]).astype(BF16)
        o = ws_qs[L:] + jnp.dot(qk, tile_rows(v_new) * m_wf, preferred_element_type=F32)
        cross = lax.dot_general((k * er_f).astype(BF16), v_new, (((0,), (0,)), ((), ())),
                                preferred_element_type=F32)
        fold = None
        g_last = None
        for h in range(H):
            part = jnp.where(rowblk == h, cross[:, h * DN_DV:(h + 1) * DN_DV], 0.0)
            gl = jnp.where(rowblk == h, e_last[:, c0 + h:c0 + h + 1], 0.0)
            fold = part if fold is None else fold + part
            g_last = gl if g_last is None else g_last + gl
        results.append((d, o_ref, g_last * s_old + fold, o))

    for d, o_ref, s_new, o in results:
        s_scr[d] = s_new
        o_ref[...] = o

    @pl.when(i == pl.num_programs(0) - 1)
    def _():
        sfin_ref[...] = s_scr[...]


def _dn_scan(q, k, v, aux, s0, consts):
    M = q.shape[0]
    N = M // DN_CHUNK
    fwd = lambda i: (i, 0)
    bwd = lambda i: (N - 1 - i, 0)
    cs = lambda w, im: pl.BlockSpec((DN_CHUNK, w), im)
    st = pl.BlockSpec(s0.shape, lambda i: (0, 0, 0))
    const_specs = [pl.BlockSpec(c.shape, lambda i, _n=c.ndim: (0,) * _n) for c in consts]
    return pl.pallas_call(
        _dn_scan_kernel,
        out_shape=(jax.ShapeDtypeStruct((M, BRANCH_W), F32), jax.ShapeDtypeStruct((M, BRANCH_W), F32),
                   jax.ShapeDtypeStruct(s0.shape, F32)),
        grid=(N,),
        in_specs=[cs(BRANCH_W, fwd), cs(BRANCH_W, fwd), cs(BRANCH_W, fwd), cs(LANES, fwd),
                  cs(BRANCH_W, bwd), cs(BRANCH_W, bwd), cs(BRANCH_W, bwd), cs(LANES, bwd), st] + const_specs,
        out_specs=(cs(BRANCH_W, fwd), cs(BRANCH_W, bwd), st),
        scratch_shapes=[pltpu.VMEM(s0.shape, F32)],
        compiler_params=_cparams(("arbitrary",), 32),
        name="dn_scan",
    )(q, k, v, aux, q, k, v, aux, s0, *consts)


def _dn_out_kernel(of_ref, ob_ref, z_ref, g_ref, y_ref):
    o = of_ref[...] + ob_ref[...]
    z = z_ref[...].astype(F32)
    for h in range(DN_HEADS):
        sl = slice(h * DN_DV, (h + 1) * DN_DV)
        oh = o[:, sl]
        zh = z[:, sl]
        yh = oh * lax.rsqrt(jnp.mean(oh * oh, axis=-1, keepdims=True) + RMS_EPS) * g_ref[...]
        y_ref[:, sl] = (yh * (zh * jax.nn.sigmoid(zh))).astype(y_ref.dtype)


def _dn_out(o_f, o_b, proj, norm_g, tm):
    M = o_f.shape[0]
    blk = pl.BlockSpec((tm, BRANCH_W), lambda m: (m, 0))
    return pl.pallas_call(
        _dn_out_kernel,
        out_shape=jax.ShapeDtypeStruct((M, BRANCH_W), BF16),
        grid=(M // tm,),
        in_specs=[blk, blk, pl.BlockSpec((tm, BRANCH_W), lambda m: (m, COL_DZ // BRANCH_W)),
                  pl.BlockSpec((1, DN_DV), lambda m: (0, 0))],
        out_specs=blk,
        compiler_params=_cparams(("parallel",), 32),
        name="dn_out",
    )(o_f, o_b, proj, norm_g)


def _deltanet(proj, ab, projc, abc, conv_w, a_log, dt_bias, norm_g, want_ctx, tm):
    cw = jnp.pad(conv_w, ((0, 8 - DN_CONV), (0, 0)))
    alog = jnp.pad(a_log.reshape(1, DN_STATES), ((0, 0), (0, LANES - DN_STATES)))
    dtb = jnp.pad(dt_bias.reshape(1, DN_STATES), ((0, 0), (0, LANES - DN_STATES)))
    ql, kl, vl, auxl = _dn_prep(proj, ab, cw, alog, dtb, tm)
    qx, kx, vx, auxx = _dn_prep(projc, abc, cw, alog, dtb, projc.shape[0])
    consts = _dn_constants()
    s0 = jnp.zeros((2, DN_HEADS * DN_DK, DN_DV), F32)
    oc_f, oc_b, s1 = _dn_scan(qx, kx, vx, auxx, s0, consts)
    o_f, o_b, _ = _dn_scan(ql, kl, vl, auxl, s1, consts)
    y = _dn_out(o_f, o_b, proj, norm_g[None], tm)
    yc = _dn_out(oc_f, oc_b, projc, norm_g[None], projc.shape[0]) if want_ctx else None
    return y, yc


SWA_PREP_W = 8 * LANES


def _rope_tables(T):
    half = SWA_HD // 4
    inv = ROPE_BASE ** (-jnp.arange(half, dtype=F32) / half)
    t = jnp.arange(T, dtype=jnp.int32)
    ang_r = (t // GRID_W).astype(F32)[:, None] * inv
    ang_c = (t % GRID_W).astype(F32)[:, None] * inv
    cos = jnp.concatenate([jnp.cos(ang_r)] * 2 + [jnp.cos(ang_c)] * 2, axis=1)
    sin = jnp.concatenate([-jnp.sin(ang_r), jnp.sin(ang_r), -jnp.sin(ang_c), jnp.sin(ang_c)], axis=1)
    return jnp.tile(cos, (1, 2)), jnp.tile(sin, (1, 2))


def _swa_prep_kernel(q_ref, kv_ref, cos_ref, sin_ref, qo_ref, kvo_ref, *, rope):
    lane = lax.broadcasted_iota(jnp.int32, (q_ref.shape[0], LANES), 1)
    first = (lane % (SWA_HD // 2)) < (SWA_HD // 4)
    low = lane < SWA_HD

    def rot(x):
        if not rope:
            return x
        partner = jnp.where(first, pltpu.roll(x, LANES - SWA_HD // 4, 1), pltpu.roll(x, SWA_HD // 4, 1))
        return x * cos_ref[...] + partner * sin_ref[...]

    for c in range(q_ref.shape[1] // LANES):
        cols = slice(c * LANES, (c + 1) * LANES)
        qo_ref[:, cols] = (rot(q_ref[:, cols].astype(F32)) * SWA_HD ** -0.5).astype(qo_ref.dtype)
    kv = kv_ref[...].astype(F32)
    for j, x in enumerate((rot(kv[:, :LANES]), kv[:, LANES:])):
        sw = pltpu.roll(x, SWA_HD, 1)
        parts = (jnp.where(low, x, 0.0), jnp.where(low, 0.0, sw), jnp.where(low, sw, 0.0), jnp.where(low, 0.0, x))
        for i, part in enumerate(parts):
            kvo_ref[:, (4 * j + i) * LANES:(4 * j + i + 1) * LANES] = part.astype(kvo_ref.dtype)


def _swa_prep(proj, cos, sin, rope, tm):
    M = proj.shape[0]
    tab = pl.BlockSpec((tm, LANES), lambda m: (m, 0))
    return pl.pallas_call(
        functools.partial(_swa_prep_kernel, rope=rope),
        out_shape=(jax.ShapeDtypeStruct((M, BRANCH_W), BF16), jax.ShapeDtypeStruct((M, SWA_PREP_W), BF16)),
        grid=(M // tm,),
        in_specs=[pl.BlockSpec((tm, BRANCH_W), lambda m: (m, COL_SQ // BRANCH_W)),
                  pl.BlockSpec((tm, 2 * LANES), lambda m: (m, COL_SK // (2 * LANES))), tab, tab],
        out_specs=(pl.BlockSpec((tm, BRANCH_W), lambda m: (m, 0)), pl.BlockSpec((tm, SWA_PREP_W), lambda m: (m, 0))),
        compiler_params=_cparams(("parallel",), 32),
        name="swa_prep",
    )(proj, proj, cos, sin)


def _swa_masks(n_ctx):
    nk = 3 * SWA_BLOCK + n_ctx
    i = np.arange(SWA_BLOCK)[:, None]
    j = np.arange(nk)[None, :]
    lat = j < 3 * SWA_BLOCK
    band = ~lat | (np.abs(j - SWA_BLOCK - i) <= SWA_WINDOW)
    no_prev = ~(j < SWA_BLOCK) | (i < 0)
    no_next = ~((j >= 2 * SWA_BLOCK) & lat) | (i < 0)
    ctx_only = ~lat | (i < 0)
    m = np.stack([np.where(ok, 0.0, -np.inf) for ok in (band, no_prev, no_next, ctx_only)]).astype(np.float32)
    return jnp.asarray(np.concatenate([m, m], axis=2))


def _swa_attn_kernel(sink_ref, q_ref, kvp_ref, kvo_ref, kvn_ref, kvc_ref, m_ref, y_ref, *, latent_keys):
    n = pl.program_id(0)
    kv = jnp.concatenate([kvp_ref[...], kvo_ref[...], kvn_ref[...], kvc_ref[...]], axis=0)
    nk = kv.shape[0]
    if latent_keys:
        mask = (m_ref[0] + jnp.where(n == 0, m_ref[1], 0.0) + jnp.where(n == pl.num_programs(0) - 1, m_ref[2], 0.0))
    else:
        mask = m_ref[3]
    lane = lax.broadcasted_iota(jnp.int32, (q_ref.shape[0], LANES), 1)
    for c in range(SWA_Q_HEADS // 2):
        kh = (2 * c) // SWA_GROUP
        cols = slice(c * LANES, (c + 1) * LANES)
        kk = jnp.concatenate([kv[:, (2 * kh) * LANES:(2 * kh + 1) * LANES],
                              kv[:, (2 * kh + 1) * LANES:(2 * kh + 2) * LANES]], axis=0)
        vv = jnp.concatenate([kv[:, (4 + 2 * kh) * LANES:(5 + 2 * kh) * LANES],
                              kv[:, (5 + 2 * kh) * LANES:(6 + 2 * kh) * LANES]], axis=0)
        s = lax.dot_general(q_ref[:, cols], kk, (((1,), (1,)), ((), ())), preferred_element_type=F32) + mask
        ps, dens = [], []
        for half in range(2):
            sh = s[:, half * nk:(half + 1) * nk]
            sink = sink_ref[2 * c + half]
            mx = jnp.maximum(jnp.max(sh, axis=1, keepdims=True), sink)
            p = jnp.exp(sh - mx)
            dens.append(jnp.sum(p, axis=1, keepdims=True) + jnp.exp(sink - mx))
            ps.append(p.astype(BF16))
        o = jnp.dot(jnp.concatenate(ps, axis=1), vv, preferred_element_type=F32)
        y_ref[:, cols] = (o / jnp.where(lane < SWA_HD, dens[0], dens[1])).astype(y_ref.dtype)


def _swa_attn(q, kvr, kvc, sink, masks, latent_keys):
    M = q.shape[0]
    NB = M // SWA_BLOCK
    blk = lambda im: pl.BlockSpec((SWA_BLOCK, SWA_PREP_W), im)
    return pl.pallas_call(
        functools.partial(_swa_attn_kernel, latent_keys=latent_keys),
        out_shape=jax.ShapeDtypeStruct((M, BRANCH_W), BF16),
        grid=(NB,),
        in_specs=[pl.BlockSpec(memory_space=pltpu.SMEM),
                  pl.BlockSpec((SWA_BLOCK, BRANCH_W), lambda n: (n, 0)),
                  blk(lambda n: (jnp.maximum(n - 1, 0), 0)), blk(lambda n: (n, 0)),
                  blk(lambda n: (jnp.minimum(n + 1, NB - 1), 0)),
                  pl.BlockSpec(kvc.shape, lambda n: (0, 0)),
                  pl.BlockSpec(masks.shape, lambda n: (0, 0, 0))],
        out_specs=pl.BlockSpec((SWA_BLOCK, BRANCH_W), lambda n: (n, 0)),
        compiler_params=_cparams(("parallel",), 32),
        name="swa_attn",
    )(sink, q, kvr, kvr, kvr, kvc, masks)


def _swa(proj, projc, sink, want_ctx, tm):
    T, Lc = proj.shape[0], projc.shape[0]
    cos, sin = _rope_tables(T)
    q, kvr = _swa_prep(proj, cos, sin, True, tm)
    none = jnp.zeros((Lc, LANES), F32)
    qc, kvc = _swa_prep(projc, none, none, False, Lc)
    masks = _swa_masks(Lc)
    y = _swa_attn(q, kvr, kvc, sink, masks, True)
    yc = _swa_attn(qc, kvc, kvc, sink, masks, False) if want_ctx else None
    return y, yc


def kernel(x, c, ctx, c_ctx, w_ada, b_ada, norm1_g, norm2_g, w_in, pool_w, pool_scale, dn_conv_w, dn_a_log, dn_dt_bias, dn_norm_g, swa_sink, sgu_norm_g, sgu_w, sgu_b, w_gate, w_branch, w_out, w_router, b_router, w_e_gate, w_e_up, w_e_down, final_g):
    B, T, D = x.shape
    assert B == 1 and D == D_MODEL
    Lc = ctx.shape[1]
    depth = w_ada.shape[0]

    cond8 = jnp.zeros((8, D), F32).at[0].set(c[0]).at[1].set(c_ctx)
    mods = _ada(cond8, w_ada, b_ada)

    xl = x[0]
    xc = ctx[0]
    tm_l = 512
    tm_c = Lc
    for i in range(depth):
        want_ctx = i < depth - 1
        ml = [mods[i, 0:1, j * D:(j + 1) * D] for j in range(6)]
        mc = [mods[i, 1:2, j * D:(j + 1) * D] for j in range(6)]
        n1 = norm1_g[i][None]
        n2 = norm2_g[i][None]
        wi = w_in[i]
        o = (0,) + IN_OFFSETS + (sum(IN_SPLITS),)
        w_main = jnp.concatenate([wi[:, o[1]:o[5]], wi[:, o[0]:o[1]], wi[:, o[7]:o[8]], wi[:, o[10]:o[12]],
                                  wi[:, o[8]:o[10]]], axis=1).astype(BF16)
        w_ab = jnp.pad(wi[:, AB_LO:AB_LO + AB_W], ((0, 0), (0, LANES - AB_W))).astype(BF16)
        wgt = w_gate[i].astype(BF16)
        wbr = w_branch[i].astype(BF16)
        wo = w_out[i].astype(BF16)
        wr_t = w_router[i].T.astype(BF16)
        br = b_router[i][:, None]
        weg = w_e_gate[i].astype(BF16)
        weu = w_e_up[i].astype(BF16)
        wed = w_e_down[i].astype(BF16)

        proj, ab, h = _inproj(xl, n1, ml[0], ml[1], w_main, w_ab, tm_l)
        projc, abc, hc = _inproj(xc, n1, mc[0], mc[1], w_main, w_ab, tm_c)
        y_pool = _pool(proj, pool_w[i], pool_scale[i][None], tm_l)
        y_dn, yc_dn = _deltanet(proj, ab, projc, abc, dn_conv_w[i], dn_a_log[i], dn_dt_bias[i], dn_norm_g[i],
                                want_ctx, tm_l)
        y_swa, yc_swa = _swa(proj, projc, swa_sink[i], want_ctx, tm_l)
        y_sgu = _sgu(proj, sgu_norm_g[i][None], sgu_w[i], sgu_b[i], tm_l)
        ys = (y_pool, y_dn, y_swa, y_sgu)

        acc = _merge(h, ys, wgt, wbr, tm_l)
        xl, h2, aff_t = _outproj(acc, xl, wo, ml[2], n2, ml[3], ml[4], wr_t, br, tm_l)
        xl = _moe(xl, h2, aff_t, ml[5], weg, weu, wed)

        if want_ctx:
            yc_pool = _pool(projc, pool_w[i], pool_scale[i][None], tm_c)
            yc_sgu = _sgu(projc, sgu_norm_g[i][None], sgu_w[i], sgu_b[i], tm_c)
            ycs = (yc_pool, yc_dn, yc_swa, yc_sgu)
            accc = _merge(hc, ycs, wgt, wbr, tm_c)
            xc, hc2, affc_t = _outproj(accc, xc, wo, mc[2], n2, mc[3], mc[4], wr_t, br, tm_c)
            xc = _moe(xc, hc2, affc_t, mc[5], weg, weu, wed)

    return _final_norm(xl, final_g[None])[None]
```

```python
import functools
import math

import numpy as np
import jax
import jax.numpy as jnp
from jax import lax
from jax.experimental import pallas as pl
from jax.experimental.pallas import tpu as pltpu

F32 = jnp.float32
BF16 = jnp.bfloat16

D_MODEL = 2048
GRID_W = 64
N_BRANCH = 4
BRANCH_W = D_MODEL // N_BRANCH
POOL_WINDOWS = (2, 4, 8, 16)
POOL_GROUP_W = BRANCH_W // len(POOL_WINDOWS)
DN_DK = 128
DN_DV = 128
DN_HEADS = BRANCH_W // DN_DV
DN_CHUNK = 64
SWA_HD = 64
SWA_Q_HEADS = BRANCH_W // SWA_HD
SWA_KV_HEADS = SWA_Q_HEADS // 4
SWA_GROUP = SWA_Q_HEADS // SWA_KV_HEADS
SWA_WINDOW = 128
SWA_BLOCK = 128
ROPE_BASE = 10000.0
SGU_CHUNK = 128
SGU_GROUPS = 4
SGU_GROUP_W = BRANCH_W // SGU_GROUPS
N_EXPERTS = 16
EC_FACTOR = 2
RMS_EPS = 1e-6

IN_SPLITS = (BRANCH_W,
             DN_HEADS * DN_DK, DN_HEADS * DN_DK,
             DN_HEADS * DN_DV, DN_HEADS * DN_DV,
             2 * DN_HEADS, 2 * DN_HEADS,
             SWA_Q_HEADS * SWA_HD, SWA_KV_HEADS * SWA_HD, SWA_KV_HEADS * SWA_HD,
             BRANCH_W, BRANCH_W)
IN_OFFSETS = tuple(int(o) for o in np.cumsum(IN_SPLITS)[:-1])
AB_LO = IN_OFFSETS[4]
AB_W = 4 * DN_HEADS
MAIN_W = sum(IN_SPLITS) - AB_W
COL_DZ, COL_P, COL_SQ, COL_GU, COL_GV, COL_SK, COL_SV = 1536, 2048, 2560, 3072, 3584, 4096, 4224
LANES = 128
MOE_TILE = 256
MOE_ALIGN = 16
MOE_WIN = 80
MIB = 2 ** 20


def _cparams(sem, vmem_mib):
    return pltpu.CompilerParams(dimension_semantics=sem, vmem_limit_bytes=vmem_mib * MIB)


def _ada_kernel(c_ref, w_ref, b_ref, o_ref):
    c = c_ref[...]
    s = (c * jax.nn.sigmoid(c)).astype(BF16)
    o_ref[0] = jnp.dot(s, w_ref[0].astype(BF16), preferred_element_type=F32) + b_ref[0]


def _ada(cond8, w_ada, b_ada):
    L, D, N = w_ada.shape
    tn = 1024
    return pl.pallas_call(
        _ada_kernel,
        out_shape=jax.ShapeDtypeStruct((L, 8, N), F32),
        grid=(L, N // tn),
        in_specs=[pl.BlockSpec((8, D), lambda l, n: (0, 0)),
                  pl.BlockSpec((1, D, tn), lambda l, n: (l, 0, n)),
                  pl.BlockSpec((1, 1, tn), lambda l, n: (l, 0, n))],
        out_specs=pl.BlockSpec((1, 8, tn), lambda l, n: (l, 0, n)),
        compiler_params=_cparams(("parallel", "parallel"), 32),
        name="ada_mod",
    )(cond8, w_ada, b_ada.reshape(L, 1, N))


def _inproj_kernel(x_ref, g_ref, sh_ref, sc_ref, w_ref, wab_ref, proj_ref, ab_ref, h_ref, hs_ref):
    @pl.when(pl.program_id(1) == 0)
    def _():
        x = x_ref[...]
        y = x * lax.rsqrt(jnp.mean(x * x, axis=-1, keepdims=True) + RMS_EPS) * g_ref[...]
        hb = (y * (1.0 + sc_ref[...]) + sh_ref[...]).astype(BF16)
        hs_ref[...] = hb
        h_ref[...] = hb
        ab_ref[...] = jnp.dot(hb, wab_ref[...], preferred_element_type=F32)

    proj_ref[...] = jnp.dot(hs_ref[...], w_ref[...], preferred_element_type=F32).astype(proj_ref.dtype)


def _inproj(x, g, sh, sc, w_main, w_ab, tm):
    M, D = x.shape
    N = w_main.shape[1]
    tn = N // 2
    vec = pl.BlockSpec((1, D), lambda m, n: (0, 0))
    return pl.pallas_call(
        _inproj_kernel,
        out_shape=(jax.ShapeDtypeStruct((M, N), BF16),
                   jax.ShapeDtypeStruct((M, LANES), F32),
                   jax.ShapeDtypeStruct((M, D), BF16)),
        grid=(M // tm, N // tn),
        in_specs=[pl.BlockSpec((tm, D), lambda m, n: (m, 0)), vec, vec, vec,
                  pl.BlockSpec((D, tn), lambda m, n: (0, n)),
                  pl.BlockSpec((D, LANES), lambda m, n: (0, 0))],
        out_specs=(pl.BlockSpec((tm, tn), lambda m, n: (m, n)),
                   pl.BlockSpec((tm, LANES), lambda m, n: (m, 0)),
                   pl.BlockSpec((tm, D), lambda m, n: (m, 0))),
        scratch_shapes=[pltpu.VMEM((tm, D), BF16)],
        compiler_params=_cparams(("parallel", "arbitrary"), 48),
        name="inproj",
    )(x, g, sh, sc, w_main, w_ab)


def _merge_kernel(h_ref, y0_ref, y1_ref, y2_ref, y3_ref, wg_ref, wb_ref, o_ref):
    h = h_ref[...]
    acc = None
    for i, y_ref in enumerate((y0_ref, y1_ref, y2_ref, y3_ref)):
        gate = jnp.dot(h, wg_ref[i], preferred_element_type=F32)
        br = jnp.dot(y_ref[...], wb_ref[i], preferred_element_type=F32)
        t = jax.nn.sigmoid(gate) * br
        acc = t if acc is None else acc + t
    o_ref[...] = acc.astype(o_ref.dtype)


def _merge(h, ys, w_gate, w_branch, tm):
    M, D = h.shape
    tn = 512
    ybs = pl.BlockSpec((tm, BRANCH_W), lambda m, n: (m, 0))
    return pl.pallas_call(
        _merge_kernel,
        out_shape=jax.ShapeDtypeStruct((M, D), BF16),
        grid=(M // tm, D // tn),
        in_specs=[pl.BlockSpec((tm, D), lambda m, n: (m, 0)), ybs, ybs, ybs, ybs,
                  pl.BlockSpec((N_BRANCH, D, tn), lambda m, n: (0, 0, n)),
                  pl.BlockSpec((N_BRANCH, BRANCH_W, tn), lambda m, n: (0, 0, n))],
        out_specs=pl.BlockSpec((tm, tn), lambda m, n: (m, n)),
        compiler_params=_cparams(("parallel", "arbitrary"), 48),
        name="merge",
    )(h, *ys, w_gate, w_branch)


def _outproj_kernel(acc_ref, x_ref, w_ref, g1_ref, ng_ref, sh_ref, sc_ref, wr_ref, br_ref,
                    xo_ref, h2_ref, aff_ref):
    out = jnp.dot(acc_ref[...], w_ref[...], preferred_element_type=F32)
    xn = x_ref[...] + g1_ref[...] * out
    xo_ref[...] = xn
    y = xn * lax.rsqrt(jnp.mean(xn * xn, axis=-1, keepdims=True) + RMS_EPS) * ng_ref[...]
    h2 = (y * (1.0 + sc_ref[...]) + sh_ref[...]).astype(BF16)
    h2_ref[...] = h2
    logits = lax.dot_general(wr_ref[...], h2, (((1,), (1,)), ((), ())), preferred_element_type=F32)
    logits = logits + br_ref[...]
    e = jnp.exp(logits - jnp.max(logits, axis=0, keepdims=True))
    aff = e / jnp.sum(e, axis=0, keepdims=True)
    for j in range(aff_ref.shape[0]):
        aff_ref[j] = aff[:, j * MOE_TILE:(j + 1) * MOE_TILE]


def _outproj(acc, x, w_out, g1, ng, sh, sc, w_router_t, b_router, tm):
    M, D = x.shape
    E = w_router_t.shape[0]
    vec = pl.BlockSpec((1, D), lambda m: (0, 0))
    return pl.pallas_call(
        _outproj_kernel,
        out_shape=(jax.ShapeDtypeStruct((M, D), F32),
                   jax.ShapeDtypeStruct((M, D), BF16),
                   jax.ShapeDtypeStruct((M // MOE_TILE, E, MOE_TILE), F32)),
        grid=(M // tm,),
        in_specs=[pl.BlockSpec((tm, D), lambda m: (m, 0)),
                  pl.BlockSpec((tm, D), lambda m: (m, 0)),
                  pl.BlockSpec((D, D), lambda m: (0, 0)),
                  vec, vec, vec, vec,
                  pl.BlockSpec((E, D), lambda m: (0, 0)),
                  pl.BlockSpec((E, 1), lambda m: (0, 0))],
        out_specs=(pl.BlockSpec((tm, D), lambda m: (m, 0)),
                   pl.BlockSpec((tm, D), lambda m: (m, 0)),
                   pl.BlockSpec((tm // MOE_TILE, E, MOE_TILE), lambda m: (m, 0, 0))),
        compiler_params=_cparams(("parallel",), 48),
        name="outproj",
    )(acc, x, w_out, g1, ng, sh, sc, w_router_t, b_router)


def _route_kernel(aff_ref, u_ref, l_ref, rank_ref, base_ref, tot_ref, *, cap):
    aff = aff_ref[...]
    nt, E, W = aff.shape

    def count(mask):
        return jnp.sum(jnp.sum(mask.astype(jnp.int32), axis=0), axis=1, keepdims=True)

    def search(it, thr):
        cand = thr | lax.shift_left(jnp.int32(1), 30 - it)
        c = count(aff >= lax.bitcast_convert_type(cand, F32)[None])
        return jnp.where(c >= cap, cand, thr)

    thr = lax.bitcast_convert_type(lax.fori_loop(0, 31, search, jnp.zeros((E, 1), jnp.int32)), F32)[None]
    gt = aff > thr
    eq = aff == thr
    need = (cap - count(gt)).astype(F32)[None]

    def prefix(mask):
        m2 = jnp.where(mask, 1.0, 0.0).astype(BF16).reshape(nt * E, W)
        incl = jnp.dot(m2, u_ref[...], preferred_element_type=F32)
        tot = jnp.broadcast_to(incl[:, W - 1:W], (nt * E, LANES))
        base = jnp.dot(l_ref[...], tot.astype(BF16), preferred_element_type=F32)
        return (incl - m2.astype(F32) + base[:, 0:1]).reshape(nt, E, W), base, tot

    eq_rank, _, _ = prefix(eq)
    sel = gt | (eq & (eq_rank < need))
    rank, base, tot = prefix(sel)
    rank_ref[...] = jnp.where(sel, rank.astype(jnp.int32), -1)
    base_ref[...] = base.astype(jnp.int32)
    tot_ref[...] = tot.astype(jnp.int32)


def _route(aff_tm, cap):
    nt, E, W = aff_tm.shape
    upper = np.triu(np.ones((W, W), np.float32))
    idx = np.arange(nt * E)
    lower = ((idx[:, None] % E) == (idx[None, :] % E)) & ((idx[None, :] // E) < (idx[:, None] // E))
    full = lambda a: pl.BlockSpec(a.shape, lambda: (0,) * a.ndim)
    consts = (jnp.asarray(upper, BF16), jnp.asarray(lower, BF16))
    return pl.pallas_call(
        functools.partial(_route_kernel, cap=cap),
        out_shape=(jax.ShapeDtypeStruct((nt, E, W), jnp.int32), jax.ShapeDtypeStruct((nt * E, LANES), jnp.int32),
                   jax.ShapeDtypeStruct((nt * E, LANES), jnp.int32)),
        in_specs=[full(aff_tm), full(consts[0]), full(consts[1])],
        out_specs=(pl.BlockSpec((nt, E, W), lambda: (0, 0, 0)), pl.BlockSpec((nt * E, LANES), lambda: (0, 0)),
                   pl.BlockSpec((nt * E, LANES), lambda: (0, 0))),
        compiler_params=pltpu.CompilerParams(vmem_limit_bytes=40 * MIB),
        name="moe_route",
    )(aff_tm, *consts)


def _slot_onehots(rank, cur, lo_ref, cnt_ref, b, R, clamp_hi):
    E = rank.shape[0]
    riota = lax.broadcasted_iota(jnp.int32, (R, rank.shape[1]), 0)
    ohs, starts, news = [], [], []
    for e in range(E):
        c = cur[e]
        end = lo_ref[b * E + e] + cnt_ref[b * E + e]
        w = (c // MOE_ALIGN) * MOE_ALIGN
        if clamp_hi is not None:
            w = jnp.minimum(w, clamp_hi)
        rk = rank[e:e + 1, :]
        ohs.append(((rk - w) == riota) & (rk >= c))
        starts.append(w)
        news.append(jnp.minimum(end, w + R))
    return ohs, starts, news


def _gather_kernel(lo_ref, cnt_ref, h_ref, rank_ref, aff_ref, xe_ref, xg_ref,
                   stage, gstage, carry, gcarry, cur, npass, sem, *, R, cap):
    b = pl.program_id(0)
    E = rank_ref.shape[1]
    A = MOE_ALIGN

    @pl.when(b == 0)
    def _():
        carry[...] = jnp.zeros_like(carry)
        gcarry[...] = jnp.zeros_like(gcarry)
        npass[0] = 0
        stage[1, 0:R, :] = jnp.zeros((R, stage.shape[2]), stage.dtype)
        gstage[1, 0:R, :] = jnp.zeros((R, LANES), F32)
        pads = []
        for e in range(E):
            pads.append(pltpu.make_async_copy(stage.at[1, pl.ds(0, R)], xe_ref.at[e, pl.ds(cap, R)], sem.at[1]))
            pads.append(pltpu.make_async_copy(gstage.at[1, pl.ds(0, R)], xg_ref.at[e, pl.ds(cap, R)], sem.at[1]))
        for cp in pads:
            cp.start()
        for cp in pads:
            cp.wait()

    for e in range(E):
        cur[e] = lo_ref[b * E + e]
    rank = rank_ref[0]
    aff = aff_ref[0]
    h = h_ref[...]
    srow = lax.broadcasted_iota(jnp.int32, (A, 1), 0)

    def copies(slot, starts):
        cps = []
        for e in range(E):
            w = pl.multiple_of(starts[e], A)
            cps.append(pltpu.make_async_copy(stage.at[slot, pl.ds(e * R, R)], xe_ref.at[e, pl.ds(w, R)], sem.at[slot]))
            cps.append(pltpu.make_async_copy(gstage.at[slot, pl.ds(e * R, R)], xg_ref.at[e, pl.ds(w, R)], sem.at[slot]))
        return cps

    def one_pass(more):
        g = npass[0]
        slot = g % 2
        ohs, starts, news = _slot_onehots(rank, cur, lo_ref, cnt_ref, b, R, None)
        rows = jnp.dot(jnp.concatenate([jnp.where(oh, 1.0, 0.0).astype(BF16) for oh in ohs], axis=0), h,
                       preferred_element_type=F32)
        more = jnp.int32(0)
        for e in range(E):
            gate = jnp.sum(jnp.where(ohs[e], aff[e:e + 1, :], 0.0), axis=1, keepdims=True)
            gate = jnp.broadcast_to(gate, (R, LANES))
            stage[slot, e * R:e * R + A, :] = (rows[e * R:e * R + A] + carry[e]).astype(stage.dtype)
            stage[slot, e * R + A:(e + 1) * R, :] = rows[e * R + A:(e + 1) * R].astype(stage.dtype)
            gstage[slot, e * R:e * R + A, :] = gate[:A] + gcarry[e]
            gstage[slot, e * R + A:(e + 1) * R, :] = gate[A:]
            new = news[e]
            grp = jnp.minimum((new - starts[e]) // A, R // A - 1)
            keep = srow < (new - (new // A) * A)
            off = pl.multiple_of(e * R + grp * A, A)
            carry[e] = jnp.where(keep, stage[slot, pl.ds(off, A), :].astype(F32), 0.0)
            gcarry[e] = jnp.where(keep, gstage[slot, pl.ds(off, A), :], 0.0)
            cur[e] = new
            more = jnp.maximum(more, (new < lo_ref[b * E + e] + cnt_ref[b * E + e]).astype(jnp.int32))

        @pl.when(g > 0)
        def _():
            for cp in copies(1 - slot, [0] * E):
                cp.wait()

        for cp in copies(slot, starts):
            cp.start()
        npass[0] = g + 1
        return more

    lax.while_loop(lambda more: more > 0, one_pass, jnp.int32(1))

    @pl.when(b == pl.num_programs(0) - 1)
    def _():
        for cp in copies((npass[0] - 1) % 2, [0] * E):
            cp.wait()


def _gather(h2, rank, aff_tm, lo, cnt, cap, R):
    n, D = h2.shape
    nt, E, W = rank.shape
    grid_spec = pltpu.PrefetchScalarGridSpec(
        num_scalar_prefetch=2,
        grid=(nt,),
        in_specs=[pl.BlockSpec((W, D), lambda b, lo, cnt: (b, 0)),
                  pl.BlockSpec((1, E, W), lambda b, lo, cnt: (b, 0, 0)),
                  pl.BlockSpec((1, E, W), lambda b, lo, cnt: (b, 0, 0))],
        out_specs=(pl.BlockSpec(memory_space=pl.ANY), pl.BlockSpec(memory_space=pl.ANY)),
        scratch_shapes=[pltpu.VMEM((2, E * R, D), BF16), pltpu.VMEM((2, E * R, LANES), F32),
                        pltpu.VMEM((E, MOE_ALIGN, D), F32), pltpu.VMEM((E, MOE_ALIGN, LANES), F32),
                        pltpu.SMEM((E,), jnp.int32), pltpu.SMEM((1,), jnp.int32),
                        pltpu.SemaphoreType.DMA((2,))])
    return pl.pallas_call(
        functools.partial(_gather_kernel, R=R, cap=cap),
        out_shape=(jax.ShapeDtypeStruct((E, cap + R, D), BF16), jax.ShapeDtypeStruct((E, cap + R, LANES), F32)),
        grid_spec=grid_spec,
        compiler_params=_cparams(("arbitrary",), 48),
        name="moe_gather",
    )(lo, cnt, h2, rank, aff_tm)


FFN_STEPS = 4


def _ffn_kernel(x_ref, g_ref, wg_ref, wu_ref, wd_ref, o_ref, wgs, wus, wds):
    e = pl.program_id(0)
    s = pl.program_id(1)
    n_exp = pl.num_programs(0) - 1
    fq = wg_ref.shape[2]

    @pl.when(e < n_exp)
    def _():
        slot = e % 2
        for j in range(FFN_STEPS):

            @pl.when(s == j)
            def _():
                wgs[slot, :, j * fq:(j + 1) * fq] = wg_ref[0].astype(BF16)
                wus[slot, :, j * fq:(j + 1) * fq] = wu_ref[0].astype(BF16)
                wds[slot, j * fq:(j + 1) * fq, :] = wd_ref[0].astype(BF16)

    @pl.when(e >= 1)
    def _():
        slot = (e - 1) % 2
        x = x_ref[0]
        a = jnp.dot(x, wgs[slot], preferred_element_type=F32)
        b = jnp.dot(x, wus[slot], preferred_element_type=F32)
        hid = (a * jax.nn.sigmoid(a) * b).astype(BF16)
        o_ref[0] = (jnp.dot(hid, wds[slot], preferred_element_type=F32) * g_ref[0][:, 0:1]).astype(o_ref.dtype)


def _ffn(xe, xg, wg, wu, wd, cap):
    E, _, D = xe.shape
    F = wg.shape[2]
    S = FFN_STEPS
    M = S if cap % (S * LANES) == 0 else 1
    tm = cap // M
    fq = F // S
    rows = lambda e, s: (jnp.maximum(e - 1, 0), jnp.where(e == 0, 0, s % M), 0)
    return pl.pallas_call(
        _ffn_kernel,
        out_shape=jax.ShapeDtypeStruct((E, cap, D), BF16),
        grid=(E + 1, S),
        in_specs=[pl.BlockSpec((1, tm, D), rows),
                  pl.BlockSpec((1, tm, LANES), rows),
                  pl.BlockSpec((1, D, fq), lambda e, s: (jnp.minimum(e, E - 1), 0, s)),
                  pl.BlockSpec((1, D, fq), lambda e, s: (jnp.minimum(e, E - 1), 0, s)),
                  pl.BlockSpec((1, fq, D), lambda e, s: (jnp.minimum(e, E - 1), s, 0))],
        out_specs=pl.BlockSpec((1, tm, D), rows),
        scratch_shapes=[pltpu.VMEM((2, D, F), BF16), pltpu.VMEM((2, D, F), BF16), pltpu.VMEM((2, F, D), BF16)],
        compiler_params=_cparams(("arbitrary", "arbitrary"), 52),
        name="expert_ffn",
    )(xe, xg, wg, wu, wd)


def _combine_kernel(lo_ref, cnt_ref, rank_ref, x_ref, g2_ref, ng_ref, y_ref, o_ref, ystage, yovf, acc, cur, sem, osem,
                    *, R, cap, final_norm):
    b = pl.program_id(0)
    nb = pl.num_programs(0)
    E = rank_ref.shape[1]
    A = MOE_ALIGN

    def window(c):
        return pl.multiple_of(jnp.minimum((c // A) * A, cap - R), A)

    def main_copies(tile, slot):
        return [pltpu.make_async_copy(y_ref.at[e, pl.ds(window(lo_ref[tile * E + e]), R)],
                                      ystage.at[slot, pl.ds(e * R, R)], sem.at[slot]) for e in range(E)]

    @pl.when(b == 0)
    def _():
        for cp in main_copies(0, 0):
            cp.start()

    @pl.when(b + 1 < nb)
    def _():
        for cp in main_copies(b + 1, (b + 1) % 2):
            cp.start()

    for e in range(E):
        cur[e] = lo_ref[b * E + e]
    rank = rank_ref[0]

    def expand(y_rows):
        ohs, _, news = _slot_onehots(rank, cur, lo_ref, cnt_ref, b, R, cap - R)
        more = jnp.int32(0)
        for e in range(E):
            cur[e] = news[e]
            more = jnp.maximum(more, (news[e] < lo_ref[b * E + e] + cnt_ref[b * E + e]).astype(jnp.int32))
        oh = jnp.concatenate([jnp.where(o, 1.0, 0.0).astype(BF16) for o in ohs], axis=0)
        return lax.dot_general(oh, y_rows, (((0,), (0,)), ((), ())), preferred_element_type=F32), more

    for cp in main_copies(b, b % 2):
        cp.wait()
    first, more = expand(ystage[b % 2])
    acc[...] = first

    def extra_pass(more):
        cps = [pltpu.make_async_copy(y_ref.at[e, pl.ds(window(cur[e]), R)], yovf.at[pl.ds(e * R, R)], osem.at[0])
               for e in range(E)]
        for cp in cps:
            cp.start()
        for cp in cps:
            cp.wait()
        part, more = expand(yovf[...])
        acc[...] += part
        return more

    lax.while_loop(lambda more: more > 0, extra_pass, more)
    out = x_ref[...] + g2_ref[...] * acc[...]
    if final_norm:
        out = out * lax.rsqrt(jnp.mean(out * out, axis=-1, keepdims=True) + RMS_EPS) * ng_ref[...]
    o_ref[...] = out


def _combine(y, rank, lo, cnt, x, g2, norm_g, final_norm, cap, R):
    n, D = x.shape
    nt, E, W = rank.shape
    grid_spec = pltpu.PrefetchScalarGridSpec(
        num_scalar_prefetch=2,
        grid=(nt,),
        in_specs=[pl.BlockSpec((1, E, W), lambda b, lo, cnt: (b, 0, 0)),
                  pl.BlockSpec((W, D), lambda b, lo, cnt: (b, 0)),
                  pl.BlockSpec((1, D), lambda b, lo, cnt: (0, 0)),
                  pl.BlockSpec((1, D), lambda b, lo, cnt: (0, 0)),
                  pl.BlockSpec(memory_space=pl.ANY)],
        out_specs=pl.BlockSpec((W, D), lambda b, lo, cnt: (b, 0)),
        scratch_shapes=[pltpu.VMEM((2, E * R, D), BF16), pltpu.VMEM((E * R, D), BF16), pltpu.VMEM((W, D), F32),
                        pltpu.SMEM((E,), jnp.int32), pltpu.SemaphoreType.DMA((2,)), pltpu.SemaphoreType.DMA((1,))])
    return pl.pallas_call(
        functools.partial(_combine_kernel, R=R, cap=cap, final_norm=final_norm),
        out_shape=jax.ShapeDtypeStruct((n, D), F32),
        grid_spec=grid_spec,
        compiler_params=_cparams(("arbitrary",), 48),
        name="moe_combine",
    )(lo, cnt, rank, x, g2, norm_g, y)


def _moe(x, h2, aff_tm, g2, wg, wu, wd, norm_g, final_norm):
    n = h2.shape[0]
    E = aff_tm.shape[1]
    cap = EC_FACTOR * n // E
    R = min(MOE_WIN, cap)
    rank, base, tot = _route(aff_tm, cap)
    lo, cnt = base[:, 0], tot[:, 0]
    xe, xg = _gather(h2, rank, aff_tm, lo, cnt, cap, R)
    y = _ffn(xe, xg, wg, wu, wd, cap)
    return _combine(y, rank, lo, cnt, x, g2, norm_g, final_norm, cap, R)


POOL_HALO = 16


def _pool_kernel(cur_ref, prev_ref, next_ref, w_ref, sc_ref, y_ref, scr, *, seq_len):
    m = pl.program_id(0)
    tm = cur_ref.shape[0]
    H = POOL_HALO
    scr[0:H, :] = jnp.where(m > 0, prev_ref[...].astype(F32), 0.0)
    scr[H:H + tm, :] = cur_ref[...].astype(F32)
    scr[H + tm:2 * H + tm, :] = jnp.where(m < pl.num_programs(0) - 1, next_ref[...].astype(F32), 0.0)
    t = m * tm + lax.broadcasted_iota(jnp.int32, (tm, 1), 0)
    for gi, w in enumerate(POOL_WINDOWS):
        cols = slice(gi * POOL_GROUP_W, (gi + 1) * POOL_GROUP_W)
        total = None
        for k in range(-(w // 2), w // 2):
            term = scr[H + k:H + k + tm, cols]
            total = term if total is None else total + term
        n = jnp.minimum(t + (w // 2 - 1), seq_len - 1) - jnp.maximum(t - w // 2, 0) + 1
        centred = total / n.astype(F32) - scr[H:H + tm, cols]
        y = jnp.dot(centred.astype(BF16), w_ref[gi].astype(BF16), preferred_element_type=F32)
        y_ref[:, cols] = (y * sc_ref[:, cols]).astype(y_ref.dtype)


def _pool(proj, pool_w, pool_scale, tm):
    M = proj.shape[0]
    hb = tm // POOL_HALO
    nhb = M // POOL_HALO
    cb = COL_P // BRANCH_W
    return pl.pallas_call(
        functools.partial(_pool_kernel, seq_len=M),
        out_shape=jax.ShapeDtypeStruct((M, BRANCH_W), BF16),
        grid=(M // tm,),
        in_specs=[pl.BlockSpec((tm, BRANCH_W), lambda m: (m, cb)),
                  pl.BlockSpec((POOL_HALO, BRANCH_W), lambda m: (jnp.maximum(m * hb - 1, 0), cb)),
                  pl.BlockSpec((POOL_HALO, BRANCH_W), lambda m: (jnp.minimum((m + 1) * hb, nhb - 1), cb)),
                  pl.BlockSpec(pool_w.shape, lambda m: (0, 0, 0)),
                  pl.BlockSpec((1, BRANCH_W), lambda m: (0, 0))],
        out_specs=pl.BlockSpec((tm, BRANCH_W), lambda m: (m, 0)),
        scratch_shapes=[pltpu.VMEM((tm + 2 * POOL_HALO, BRANCH_W), F32)],
        compiler_params=_cparams(("parallel",), 32),
        name="pool_mix",
    )(proj, proj, proj, pool_w, pool_scale)


def _gelu(x):
    return 0.5 * x * (1.0 + lax.erf(x * (2.0 ** -0.5)))


def _sgu_kernel(u_ref, v_ref, g_ref, w_ref, b_ref, y_ref):
    tm = u_ref.shape[0]
    v = _gelu(v_ref[...].astype(F32))
    v = (v * lax.rsqrt(jnp.mean(v * v, axis=-1, keepdims=True) + RMS_EPS) * g_ref[...]).astype(BF16)
    u = _gelu(u_ref[...].astype(F32))
    for c in range(tm // SGU_CHUNK):
        rows = slice(c * SGU_CHUNK, (c + 1) * SGU_CHUNK)
        for g in range(SGU_GROUPS):
            cols = slice(g * SGU_GROUP_W, (g + 1) * SGU_GROUP_W)
            mixed = jnp.dot(w_ref[g].astype(BF16), v[rows, cols], preferred_element_type=F32) + b_ref[:, g:g + 1]
            y_ref[rows, cols] = (u[rows, cols] * mixed).astype(y_ref.dtype)


def _sgu(proj, norm_g, w_s, b_s, tm):
    M = proj.shape[0]
    bt = jnp.pad(b_s.T, ((0, 0), (0, LANES - SGU_GROUPS)))
    return pl.pallas_call(
        _sgu_kernel,
        out_shape=jax.ShapeDtypeStruct((M, BRANCH_W), BF16),
        grid=(M // tm,),
        in_specs=[pl.BlockSpec((tm, BRANCH_W), lambda m: (m, COL_GU // BRANCH_W)),
                  pl.BlockSpec((tm, BRANCH_W), lambda m: (m, COL_GV // BRANCH_W)),
                  pl.BlockSpec((1, BRANCH_W), lambda m: (0, 0)),
                  pl.BlockSpec(w_s.shape, lambda m: (0, 0, 0)),
                  pl.BlockSpec((SGU_CHUNK, LANES), lambda m: (0, 0))],
        out_specs=pl.BlockSpec((tm, BRANCH_W), lambda m: (m, 0)),
        compiler_params=_cparams(("parallel",), 32),
        name="sgu_mix",
    )(proj, proj, norm_g, w_s, bt)


DN_CONV = 5
DN_HALO = 16
DN_QKV_W = 3 * BRANCH_W
DN_STATES = 2 * DN_HEADS


def _dn_prep_kernel(cur_ref, prev_ref, next_ref, ab_ref, cw_ref, alog_ref, dtb_ref,
                    q_ref, k_ref, v_ref, aux_ref, scr):
    m = pl.program_id(0)
    tm = cur_ref.shape[0]
    scr[0:DN_HALO, :] = jnp.where(m > 0, prev_ref[...].astype(F32), 0.0)
    scr[DN_HALO:DN_HALO + tm, :] = cur_ref[...].astype(F32)
    scr[DN_HALO + tm:2 * DN_HALO + tm, :] = jnp.where(m < pl.num_programs(0) - 1, next_ref[...].astype(F32), 0.0)
    acc = None
    for t in range(DN_CONV):
        lo = DN_HALO - DN_CONV // 2 + t
        term = scr[lo:lo + tm, :] * cw_ref[t:t + 1, :]
        acc = term if acc is None else acc + term
    y = acc * jax.nn.sigmoid(acc)
    for h in range(DN_HEADS):
        qh = y[:, h * DN_DK:(h + 1) * DN_DK]
        q_ref[:, h * DN_DK:(h + 1) * DN_DK] = (
            qh * lax.rsqrt(jnp.sum(qh * qh, axis=-1, keepdims=True) + 1e-6) * DN_DK ** -0.5)
        kh = y[:, BRANCH_W + h * DN_DK:BRANCH_W + (h + 1) * DN_DK]
        k_ref[:, h * DN_DK:(h + 1) * DN_DK] = kh * lax.rsqrt(jnp.sum(kh * kh, axis=-1, keepdims=True) + 1e-6)
    v_ref[...] = y[:, 2 * BRANCH_W:]

    ab = ab_ref[...]
    lane = lax.broadcasted_iota(jnp.int32, ab.shape, 1)
    sp = ab + dtb_ref[...]
    softplus = jnp.maximum(sp, 0.0) + jnp.log(1.0 + jnp.exp(-jnp.abs(sp)))
    g = jnp.where(lane < DN_STATES, -jnp.exp(alog_ref[...]) * softplus, 0.0)
    beta = jax.nn.sigmoid(ab)
    g1 = g.astype(BF16)
    r1 = g - g1.astype(F32)
    g2 = r1.astype(BF16)
    g3 = (r1 - g2.astype(F32)).astype(BF16)
    gs = jnp.concatenate([g1, g2, g3], axis=1)
    row = lax.broadcasted_iota(jnp.int32, (tm, tm), 0)
    col = lax.broadcasted_iota(jnp.int32, (tm, tm), 1)
    same = (row // DN_CHUNK) == (col // DN_CHUNK)
    tri_f = jnp.where(same & (col <= row), 1.0, 0.0).astype(BF16)
    tri_b = jnp.where(same & (col >= row), 1.0, 0.0).astype(BF16)
    cf = jnp.dot(tri_f, gs, preferred_element_type=F32)
    cb = jnp.dot(tri_b, gs, preferred_element_type=F32)
    cf = cf[:, :LANES] + cf[:, LANES:2 * LANES] + cf[:, 2 * LANES:]
    cb = cb[:, :LANES] + cb[:, LANES:2 * LANES] + cb[:, 2 * LANES:]
    gc = jnp.where(lane < DN_HEADS, cf, cb)
    aux_ref[...] = jnp.where(lane < DN_STATES, gc, beta)


def _dn_prep(proj, ab, cw, alog, dtb, tm):
    M = proj.shape[0]
    hb = tm // DN_HALO
    nhb = M // DN_HALO
    row = pl.BlockSpec((1, LANES), lambda m: (0, 0))
    return pl.pallas_call(
        _dn_prep_kernel,
        out_shape=(jax.ShapeDtypeStruct((M, BRANCH_W), F32),) * 3 + (jax.ShapeDtypeStruct((M, LANES), F32),),
        grid=(M // tm,),
        in_specs=[pl.BlockSpec((tm, DN_QKV_W), lambda m: (m, 0)),
                  pl.BlockSpec((DN_HALO, DN_QKV_W), lambda m: (jnp.maximum(m * hb - 1, 0), 0)),
                  pl.BlockSpec((DN_HALO, DN_QKV_W), lambda m: (jnp.minimum((m + 1) * hb, nhb - 1), 0)),
                  pl.BlockSpec((tm, LANES), lambda m: (m, 0)),
                  pl.BlockSpec((8, DN_QKV_W), lambda m: (0, 0)), row, row],
        out_specs=(pl.BlockSpec((tm, BRANCH_W), lambda m: (m, 0)),) * 3 + (pl.BlockSpec((tm, LANES), lambda m: (m, 0)),),
        scratch_shapes=[pltpu.VMEM((tm + 2 * DN_HALO, DN_QKV_W), F32)],
        compiler_params=_cparams(("parallel",), 40),
        name="dn_prep",
    )(proj, proj, proj, ab, cw, alog, dtb)


def _dn_constants():
    L, H = DN_CHUNK, DN_HEADS
    WW, FW = DN_STATES * L, DN_STATES * DN_DK
    r = np.arange(L)[:, None]
    lane = np.arange(WW)[None, :]
    c = lane % L
    lower = lane < H * L
    cm = np.zeros((9, L, WW), np.float32)
    cm[0] = r == c
    cm[1] = np.where(np.where(lower, r >= c, r <= c), 0.0, -np.inf)
    cm[2] = np.where(lower, r > c, r < c)
    hi, lo = np.where(lower, r, c), np.where(lower, c, r)
    for j in range(6):
        s = 2 ** j
        cm[3 + j] = ((r // (2 * s)) == (c // (2 * s))) & (hi % (2 * s) >= s) & (lo % (2 * s) < s)
    rw = np.arange(WW)[:, None]
    rf = np.arange(H * DN_DK)[:, None]
    m_ww = rw // L == np.arange(WW)[None] // L
    m_wf = rw // L == np.arange(FW)[None] // DN_DK
    m_w2f = rw // L == (np.arange(2 * FW)[None] % FW) // DN_DK
    m_ff = rf // DN_DK == np.arange(H * DN_DK)[None] // DN_DK
    return (jnp.asarray(cm),) + tuple(jnp.asarray(m, BF16) for m in (m_ww, m_wf, m_w2f, m_ff))


def _dn_scan_kernel(qf_ref, kf_ref, vf_ref, af_ref, qb_ref, kb_ref, vb_ref, ab_ref, s0_ref,
                    cm_ref, mww_ref, mwf_ref, mw2f_ref, mff_ref, of_ref, ob_ref, sfin_ref, s_scr):
    i = pl.program_id(0)
    L, H = DN_CHUNK, DN_HEADS
    FD = H * DN_DK

    @pl.when(i == 0)
    def _():
        s_scr[...] = s0_ref[...]

    lane = lax.broadcasted_iota(jnp.int32, (L, LANES), 1)
    rowblk = lax.broadcasted_iota(jnp.int32, (FD, DN_DV), 0) // DN_DK
    m_ww, m_wf, m_w2f, m_ff = mww_ref[...], mwf_ref[...], mw2f_ref[...], mff_ref[...]

    def tile_rows(a):
        return jnp.concatenate([a] * DN_STATES, axis=0)

    def col(arr, c):
        return jnp.broadcast_to(arr[:, c:c + 1], (arr.shape[0], LANES))

    def feat(arrs, base):
        return jnp.concatenate([col(arrs[d], base + d * H + h) for d in range(2) for h in range(H)], axis=1)

    def wide(arrs, base):
        return jnp.concatenate([jnp.where(lane < L, col(arrs[d], base + d * H + 2 * p),
                                          col(arrs[d], base + d * H + 2 * p + 1))
                                for d in range(2) for p in range(H // 2)], axis=1)

    aux = (af_ref[...], ab_ref[...])
    last = (aux[0][L - 1:L, :], aux[1][0:1, :])
    e_gc = tuple(jnp.exp(a) for a in aux)
    e_rem = tuple(jnp.exp(l - a) for l, a in zip(last, aux))
    e_last = tuple(jnp.exp(l) for l in last)
    beta_f = feat(aux, DN_STATES)
    eg_f = feat(e_gc, 0)
    er_f = feat(e_rem, 0)
    gcw = wide(aux, 0)
    eye = cm_ref[0]
    gc_row = jnp.sum(eye * gcw, axis=0, keepdims=True)
    decay = jnp.exp(gcw - gc_row + cm_ref[1])
    q = jnp.concatenate([qf_ref[...], qb_ref[...]], axis=1)
    k = jnp.concatenate([kf_ref[...], kb_ref[...]], axis=1)
    v = jnp.concatenate([vf_ref[...], vb_ref[...]], axis=1)
    kbeta = k * beta_f
    kq = lax.dot_general(jnp.concatenate([kbeta, q], axis=0).astype(BF16), tile_rows(k.astype(BF16)) * m_wf,
                         (((1,), (1,)), ((), ())), preferred_element_type=F32)
    mm = kq[:L] * decay * cm_ref[2]
    qk = (kq[L:] * decay).astype(BF16)
    x = eye - mm * cm_ref[3]
    for j in range(4, 9):
        xb = x.astype(BF16)
        t = jnp.dot(xb, tile_rows((mm * cm_ref[j]).astype(BF16)) * m_ww, preferred_element_type=F32)
        x = x - jnp.dot(t.astype(BF16), tile_rows(xb) * m_ww, preferred_element_type=F32)
    rhs = jnp.concatenate([v * beta_f, kbeta * eg_f], axis=1)
    sol = rhs + jnp.dot((x - eye).astype(BF16), tile_rows(rhs.astype(BF16)) * m_w2f, preferred_element_type=F32)
    u = sol[:, :2 * FD]
    w = sol[:, 2 * FD:]
    qd = q * eg_f
    kd = (k * er_f).astype(BF16)
    s_old, ws_qs = [], []
    for d in range(2):
        cols = slice(d * FD, (d + 1) * FD)
        s_old.append(s_scr[d])
        s_bd = jnp.concatenate([s_old[d].astype(BF16)] * H, axis=1) * m_ff
        ws_qs.append(jnp.dot(jnp.concatenate([w[:, cols], qd[:, cols]], axis=0).astype(BF16), s_bd,
                             preferred_element_type=F32))
    v_new = (u - jnp.concatenate([ws_qs[0][:L], ws_qs[1][:L]], axis=1)).astype(BF16)
    o = jnp.concatenate([ws_qs[0][L:], ws_qs[1][L:]], axis=1) + jnp.dot(qk, tile_rows(v_new) * m_wf,
                                                                       preferred_element_type=F32)
    for d, o_ref in enumerate((of_ref, ob_ref)):
        cols = slice(d * FD, (d + 1) * FD)
        cross = lax.dot_general(kd[:, cols], v_new[:, cols], (((0,), (0,)), ((), ())), preferred_element_type=F32)
        fold = None
        g_last = None
        for h in range(H):
            part = jnp.where(rowblk == h, cross[:, h * DN_DV:(h + 1) * DN_DV], 0.0)
            gl = jnp.where(rowblk == h, e_last[d][:, d * H + h:d * H + h + 1], 0.0)
            fold = part if fold is None else fold + part
            g_last = gl if g_last is None else g_last + gl
        s_scr[d] = g_last * s_old[d] + fold
        o_ref[...] = o[:, cols]

    @pl.when(i == pl.num_programs(0) - 1)
    def _():
        sfin_ref[...] = s_scr[...]


def _dn_scan(q, k, v, aux, s0, consts):
    M = q.shape[0]
    N = M // DN_CHUNK
    fwd = lambda i: (i, 0)
    bwd = lambda i: (N - 1 - i, 0)
    cs = lambda w, im: pl.BlockSpec((DN_CHUNK, w), im)
    st = pl.BlockSpec(s0.shape, lambda i: (0, 0, 0))
    const_specs = [pl.BlockSpec(c.shape, lambda i, _n=c.ndim: (0,) * _n) for c in consts]
    return pl.pallas_call(
        _dn_scan_kernel,
        out_shape=(jax.ShapeDtypeStruct((M, BRANCH_W), F32), jax.ShapeDtypeStruct((M, BRANCH_W), F32),
                   jax.ShapeDtypeStruct(s0.shape, F32)),
        grid=(N,),
        in_specs=[cs(BRANCH_W, fwd), cs(BRANCH_W, fwd), cs(BRANCH_W, fwd), cs(LANES, fwd),
                  cs(BRANCH_W, bwd), cs(BRANCH_W, bwd), cs(BRANCH_W, bwd), cs(LANES, bwd), st] + const_specs,
        out_specs=(cs(BRANCH_W, fwd), cs(BRANCH_W, bwd), st),
        scratch_shapes=[pltpu.VMEM(s0.shape, F32)],
        compiler_params=_cparams(("arbitrary",), 32),
        name="dn_scan",
    )(q, k, v, aux, q, k, v, aux, s0, *consts)


def _dn_out_kernel(of_ref, ob_ref, z_ref, g_ref, y_ref):
    o = of_ref[...] + ob_ref[...]
    z = z_ref[...].astype(F32)
    for h in range(DN_HEADS):
        sl = slice(h * DN_DV, (h + 1) * DN_DV)
        oh = o[:, sl]
        zh = z[:, sl]
        yh = oh * lax.rsqrt(jnp.mean(oh * oh, axis=-1, keepdims=True) + RMS_EPS) * g_ref[...]
        y_ref[:, sl] = (yh * (zh * jax.nn.sigmoid(zh))).astype(y_ref.dtype)


def _dn_out(o_f, o_b, proj, norm_g, tm):
    M = o_f.shape[0]
    blk = pl.BlockSpec((tm, BRANCH_W), lambda m: (m, 0))
    return pl.pallas_call(
        _dn_out_kernel,
        out_shape=jax.ShapeDtypeStruct((M, BRANCH_W), BF16),
        grid=(M // tm,),
        in_specs=[blk, blk, pl.BlockSpec((tm, BRANCH_W), lambda m: (m, COL_DZ // BRANCH_W)),
                  pl.BlockSpec((1, DN_DV), lambda m: (0, 0))],
        out_specs=blk,
        compiler_params=_cparams(("parallel",), 32),
        name="dn_out",
    )(o_f, o_b, proj, norm_g)


def _deltanet(proj, ab, projc, abc, conv_w, a_log, dt_bias, norm_g, want_ctx, tm):
    cw = jnp.pad(conv_w, ((0, 8 - DN_CONV), (0, 0)))
    alog = jnp.pad(a_log.reshape(1, DN_STATES), ((0, 0), (0, LANES - DN_STATES)))
    dtb = jnp.pad(dt_bias.reshape(1, DN_STATES), ((0, 0), (0, LANES - DN_STATES)))
    ql, kl, vl, auxl = _dn_prep(proj, ab, cw, alog, dtb, tm)
    qx, kx, vx, auxx = _dn_prep(projc, abc, cw, alog, dtb, projc.shape[0])
    consts = _dn_constants()
    s0 = jnp.zeros((2, DN_HEADS * DN_DK, DN_DV), F32)
    oc_f, oc_b, s1 = _dn_scan(qx, kx, vx, auxx, s0, consts)
    o_f, o_b, _ = _dn_scan(ql, kl, vl, auxl, s1, consts)
    y = _dn_out(o_f, o_b, proj, norm_g[None], tm)
    yc = _dn_out(oc_f, oc_b, projc, norm_g[None], projc.shape[0]) if want_ctx else None
    return y, yc


SWA_PREP_W = 8 * LANES


def _rope_tables(T):
    half = SWA_HD // 4
    inv = ROPE_BASE ** (-jnp.arange(half, dtype=F32) / half)
    t = jnp.arange(T, dtype=jnp.int32)
    ang_r = (t // GRID_W).astype(F32)[:, None] * inv
    ang_c = (t % GRID_W).astype(F32)[:, None] * inv
    cos = jnp.concatenate([jnp.cos(ang_r)] * 2 + [jnp.cos(ang_c)] * 2, axis=1)
    sin = jnp.concatenate([-jnp.sin(ang_r), jnp.sin(ang_r), -jnp.sin(ang_c), jnp.sin(ang_c)], axis=1)
    return jnp.tile(cos, (1, 2)), jnp.tile(sin, (1, 2))


def _swa_prep_kernel(q_ref, kv_ref, cos_ref, sin_ref, qo_ref, kvo_ref, *, rope):
    lane = lax.broadcasted_iota(jnp.int32, (q_ref.shape[0], LANES), 1)
    first = (lane % (SWA_HD // 2)) < (SWA_HD // 4)
    low = lane < SWA_HD

    def rot(x):
        if not rope:
            return x
        partner = jnp.where(first, pltpu.roll(x, LANES - SWA_HD // 4, 1), pltpu.roll(x, SWA_HD // 4, 1))
        return x * cos_ref[...] + partner * sin_ref[...]

    for c in range(q_ref.shape[1] // LANES):
        cols = slice(c * LANES, (c + 1) * LANES)
        qo_ref[:, cols] = (rot(q_ref[:, cols].astype(F32)) * SWA_HD ** -0.5).astype(qo_ref.dtype)
    kv = kv_ref[...].astype(F32)
    for j, x in enumerate((rot(kv[:, :LANES]), kv[:, LANES:])):
        sw = pltpu.roll(x, SWA_HD, 1)
        parts = (jnp.where(low, x, 0.0), jnp.where(low, 0.0, sw), jnp.where(low, sw, 0.0), jnp.where(low, 0.0, x))
        for i, part in enumerate(parts):
            kvo_ref[:, (4 * j + i) * LANES:(4 * j + i + 1) * LANES] = part.astype(kvo_ref.dtype)


def _swa_prep(proj, cos, sin, rope, tm):
    M = proj.shape[0]
    tab = pl.BlockSpec((tm, LANES), lambda m: (m, 0))
    return pl.pallas_call(
        functools.partial(_swa_prep_kernel, rope=rope),
        out_shape=(jax.ShapeDtypeStruct((M, BRANCH_W), BF16), jax.ShapeDtypeStruct((M, SWA_PREP_W), BF16)),
        grid=(M // tm,),
        in_specs=[pl.BlockSpec((tm, BRANCH_W), lambda m: (m, COL_SQ // BRANCH_W)),
                  pl.BlockSpec((tm, 2 * LANES), lambda m: (m, COL_SK // (2 * LANES))), tab, tab],
        out_specs=(pl.BlockSpec((tm, BRANCH_W), lambda m: (m, 0)), pl.BlockSpec((tm, SWA_PREP_W), lambda m: (m, 0))),
        compiler_params=_cparams(("parallel",), 32),
        name="swa_prep",
    )(proj, proj, cos, sin)


def _swa_masks(n_ctx):
    nk = 3 * SWA_BLOCK + n_ctx
    i = np.arange(SWA_BLOCK)[:, None]
    j = np.arange(nk)[None, :]
    lat = j < 3 * SWA_BLOCK
    band = ~lat | (np.abs(j - SWA_BLOCK - i) <= SWA_WINDOW)
    no_prev = ~(j < SWA_BLOCK) | (i < 0)
    no_next = ~((j >= 2 * SWA_BLOCK) & lat) | (i < 0)
    ctx_only = ~lat | (i < 0)
    m = np.stack([np.where(ok, 0.0, -np.inf) for ok in (band, no_prev, no_next, ctx_only)]).astype(np.float32)
    return jnp.asarray(np.concatenate([m, m], axis=2))


def _swa_attn_kernel(sink_ref, q_ref, kvp_ref, kvo_ref, kvn_ref, kvc_ref, m_ref, y_ref, *, latent_keys):
    n = pl.program_id(0)
    kv = jnp.concatenate([kvp_ref[...], kvo_ref[...], kvn_ref[...], kvc_ref[...]], axis=0)
    nk = kv.shape[0]
    if latent_keys:
        mask = (m_ref[0] + jnp.where(n == 0, m_ref[1], 0.0) + jnp.where(n == pl.num_programs(0) - 1, m_ref[2], 0.0))
    else:
        mask = m_ref[3]
    lane = lax.broadcasted_iota(jnp.int32, (q_ref.shape[0], LANES), 1)
    for c in range(SWA_Q_HEADS // 2):
        kh = (2 * c) // SWA_GROUP
        cols = slice(c * LANES, (c + 1) * LANES)
        kk = jnp.concatenate([kv[:, (2 * kh) * LANES:(2 * kh + 1) * LANES],
                              kv[:, (2 * kh + 1) * LANES:(2 * kh + 2) * LANES]], axis=0)
        vv = jnp.concatenate([kv[:, (4 + 2 * kh) * LANES:(5 + 2 * kh) * LANES],
                              kv[:, (5 + 2 * kh) * LANES:(6 + 2 * kh) * LANES]], axis=0)
        s = lax.dot_general(q_ref[:, cols], kk, (((1,), (1,)), ((), ())), preferred_element_type=F32) + mask
        ps, dens = [], []
        for half in range(2):
            sh = s[:, half * nk:(half + 1) * nk]
            sink = sink_ref[2 * c + half]
            mx = jnp.maximum(jnp.max(sh, axis=1, keepdims=True), sink)
            p = jnp.exp(sh - mx)
            dens.append(jnp.sum(p, axis=1, keepdims=True) + jnp.exp(sink - mx))
            ps.append(p.astype(BF16))
        o = jnp.dot(jnp.concatenate(ps, axis=1), vv, preferred_element_type=F32)
        y_ref[:, cols] = (o / jnp.where(lane < SWA_HD, dens[0], dens[1])).astype(y_ref.dtype)


def _swa_attn(q, kvr, kvc, sink, masks, latent_keys):
    M = q.shape[0]
    NB = M // SWA_BLOCK
    blk = lambda im: pl.BlockSpec((SWA_BLOCK, SWA_PREP_W), im)
    return pl.pallas_call(
        functools.partial(_swa_attn_kernel, latent_keys=latent_keys),
        out_shape=jax.ShapeDtypeStruct((M, BRANCH_W), BF16),
        grid=(NB,),
        in_specs=[pl.BlockSpec(memory_space=pltpu.SMEM),
                  pl.BlockSpec((SWA_BLOCK, BRANCH_W), lambda n: (n, 0)),
                  blk(lambda n: (jnp.maximum(n - 1, 0), 0)), blk(lambda n: (n, 0)),
                  blk(lambda n: (jnp.minimum(n + 1, NB - 1), 0)),
                  pl.BlockSpec(kvc.shape, lambda n: (0, 0)),
                  pl.BlockSpec(masks.shape, lambda n: (0, 0, 0))],
        out_specs=pl.BlockSpec((SWA_BLOCK, BRANCH_W), lambda n: (n, 0)),
        compiler_params=_cparams(("parallel",), 32),
        name="swa_attn",
    )(sink, q, kvr, kvr, kvr, kvc, masks)


def _swa(proj, projc, sink, want_ctx, tm):
    T, Lc = proj.shape[0], projc.shape[0]
    cos, sin = _rope_tables(T)
    q, kvr = _swa_prep(proj, cos, sin, True, tm)
    none = jnp.zeros((Lc, LANES), F32)
    qc, kvc = _swa_prep(projc, none, none, False, Lc)
    masks = _swa_masks(Lc)
    y = _swa_attn(q, kvr, kvc, sink, masks, True)
    yc = _swa_attn(qc, kvc, kvc, sink, masks, False) if want_ctx else None
    return y, yc


def kernel(x, c, ctx, c_ctx, w_ada, b_ada, norm1_g, norm2_g, w_in, pool_w, pool_scale, dn_conv_w, dn_a_log, dn_dt_bias, dn_norm_g, swa_sink, sgu_norm_g, sgu_w, sgu_b, w_gate, w_branch, w_out, w_router, b_router, w_e_gate, w_e_up, w_e_down, final_g):
    B, T, D = x.shape
    assert B == 1 and D == D_MODEL
    Lc = ctx.shape[1]
    depth = w_ada.shape[0]

    cond8 = jnp.zeros((8, D), F32).at[0].set(c[0]).at[1].set(c_ctx)
    mods = _ada(cond8, w_ada, b_ada)

    xl = x[0]
    xc = ctx[0]
    tm_l = 512
    tm_c = Lc
    for i in range(depth):
        want_ctx = i < depth - 1
        ml = [mods[i, 0:1, j * D:(j + 1) * D] for j in range(6)]
        mc = [mods[i, 1:2, j * D:(j + 1) * D] for j in range(6)]
        n1 = norm1_g[i][None]
        n2 = norm2_g[i][None]
        wi = w_in[i]
        o = (0,) + IN_OFFSETS + (sum(IN_SPLITS),)
        w_main = jnp.concatenate([wi[:, o[1]:o[5]], wi[:, o[0]:o[1]], wi[:, o[7]:o[8]], wi[:, o[10]:o[12]],
                                  wi[:, o[8]:o[10]]], axis=1).astype(BF16)
        w_ab = jnp.pad(wi[:, AB_LO:AB_LO + AB_W], ((0, 0), (0, LANES - AB_W))).astype(BF16)
        wgt = w_gate[i].astype(BF16)
        wbr = w_branch[i].astype(BF16)
        wo = w_out[i].astype(BF16)
        wr_t = w_router[i].T.astype(BF16)
        br = b_router[i][:, None]
        weg, weu, wed = w_e_gate[i], w_e_up[i], w_e_down[i]

        proj, ab, h = _inproj(xl, n1, ml[0], ml[1], w_main, w_ab, tm_l)
        projc, abc, hc = _inproj(xc, n1, mc[0], mc[1], w_main, w_ab, tm_c)
        y_pool = _pool(proj, pool_w[i], pool_scale[i][None], tm_l)
        y_dn, yc_dn = _deltanet(proj, ab, projc, abc, dn_conv_w[i], dn_a_log[i], dn_dt_bias[i], dn_norm_g[i],
                                want_ctx, tm_l)
        y_swa, yc_swa = _swa(proj, projc, swa_sink[i], want_ctx, tm_l)
        y_sgu = _sgu(proj, sgu_norm_g[i][None], sgu_w[i], sgu_b[i], tm_l)
        ys = (y_pool, y_dn, y_swa, y_sgu)

        acc = _merge(h, ys, wgt, wbr, tm_l)
        xl, h2, aff_t = _outproj(acc, xl, wo, ml[2], n2, ml[3], ml[4], wr_t, br, tm_l)
        xl = _moe(xl, h2, aff_t, ml[5], weg, weu, wed, final_g[None], i == depth - 1)

        if want_ctx:
            yc_pool = _pool(projc, pool_w[i], pool_scale[i][None], tm_c)
            yc_sgu = _sgu(projc, sgu_norm_g[i][None], sgu_w[i], sgu_b[i], tm_c)
            ycs = (yc_pool, yc_dn, yc_swa, yc_sgu)
            accc = _merge(hc, ycs, wgt, wbr, tm_c)
            xc, hc2, affc_t = _outproj(accc, xc, wo, mc[2], n2, mc[3], mc[4], wr_t, br, tm_c)
            xc = _moe(xc, hc2, affc_t, mc[5], weg, weu, wed, final_g[None], False)

    return xl[None]
```

```python
import functools
import math

import numpy as np
import jax
import jax.numpy as jnp
from jax import lax
from jax.experimental import pallas as pl
from jax.experimental.pallas import tpu as pltpu

F32 = jnp.float32
BF16 = jnp.bfloat16

D_MODEL = 2048
GRID_W = 64
N_BRANCH = 4
BRANCH_W = D_MODEL // N_BRANCH
POOL_WINDOWS = (2, 4, 8, 16)
POOL_GROUP_W = BRANCH_W // len(POOL_WINDOWS)
DN_DK = 128
DN_DV = 128
DN_HEADS = BRANCH_W // DN_DV
DN_CHUNK = 64
SWA_HD = 64
SWA_Q_HEADS = BRANCH_W // SWA_HD
SWA_KV_HEADS = SWA_Q_HEADS // 4
SWA_GROUP = SWA_Q_HEADS // SWA_KV_HEADS
SWA_WINDOW = 128
SWA_BLOCK = 128
ROPE_BASE = 10000.0
SGU_CHUNK = 128
SGU_GROUPS = 4
SGU_GROUP_W = BRANCH_W // SGU_GROUPS
N_EXPERTS = 16
EC_FACTOR = 2
RMS_EPS = 1e-6

IN_SPLITS = (BRANCH_W,
             DN_HEADS * DN_DK, DN_HEADS * DN_DK,
             DN_HEADS * DN_DV, DN_HEADS * DN_DV,
             2 * DN_HEADS, 2 * DN_HEADS,
             SWA_Q_HEADS * SWA_HD, SWA_KV_HEADS * SWA_HD, SWA_KV_HEADS * SWA_HD,
             BRANCH_W, BRANCH_W)
IN_OFFSETS = tuple(int(o) for o in np.cumsum(IN_SPLITS)[:-1])
AB_LO = IN_OFFSETS[4]
AB_W = 4 * DN_HEADS
MAIN_W = sum(IN_SPLITS) - AB_W
COL_DZ, COL_P, COL_SQ, COL_GU, COL_GV, COL_SK, COL_SV = 1536, 2048, 2560, 3072, 3584, 4096, 4224
LANES = 128
MOE_TILE = 256
MOE_ALIGN = 16
MOE_WIN = 80
MIB = 2 ** 20


def _cparams(sem, vmem_mib):
    return pltpu.CompilerParams(dimension_semantics=sem, vmem_limit_bytes=vmem_mib * MIB)


def _ada_kernel(c_ref, w_ref, b_ref, o_ref):
    c = c_ref[...]
    s = (c * jax.nn.sigmoid(c)).astype(BF16)
    o_ref[0] = jnp.dot(s, w_ref[0].astype(BF16), preferred_element_type=F32) + b_ref[0]


def _ada(cond8, w_ada, b_ada):
    L, D, N = w_ada.shape
    tn = 1024
    return pl.pallas_call(
        _ada_kernel,
        out_shape=jax.ShapeDtypeStruct((L, 8, N), F32),
        grid=(L, N // tn),
        in_specs=[pl.BlockSpec((8, D), lambda l, n: (0, 0)),
                  pl.BlockSpec((1, D, tn), lambda l, n: (l, 0, n)),
                  pl.BlockSpec((1, 1, tn), lambda l, n: (l, 0, n))],
        out_specs=pl.BlockSpec((1, 8, tn), lambda l, n: (l, 0, n)),
        compiler_params=_cparams(("parallel", "parallel"), 32),
        name="ada_mod",
    )(cond8, w_ada, b_ada.reshape(L, 1, N))


def _w_in_prep_kernel(w_ref, main_ref, ab_ref):
    o = (0,) + IN_OFFSETS + (sum(IN_SPLITS),)
    pos = 0
    for lo, hi in ((o[1], o[5]), (o[0], o[1]), (o[7], o[8]), (o[10], o[12]), (o[8], o[10])):
        main_ref[:, pos:pos + hi - lo] = w_ref[:, lo:hi].astype(main_ref.dtype)
        pos += hi - lo
    ab_ref[...] = jnp.zeros(ab_ref.shape, ab_ref.dtype)
    ab_ref[:, 0:AB_W] = w_ref[:, AB_LO:AB_LO + AB_W].astype(ab_ref.dtype)


def _w_in_prep(w_in):
    D, N = w_in.shape
    tk = 256
    return pl.pallas_call(
        _w_in_prep_kernel,
        out_shape=(jax.ShapeDtypeStruct((D, MAIN_W), BF16), jax.ShapeDtypeStruct((D, LANES), BF16)),
        grid=(D // tk,),
        in_specs=[pl.BlockSpec((tk, N), lambda r: (r, 0))],
        out_specs=(pl.BlockSpec((tk, MAIN_W), lambda r: (r, 0)), pl.BlockSpec((tk, LANES), lambda r: (r, 0))),
        compiler_params=_cparams(("parallel",), 32),
        name="w_in_prep",
    )(w_in)


def _inproj_kernel(x_ref, g_ref, sh_ref, sc_ref, w_ref, wab_ref, proj_ref, ab_ref, h_ref, hs_ref):
    @pl.when(pl.program_id(1) == 0)
    def _():
        x = x_ref[...]
        y = x * lax.rsqrt(jnp.mean(x * x, axis=-1, keepdims=True) + RMS_EPS) * g_ref[...]
        hb = (y * (1.0 + sc_ref[...]) + sh_ref[...]).astype(BF16)
        hs_ref[...] = hb
        h_ref[...] = hb
        ab_ref[...] = jnp.dot(hb, wab_ref[...], preferred_element_type=F32)

    proj_ref[...] = jnp.dot(hs_ref[...], w_ref[...], preferred_element_type=F32).astype(proj_ref.dtype)


def _inproj(x, g, sh, sc, w_main, w_ab, tm):
    M, D = x.shape
    N = w_main.shape[1]
    tn = N // 2
    vec = pl.BlockSpec((1, D), lambda m, n: (0, 0))
    return pl.pallas_call(
        _inproj_kernel,
        out_shape=(jax.ShapeDtypeStruct((M, N), BF16),
                   jax.ShapeDtypeStruct((M, LANES), F32),
                   jax.ShapeDtypeStruct((M, D), BF16)),
        grid=(M // tm, N // tn),
        in_specs=[pl.BlockSpec((tm, D), lambda m, n: (m, 0)), vec, vec, vec,
                  pl.BlockSpec((D, tn), lambda m, n: (0, n)),
                  pl.BlockSpec((D, LANES), lambda m, n: (0, 0))],
        out_specs=(pl.BlockSpec((tm, tn), lambda m, n: (m, n)),
                   pl.BlockSpec((tm, LANES), lambda m, n: (m, 0)),
                   pl.BlockSpec((tm, D), lambda m, n: (m, 0))),
        scratch_shapes=[pltpu.VMEM((tm, D), BF16)],
        compiler_params=_cparams(("parallel", "arbitrary"), 48),
        name="inproj",
    )(x, g, sh, sc, w_main, w_ab)


def _merge_kernel(h_ref, y0_ref, y1_ref, y2_ref, y3_ref, wg_ref, wb_ref, o_ref):
    h = h_ref[...]
    acc = None
    for i, y_ref in enumerate((y0_ref, y1_ref, y2_ref, y3_ref)):
        gate = jnp.dot(h, wg_ref[i], preferred_element_type=F32)
        br = jnp.dot(y_ref[...], wb_ref[i], preferred_element_type=F32)
        t = jax.nn.sigmoid(gate) * br
        acc = t if acc is None else acc + t
    o_ref[...] = acc.astype(o_ref.dtype)


def _merge(h, ys, w_gate, w_branch, tm):
    M, D = h.shape
    tn = 512
    ybs = pl.BlockSpec((tm, BRANCH_W), lambda m, n: (m, 0))
    return pl.pallas_call(
        _merge_kernel,
        out_shape=jax.ShapeDtypeStruct((M, D), BF16),
        grid=(M // tm, D // tn),
        in_specs=[pl.BlockSpec((tm, D), lambda m, n: (m, 0)), ybs, ybs, ybs, ybs,
                  pl.BlockSpec((N_BRANCH, D, tn), lambda m, n: (0, 0, n)),
                  pl.BlockSpec((N_BRANCH, BRANCH_W, tn), lambda m, n: (0, 0, n))],
        out_specs=pl.BlockSpec((tm, tn), lambda m, n: (m, n)),
        compiler_params=_cparams(("parallel", "arbitrary"), 48),
        name="merge",
    )(h, *ys, w_gate, w_branch)


def _outproj_kernel(acc_ref, x_ref, w_ref, g1_ref, ng_ref, sh_ref, sc_ref, wr_ref, br_ref,
                    xo_ref, h2_ref, aff_ref):
    out = jnp.dot(acc_ref[...], w_ref[...], preferred_element_type=F32)
    xn = x_ref[...] + g1_ref[...] * out
    xo_ref[...] = xn
    y = xn * lax.rsqrt(jnp.mean(xn * xn, axis=-1, keepdims=True) + RMS_EPS) * ng_ref[...]
    h2 = (y * (1.0 + sc_ref[...]) + sh_ref[...]).astype(BF16)
    h2_ref[...] = h2
    logits = lax.dot_general(wr_ref[...], h2, (((1,), (1,)), ((), ())), preferred_element_type=F32)
    logits = logits + br_ref[...]
    e = jnp.exp(logits - jnp.max(logits, axis=0, keepdims=True))
    aff = e / jnp.sum(e, axis=0, keepdims=True)
    for j in range(aff_ref.shape[0]):
        aff_ref[j] = aff[:, j * MOE_TILE:(j + 1) * MOE_TILE]


def _outproj(acc, x, w_out, g1, ng, sh, sc, w_router_t, b_router, tm):
    M, D = x.shape
    E = w_router_t.shape[0]
    vec = pl.BlockSpec((1, D), lambda m: (0, 0))
    return pl.pallas_call(
        _outproj_kernel,
        out_shape=(jax.ShapeDtypeStruct((M, D), F32),
                   jax.ShapeDtypeStruct((M, D), BF16),
                   jax.ShapeDtypeStruct((M // MOE_TILE, E, MOE_TILE), F32)),
        grid=(M // tm,),
        in_specs=[pl.BlockSpec((tm, D), lambda m: (m, 0)),
                  pl.BlockSpec((tm, D), lambda m: (m, 0)),
                  pl.BlockSpec((D, D), lambda m: (0, 0)),
                  vec, vec, vec, vec,
                  pl.BlockSpec((E, D), lambda m: (0, 0)),
                  pl.BlockSpec((E, 1), lambda m: (0, 0))],
        out_specs=(pl.BlockSpec((tm, D), lambda m: (m, 0)),
                   pl.BlockSpec((tm, D), lambda m: (m, 0)),
                   pl.BlockSpec((tm // MOE_TILE, E, MOE_TILE), lambda m: (m, 0, 0))),
        compiler_params=_cparams(("parallel",), 48),
        name="outproj",
    )(acc, x, w_out, g1, ng, sh, sc, w_router_t, b_router)


def _route_kernel(aff_ref, u_ref, l_ref, rank_ref, base_ref, tot_ref, *, cap):
    aff = aff_ref[...]
    nt, E, W = aff.shape

    def count(mask):
        return jnp.sum(jnp.sum(mask.astype(jnp.int32), axis=0), axis=1, keepdims=True)

    def search(it, thr):
        cand = thr | lax.shift_left(jnp.int32(1), 30 - it)
        c = count(aff >= lax.bitcast_convert_type(cand, F32)[None])
        return jnp.where(c >= cap, cand, thr)

    thr = lax.bitcast_convert_type(lax.fori_loop(0, 31, search, jnp.zeros((E, 1), jnp.int32)), F32)[None]
    gt = aff > thr
    eq = aff == thr
    need = (cap - count(gt)).astype(F32)[None]

    def prefix(mask):
        m2 = jnp.where(mask, 1.0, 0.0).astype(BF16).reshape(nt * E, W)
        incl = jnp.dot(m2, u_ref[...], preferred_element_type=F32)
        tot = jnp.broadcast_to(incl[:, W - 1:W], (nt * E, LANES))
        base = jnp.dot(l_ref[...], tot.astype(BF16), preferred_element_type=F32)
        return (incl - m2.astype(F32) + base[:, 0:1]).reshape(nt, E, W), base, tot

    eq_rank, _, _ = prefix(eq)
    sel = gt | (eq & (eq_rank < need))
    rank, base, tot = prefix(sel)
    rank_ref[...] = jnp.where(sel, rank.astype(jnp.int32), -1)
    base_ref[...] = base.astype(jnp.int32)
    tot_ref[...] = tot.astype(jnp.int32)


def _route(aff_tm, cap):
    nt, E, W = aff_tm.shape
    upper = np.triu(np.ones((W, W), np.float32))
    idx = np.arange(nt * E)
    lower = ((idx[:, None] % E) == (idx[None, :] % E)) & ((idx[None, :] // E) < (idx[:, None] // E))
    full = lambda a: pl.BlockSpec(a.shape, lambda: (0,) * a.ndim)
    consts = (jnp.asarray(upper, BF16), jnp.asarray(lower, BF16))
    return pl.pallas_call(
        functools.partial(_route_kernel, cap=cap),
        out_shape=(jax.ShapeDtypeStruct((nt, E, W), jnp.int32), jax.ShapeDtypeStruct((nt * E, LANES), jnp.int32),
                   jax.ShapeDtypeStruct((nt * E, LANES), jnp.int32)),
        in_specs=[full(aff_tm), full(consts[0]), full(consts[1])],
        out_specs=(pl.BlockSpec((nt, E, W), lambda: (0, 0, 0)), pl.BlockSpec((nt * E, LANES), lambda: (0, 0)),
                   pl.BlockSpec((nt * E, LANES), lambda: (0, 0))),
        compiler_params=pltpu.CompilerParams(vmem_limit_bytes=40 * MIB),
        name="moe_route",
    )(aff_tm, *consts)


def _slot_onehots(rank, cur, lo_ref, cnt_ref, b, R, clamp_hi):
    E = rank.shape[0]
    riota = lax.broadcasted_iota(jnp.int32, (R, rank.shape[1]), 0)
    ohs, starts, news = [], [], []
    for e in range(E):
        c = cur[e]
        end = lo_ref[b * E + e] + cnt_ref[b * E + e]
        w = (c // MOE_ALIGN) * MOE_ALIGN
        if clamp_hi is not None:
            w = jnp.minimum(w, clamp_hi)
        rk = rank[e:e + 1, :]
        ohs.append(((rk - w) == riota) & (rk >= c))
        starts.append(w)
        news.append(jnp.minimum(end, w + R))
    return ohs, starts, news


def _gather_kernel(lo_ref, cnt_ref, h_ref, rank_ref, aff_ref, xe_ref, xg_ref,
                   stage, gstage, carry, gcarry, cur, npass, sem, *, R, cap):
    b = pl.program_id(0)
    E = rank_ref.shape[1]
    A = MOE_ALIGN

    @pl.when(b == 0)
    def _():
        carry[...] = jnp.zeros_like(carry)
        gcarry[...] = jnp.zeros_like(gcarry)
        npass[0] = 0
        stage[1, 0:R, :] = jnp.zeros((R, stage.shape[2]), stage.dtype)
        gstage[1, 0:R, :] = jnp.zeros((R, LANES), F32)
        pads = []
        for e in range(E):
            pads.append(pltpu.make_async_copy(stage.at[1, pl.ds(0, R)], xe_ref.at[e, pl.ds(cap, R)], sem.at[1]))
            pads.append(pltpu.make_async_copy(gstage.at[1, pl.ds(0, R)], xg_ref.at[e, pl.ds(cap, R)], sem.at[1]))
        for cp in pads:
            cp.start()
        for cp in pads:
            cp.wait()

    for e in range(E):
        cur[e] = lo_ref[b * E + e]
    rank = rank_ref[0]
    aff = aff_ref[0]
    h = h_ref[...]
    srow = lax.broadcasted_iota(jnp.int32, (A, 1), 0)

    def copies(slot, starts):
        cps = []
        for e in range(E):
            w = pl.multiple_of(starts[e], A)
            cps.append(pltpu.make_async_copy(stage.at[slot, pl.ds(e * R, R)], xe_ref.at[e, pl.ds(w, R)], sem.at[slot]))
            cps.append(pltpu.make_async_copy(gstage.at[slot, pl.ds(e * R, R)], xg_ref.at[e, pl.ds(w, R)], sem.at[slot]))
        return cps

    def one_pass(more):
        g = npass[0]
        slot = g % 2
        ohs, starts, news = _slot_onehots(rank, cur, lo_ref, cnt_ref, b, R, None)
        rows = jnp.dot(jnp.concatenate([jnp.where(oh, 1.0, 0.0).astype(BF16) for oh in ohs], axis=0), h,
                       preferred_element_type=F32)
        more = jnp.int32(0)
        for e in range(E):
            gate = jnp.sum(jnp.where(ohs[e], aff[e:e + 1, :], 0.0), axis=1, keepdims=True)
            gate = jnp.broadcast_to(gate, (R, LANES))
            stage[slot, e * R:e * R + A, :] = (rows[e * R:e * R + A] + carry[e]).astype(stage.dtype)
            stage[slot, e * R + A:(e + 1) * R, :] = rows[e * R + A:(e + 1) * R].astype(stage.dtype)
            gstage[slot, e * R:e * R + A, :] = gate[:A] + gcarry[e]
            gstage[slot, e * R + A:(e + 1) * R, :] = gate[A:]
            new = news[e]
            grp = jnp.minimum((new - starts[e]) // A, R // A - 1)
            keep = srow < (new - (new // A) * A)
            off = pl.multiple_of(e * R + grp * A, A)
            carry[e] = jnp.where(keep, stage[slot, pl.ds(off, A), :].astype(F32), 0.0)
            gcarry[e] = jnp.where(keep, gstage[slot, pl.ds(off, A), :], 0.0)
            cur[e] = new
            more = jnp.maximum(more, (new < lo_ref[b * E + e] + cnt_ref[b * E + e]).astype(jnp.int32))

        @pl.when(g > 0)
        def _():
            for cp in copies(1 - slot, [0] * E):
                cp.wait()

        for cp in copies(slot, starts):
            cp.start()
        npass[0] = g + 1
        return more

    lax.while_loop(lambda more: more > 0, one_pass, jnp.int32(1))

    @pl.when(b == pl.num_programs(0) - 1)
    def _():
        for cp in copies((npass[0] - 1) % 2, [0] * E):
            cp.wait()


def _gather(h2, rank, aff_tm, lo, cnt, cap, R):
    n, D = h2.shape
    nt, E, W = rank.shape
    grid_spec = pltpu.PrefetchScalarGridSpec(
        num_scalar_prefetch=2,
        grid=(nt,),
        in_specs=[pl.BlockSpec((W, D), lambda b, lo, cnt: (b, 0)),
                  pl.BlockSpec((1, E, W), lambda b, lo, cnt: (b, 0, 0)),
                  pl.BlockSpec((1, E, W), lambda b, lo, cnt: (b, 0, 0))],
        out_specs=(pl.BlockSpec(memory_space=pl.ANY), pl.BlockSpec(memory_space=pl.ANY)),
        scratch_shapes=[pltpu.VMEM((2, E * R, D), BF16), pltpu.VMEM((2, E * R, LANES), F32),
                        pltpu.VMEM((E, MOE_ALIGN, D), F32), pltpu.VMEM((E, MOE_ALIGN, LANES), F32),
                        pltpu.SMEM((E,), jnp.int32), pltpu.SMEM((1,), jnp.int32),
                        pltpu.SemaphoreType.DMA((2,))])
    return pl.pallas_call(
        functools.partial(_gather_kernel, R=R, cap=cap),
        out_shape=(jax.ShapeDtypeStruct((E, cap + R, D), BF16), jax.ShapeDtypeStruct((E, cap + R, LANES), F32)),
        grid_spec=grid_spec,
        compiler_params=_cparams(("arbitrary",), 48),
        name="moe_gather",
    )(lo, cnt, h2, rank, aff_tm)


FFN_STEPS = 4


def _ffn_kernel(*refs, has_ctx):
    if has_ctx:
        x_ref, g_ref, xc_ref, gc_ref, wg_ref, wu_ref, wd_ref, o_ref, oc_ref, wgs, wus, wds = refs
    else:
        x_ref, g_ref, wg_ref, wu_ref, wd_ref, o_ref, wgs, wus, wds = refs
    e = pl.program_id(0)
    s = pl.program_id(1)
    n_exp = pl.num_programs(0) - 1
    fq = wg_ref.shape[2]

    @pl.when(e < n_exp)
    def _():
        slot = e % 2
        for j in range(FFN_STEPS):

            @pl.when(s == j)
            def _():
                wgs[slot, :, j * fq:(j + 1) * fq] = wg_ref[0].astype(BF16)
                wus[slot, :, j * fq:(j + 1) * fq] = wu_ref[0].astype(BF16)
                wds[slot, j * fq:(j + 1) * fq, :] = wd_ref[0].astype(BF16)

    def expert(x, gate, slot):
        a = jnp.dot(x, wgs[slot], preferred_element_type=F32)
        b = jnp.dot(x, wus[slot], preferred_element_type=F32)
        hid = (a * jax.nn.sigmoid(a) * b).astype(BF16)
        return jnp.dot(hid, wds[slot], preferred_element_type=F32) * gate[:, 0:1]

    @pl.when(e >= 1)
    def _():
        slot = (e - 1) % 2
        o_ref[0] = expert(x_ref[0], g_ref[0], slot).astype(o_ref.dtype)
        if has_ctx:

            @pl.when(s == 0)
            def _():
                oc_ref[0] = expert(xc_ref[0], gc_ref[0], slot).astype(oc_ref.dtype)


def _ffn(xe, xg, wg, wu, wd, cap, ctx=None):
    E, _, D = xe.shape
    F = wg.shape[2]
    S = FFN_STEPS
    M = S if cap % (S * LANES) == 0 else 1
    tm = cap // M
    fq = F // S
    rows = lambda e, s: (jnp.maximum(e - 1, 0), jnp.where(e == 0, 0, s % M), 0)
    crow = lambda e, s: (jnp.maximum(e - 1, 0), 0, 0)
    in_specs = [pl.BlockSpec((1, tm, D), rows), pl.BlockSpec((1, tm, LANES), rows)]
    out_shape = [jax.ShapeDtypeStruct((E, cap, D), BF16)]
    out_specs = [pl.BlockSpec((1, tm, D), rows)]
    args = [xe, xg]
    if ctx is not None:
        xec, xgc, capc = ctx
        in_specs += [pl.BlockSpec((1, capc, D), crow), pl.BlockSpec((1, capc, LANES), crow)]
        out_shape.append(jax.ShapeDtypeStruct((E, capc, D), BF16))
        out_specs.append(pl.BlockSpec((1, capc, D), crow))
        args += [xec, xgc]
    in_specs += [pl.BlockSpec((1, D, fq), lambda e, s: (jnp.minimum(e, E - 1), 0, s)),
                 pl.BlockSpec((1, D, fq), lambda e, s: (jnp.minimum(e, E - 1), 0, s)),
                 pl.BlockSpec((1, fq, D), lambda e, s: (jnp.minimum(e, E - 1), s, 0))]
    return pl.pallas_call(
        functools.partial(_ffn_kernel, has_ctx=ctx is not None),
        out_shape=tuple(out_shape),
        grid=(E + 1, S),
        in_specs=in_specs,
        out_specs=tuple(out_specs),
        scratch_shapes=[pltpu.VMEM((2, D, F), BF16), pltpu.VMEM((2, D, F), BF16), pltpu.VMEM((2, F, D), BF16)],
        compiler_params=_cparams(("arbitrary", "arbitrary"), 52),
        name="expert_ffn",
    )(*args, wg, wu, wd)


def _combine_kernel(lo_ref, cnt_ref, rank_ref, x_ref, g2_ref, ng_ref, y_ref, o_ref, ystage, yovf, acc, cur, sem, osem,
                    *, R, cap, final_norm):
    b = pl.program_id(0)
    nb = pl.num_programs(0)
    E = rank_ref.shape[1]
    A = MOE_ALIGN

    def window(c):
        return pl.multiple_of(jnp.minimum((c // A) * A, cap - R), A)

    def main_copies(tile, slot):
        return [pltpu.make_async_copy(y_ref.at[e, pl.ds(window(lo_ref[tile * E + e]), R)],
                                      ystage.at[slot, pl.ds(e * R, R)], sem.at[slot]) for e in range(E)]

    @pl.when(b == 0)
    def _():
        for cp in main_copies(0, 0):
            cp.start()

    @pl.when(b + 1 < nb)
    def _():
        for cp in main_copies(b + 1, (b + 1) % 2):
            cp.start()

    for e in range(E):
        cur[e] = lo_ref[b * E + e]
    rank = rank_ref[0]

    def expand(y_rows):
        ohs, _, news = _slot_onehots(rank, cur, lo_ref, cnt_ref, b, R, cap - R)
        more = jnp.int32(0)
        for e in range(E):
            cur[e] = news[e]
            more = jnp.maximum(more, (news[e] < lo_ref[b * E + e] + cnt_ref[b * E + e]).astype(jnp.int32))
        oh = jnp.concatenate([jnp.where(o, 1.0, 0.0).astype(BF16) for o in ohs], axis=0)
        return lax.dot_general(oh, y_rows, (((0,), (0,)), ((), ())), preferred_element_type=F32), more

    for cp in main_copies(b, b % 2):
        cp.wait()
    first, more = expand(ystage[b % 2])
    acc[...] = first

    def extra_pass(more):
        cps = [pltpu.make_async_copy(y_ref.at[e, pl.ds(window(cur[e]), R)], yovf.at[pl.ds(e * R, R)], osem.at[0])
               for e in range(E)]
        for cp in cps:
            cp.start()
        for cp in cps:
            cp.wait()
        part, more = expand(yovf[...])
        acc[...] += part
        return more

    lax.while_loop(lambda more: more > 0, extra_pass, more)
    out = x_ref[...] + g2_ref[...] * acc[...]
    if final_norm:
        out = out * lax.rsqrt(jnp.mean(out * out, axis=-1, keepdims=True) + RMS_EPS) * ng_ref[...]
    o_ref[...] = out


def _combine(y, rank, lo, cnt, x, g2, norm_g, final_norm, cap, R):
    n, D = x.shape
    nt, E, W = rank.shape
    grid_spec = pltpu.PrefetchScalarGridSpec(
        num_scalar_prefetch=2,
        grid=(nt,),
        in_specs=[pl.BlockSpec((1, E, W), lambda b, lo, cnt: (b, 0, 0)),
                  pl.BlockSpec((W, D), lambda b, lo, cnt: (b, 0)),
                  pl.BlockSpec((1, D), lambda b, lo, cnt: (0, 0)),
                  pl.BlockSpec((1, D), lambda b, lo, cnt: (0, 0)),
                  pl.BlockSpec(memory_space=pl.ANY)],
        out_specs=pl.BlockSpec((W, D), lambda b, lo, cnt: (b, 0)),
        scratch_shapes=[pltpu.VMEM((2, E * R, D), BF16), pltpu.VMEM((E * R, D), BF16), pltpu.VMEM((W, D), F32),
                        pltpu.SMEM((E,), jnp.int32), pltpu.SemaphoreType.DMA((2,)), pltpu.SemaphoreType.DMA((1,))])
    return pl.pallas_call(
        functools.partial(_combine_kernel, R=R, cap=cap, final_norm=final_norm),
        out_shape=jax.ShapeDtypeStruct((n, D), F32),
        grid_spec=grid_spec,
        compiler_params=_cparams(("arbitrary",), 48),
        name="moe_combine",
    )(lo, cnt, rank, x, g2, norm_g, y)


def _moe_dispatch(h2, aff_tm):
    n = h2.shape[0]
    cap = EC_FACTOR * n // aff_tm.shape[1]
    R = min(MOE_WIN, cap)
    rank, base, tot = _route(aff_tm, cap)
    lo, cnt = base[:, 0], tot[:, 0]
    xe, xg = _gather(h2, rank, aff_tm, lo, cnt, cap, R)
    return xe, xg, (rank, lo, cnt, cap, R)


def _moe(xl, h2, aff_tm, g2, xc, hc2, affc_tm, gc2, wg, wu, wd, norm_g, final_norm):
    xe, xg, (rank, lo, cnt, cap, R) = _moe_dispatch(h2, aff_tm)
    if xc is None:
        (y,) = _ffn(xe, xg, wg, wu, wd, cap)
        return _combine(y, rank, lo, cnt, xl, g2, norm_g, final_norm, cap, R), None
    xec, xgc, (rankc, loc, cntc, capc, Rc) = _moe_dispatch(hc2, affc_tm)
    y, yc = _ffn(xe, xg, wg, wu, wd, cap, (xec, xgc, capc))
    return (_combine(y, rank, lo, cnt, xl, g2, norm_g, final_norm, cap, R),
            _combine(yc, rankc, loc, cntc, xc, gc2, norm_g, False, capc, Rc))


POOL_HALO = 16


def _pool_kernel(cur_ref, prev_ref, next_ref, w_ref, sc_ref, y_ref, scr, *, seq_len):
    m = pl.program_id(0)
    tm = cur_ref.shape[0]
    H = POOL_HALO
    scr[0:H, :] = jnp.where(m > 0, prev_ref[...].astype(F32), 0.0)
    scr[H:H + tm, :] = cur_ref[...].astype(F32)
    scr[H + tm:2 * H + tm, :] = jnp.where(m < pl.num_programs(0) - 1, next_ref[...].astype(F32), 0.0)
    t = m * tm + lax.broadcasted_iota(jnp.int32, (tm, 1), 0)
    for gi, w in enumerate(POOL_WINDOWS):
        cols = slice(gi * POOL_GROUP_W, (gi + 1) * POOL_GROUP_W)
        total = None
        for k in range(-(w // 2), w // 2):
            term = scr[H + k:H + k + tm, cols]
            total = term if total is None else total + term
        n = jnp.minimum(t + (w // 2 - 1), seq_len - 1) - jnp.maximum(t - w // 2, 0) + 1
        centred = total / n.astype(F32) - scr[H:H + tm, cols]
        y = jnp.dot(centred.astype(BF16), w_ref[gi].astype(BF16), preferred_element_type=F32)
        y_ref[:, cols] = (y * sc_ref[:, cols]).astype(y_ref.dtype)


def _pool(proj, pool_w, pool_scale, tm):
    M = proj.shape[0]
    hb = tm // POOL_HALO
    nhb = M // POOL_HALO
    cb = COL_P // BRANCH_W
    return pl.pallas_call(
        functools.partial(_pool_kernel, seq_len=M),
        out_shape=jax.ShapeDtypeStruct((M, BRANCH_W), BF16),
        grid=(M // tm,),
        in_specs=[pl.BlockSpec((tm, BRANCH_W), lambda m: (m, cb)),
                  pl.BlockSpec((POOL_HALO, BRANCH_W), lambda m: (jnp.maximum(m * hb - 1, 0), cb)),
                  pl.BlockSpec((POOL_HALO, BRANCH_W), lambda m: (jnp.minimum((m + 1) * hb, nhb - 1), cb)),
                  pl.BlockSpec(pool_w.shape, lambda m: (0, 0, 0)),
                  pl.BlockSpec((1, BRANCH_W), lambda m: (0, 0))],
        out_specs=pl.BlockSpec((tm, BRANCH_W), lambda m: (m, 0)),
        scratch_shapes=[pltpu.VMEM((tm + 2 * POOL_HALO, BRANCH_W), F32)],
        compiler_params=_cparams(("parallel",), 32),
        name="pool_mix",
    )(proj, proj, proj, pool_w, pool_scale)


def _gelu(x):
    return 0.5 * x * (1.0 + lax.erf(x * (2.0 ** -0.5)))


def _sgu_kernel(u_ref, v_ref, g_ref, w_ref, b_ref, y_ref):
    tm = u_ref.shape[0]
    v = _gelu(v_ref[...].astype(F32))
    v = (v * lax.rsqrt(jnp.mean(v * v, axis=-1, keepdims=True) + RMS_EPS) * g_ref[...]).astype(BF16)
    u = _gelu(u_ref[...].astype(F32))
    for c in range(tm // SGU_CHUNK):
        rows = slice(c * SGU_CHUNK, (c + 1) * SGU_CHUNK)
        for g in range(SGU_GROUPS):
            cols = slice(g * SGU_GROUP_W, (g + 1) * SGU_GROUP_W)
            mixed = jnp.dot(w_ref[g].astype(BF16), v[rows, cols], preferred_element_type=F32) + b_ref[:, g:g + 1]
            y_ref[rows, cols] = (u[rows, cols] * mixed).astype(y_ref.dtype)


def _sgu(proj, norm_g, w_s, b_s, tm):
    M = proj.shape[0]
    bt = jnp.pad(b_s.T, ((0, 0), (0, LANES - SGU_GROUPS)))
    return pl.pallas_call(
        _sgu_kernel,
        out_shape=jax.ShapeDtypeStruct((M, BRANCH_W), BF16),
        grid=(M // tm,),
        in_specs=[pl.BlockSpec((tm, BRANCH_W), lambda m: (m, COL_GU // BRANCH_W)),
                  pl.BlockSpec((tm, BRANCH_W), lambda m: (m, COL_GV // BRANCH_W)),
                  pl.BlockSpec((1, BRANCH_W), lambda m: (0, 0)),
                  pl.BlockSpec(w_s.shape, lambda m: (0, 0, 0)),
                  pl.BlockSpec((SGU_CHUNK, LANES), lambda m: (0, 0))],
        out_specs=pl.BlockSpec((tm, BRANCH_W), lambda m: (m, 0)),
        compiler_params=_cparams(("parallel",), 32),
        name="sgu_mix",
    )(proj, proj, norm_g, w_s, bt)


DN_CONV = 5
DN_HALO = 16
DN_QKV_W = 3 * BRANCH_W
DN_STATES = 2 * DN_HEADS


def _dn_prep_kernel(cur_ref, prev_ref, next_ref, ab_ref, cw_ref, alog_ref, dtb_ref,
                    q_ref, k_ref, v_ref, aux_ref, scr):
    m = pl.program_id(0)
    tm = cur_ref.shape[0]
    scr[0:DN_HALO, :] = jnp.where(m > 0, prev_ref[...].astype(F32), 0.0)
    scr[DN_HALO:DN_HALO + tm, :] = cur_ref[...].astype(F32)
    scr[DN_HALO + tm:2 * DN_HALO + tm, :] = jnp.where(m < pl.num_programs(0) - 1, next_ref[...].astype(F32), 0.0)
    acc = None
    for t in range(DN_CONV):
        lo = DN_HALO - DN_CONV // 2 + t
        term = scr[lo:lo + tm, :] * cw_ref[t:t + 1, :]
        acc = term if acc is None else acc + term
    y = acc * jax.nn.sigmoid(acc)
    for h in range(DN_HEADS):
        qh = y[:, h * DN_DK:(h + 1) * DN_DK]
        q_ref[:, h * DN_DK:(h + 1) * DN_DK] = (
            qh * lax.rsqrt(jnp.sum(qh * qh, axis=-1, keepdims=True) + 1e-6) * DN_DK ** -0.5)
        kh = y[:, BRANCH_W + h * DN_DK:BRANCH_W + (h + 1) * DN_DK]
        k_ref[:, h * DN_DK:(h + 1) * DN_DK] = kh * lax.rsqrt(jnp.sum(kh * kh, axis=-1, keepdims=True) + 1e-6)
    v_ref[...] = y[:, 2 * BRANCH_W:]

    ab = ab_ref[...]
    lane = lax.broadcasted_iota(jnp.int32, ab.shape, 1)
    sp = ab + dtb_ref[...]
    softplus = jnp.maximum(sp, 0.0) + jnp.log(1.0 + jnp.exp(-jnp.abs(sp)))
    g = jnp.where(lane < DN_STATES, -jnp.exp(alog_ref[...]) * softplus, 0.0)
    beta = jax.nn.sigmoid(ab)
    g1 = g.astype(BF16)
    r1 = g - g1.astype(F32)
    g2 = r1.astype(BF16)
    g3 = (r1 - g2.astype(F32)).astype(BF16)
    gs = jnp.concatenate([g1, g2, g3], axis=1)
    row = lax.broadcasted_iota(jnp.int32, (tm, tm), 0)
    col = lax.broadcasted_iota(jnp.int32, (tm, tm), 1)
    same = (row // DN_CHUNK) == (col // DN_CHUNK)
    tri_f = jnp.where(same & (col <= row), 1.0, 0.0).astype(BF16)
    tri_b = jnp.where(same & (col >= row), 1.0, 0.0).astype(BF16)
    cf = jnp.dot(tri_f, gs, preferred_element_type=F32)
    cb = jnp.dot(tri_b, gs, preferred_element_type=F32)
    cf = cf[:, :LANES] + cf[:, LANES:2 * LANES] + cf[:, 2 * LANES:]
    cb = cb[:, :LANES] + cb[:, LANES:2 * LANES] + cb[:, 2 * LANES:]
    gc = jnp.where(lane < DN_HEADS, cf, cb)
    aux_ref[...] = jnp.where(lane < DN_STATES, gc, beta)


def _dn_prep(proj, ab, cw, alog, dtb, tm):
    M = proj.shape[0]
    hb = tm // DN_HALO
    nhb = M // DN_HALO
    row = pl.BlockSpec((1, LANES), lambda m: (0, 0))
    return pl.pallas_call(
        _dn_prep_kernel,
        out_shape=(jax.ShapeDtypeStruct((M, BRANCH_W), F32),) * 3 + (jax.ShapeDtypeStruct((M, LANES), F32),),
        grid=(M // tm,),
        in_specs=[pl.BlockSpec((tm, DN_QKV_W), lambda m: (m, 0)),
                  pl.BlockSpec((DN_HALO, DN_QKV_W), lambda m: (jnp.maximum(m * hb - 1, 0), 0)),
                  pl.BlockSpec((DN_HALO, DN_QKV_W), lambda m: (jnp.minimum((m + 1) * hb, nhb - 1), 0)),
                  pl.BlockSpec((tm, LANES), lambda m: (m, 0)),
                  pl.BlockSpec((8, DN_QKV_W), lambda m: (0, 0)), row, row],
        out_specs=(pl.BlockSpec((tm, BRANCH_W), lambda m: (m, 0)),) * 3 + (pl.BlockSpec((tm, LANES), lambda m: (m, 0)),),
        scratch_shapes=[pltpu.VMEM((tm + 2 * DN_HALO, DN_QKV_W), F32)],
        compiler_params=_cparams(("parallel",), 40),
        name="dn_prep",
    )(proj, proj, proj, ab, cw, alog, dtb)


def _dn_constants():
    L, H = DN_CHUNK, DN_HEADS
    WW, FW = DN_STATES * L, DN_STATES * DN_DK
    r = np.arange(L)[:, None]
    lane = np.arange(WW)[None, :]
    c = lane % L
    lower = lane < H * L
    cm = np.zeros((9, L, WW), np.float32)
    cm[0] = r == c
    cm[1] = np.where(np.where(lower, r >= c, r <= c), 0.0, -np.inf)
    cm[2] = np.where(lower, r > c, r < c)
    hi, lo = np.where(lower, r, c), np.where(lower, c, r)
    for j in range(6):
        s = 2 ** j
        cm[3 + j] = ((r // (2 * s)) == (c // (2 * s))) & (hi % (2 * s) >= s) & (lo % (2 * s) < s)
    rw = np.arange(WW)[:, None]
    rf = np.arange(H * DN_DK)[:, None]
    m_ww = rw // L == np.arange(WW)[None] // L
    m_wf = rw // L == np.arange(FW)[None] // DN_DK
    m_w2f = rw // L == (np.arange(2 * FW)[None] % FW) // DN_DK
    m_ff = rf // DN_DK == np.arange(H * DN_DK)[None] // DN_DK
    return (jnp.asarray(cm),) + tuple(jnp.asarray(m, BF16) for m in (m_ww, m_wf, m_w2f, m_ff))


def _dn_scan_kernel(qf_ref, kf_ref, vf_ref, af_ref, qb_ref, kb_ref, vb_ref, ab_ref, s0_ref,
                    cm_ref, mww_ref, mwf_ref, mw2f_ref, mff_ref, of_ref, ob_ref, sfin_ref,
                    s_scr, u_st, w_st, qd_st, kd_st, qk_st, gl_st):
    j = pl.program_id(0)
    L, H = DN_CHUNK, DN_HEADS
    FD = H * DN_DK

    @pl.when(j == 0)
    def _():
        s_scr[...] = s0_ref[...]
        for st in (u_st, w_st, qd_st, kd_st, qk_st, gl_st):
            st[...] = jnp.zeros(st.shape, st.dtype)

    lane = lax.broadcasted_iota(jnp.int32, (L, LANES), 1)
    rowblk = lax.broadcasted_iota(jnp.int32, (FD, DN_DV), 0) // DN_DK
    m_ww, m_wf, m_w2f, m_ff = mww_ref[...], mwf_ref[...], mw2f_ref[...], mff_ref[...]

    def tile_rows(a):
        return jnp.concatenate([a] * DN_STATES, axis=0)

    def col(arr, c):
        return jnp.broadcast_to(arr[:, c:c + 1], (arr.shape[0], LANES))

    def feat(arrs, base):
        return jnp.concatenate([col(arrs[d], base + d * H + h) for d in range(2) for h in range(H)], axis=1)

    def wide(arrs, base):
        return jnp.concatenate([jnp.where(lane < L, col(arrs[d], base + d * H + 2 * p),
                                          col(arrs[d], base + d * H + 2 * p + 1))
                                for d in range(2) for p in range(H // 2)], axis=1)

    u, w, qd, kd, qk = u_st[...], w_st[...], qd_st[...], kd_st[...], qk_st[...]
    g_prev = gl_st[...]
    live = j > 0
    dir_cols = [slice(d * FD, (d + 1) * FD) for d in range(2)]
    s_old = [s_scr[d] for d in range(2)]
    val = {}

    def ws_qs(d):
        s_bd = jnp.concatenate([s_old[d].astype(BF16)] * H, axis=1) * m_ff
        val["ws", d] = jnp.dot(jnp.concatenate([w[:, dir_cols[d]], qd[:, dir_cols[d]]], axis=0).astype(BF16), s_bd,
                               preferred_element_type=F32)

    def outputs():
        val["v"] = (u - jnp.concatenate([val["ws", 0][:L], val["ws", 1][:L]], axis=1)).astype(BF16)
        val["o"] = (jnp.concatenate([val["ws", 0][L:], val["ws", 1][L:]], axis=1)
                    + jnp.dot(qk, tile_rows(val["v"]) * m_wf, preferred_element_type=F32))

    def state(d):
        cross = lax.dot_general(kd[:, dir_cols[d]], val["v"][:, dir_cols[d]], (((0,), (0,)), ((), ())),
                                preferred_element_type=F32)
        fold = None
        g_last = None
        for h in range(H):
            part = jnp.where(rowblk == h, cross[:, h * DN_DV:(h + 1) * DN_DV], 0.0)
            gl = jnp.where(rowblk == h, g_prev[:, d * H + h:d * H + h + 1], 0.0)
            fold = part if fold is None else fold + part
            g_last = gl if g_last is None else g_last + gl
        val["s", d] = jnp.where(live, g_last * s_old[d] + fold, s_old[d])

    queue = [lambda: ws_qs(0), lambda: ws_qs(1), outputs, lambda: state(0), lambda: state(1)]

    def issue_one():
        if queue:
            queue.pop(0)()

    aux = (af_ref[...], ab_ref[...])
    last = (aux[0][L - 1:L, :], aux[1][0:1, :])
    e_gc = tuple(jnp.exp(a) for a in aux)
    e_rem = tuple(jnp.exp(l - a) for l, a in zip(last, aux))
    beta_f = feat(aux, DN_STATES)
    eg_f = feat(e_gc, 0)
    er_f = feat(e_rem, 0)
    gcw = wide(aux, 0)
    eye = cm_ref[0]
    gc_row = jnp.sum(eye * gcw, axis=0, keepdims=True)
    decay = jnp.exp(gcw - gc_row + cm_ref[1])
    q = jnp.concatenate([qf_ref[...], qb_ref[...]], axis=1)
    k = jnp.concatenate([kf_ref[...], kb_ref[...]], axis=1)
    v = jnp.concatenate([vf_ref[...], vb_ref[...]], axis=1)
    kbeta = k * beta_f
    kq = lax.dot_general(jnp.concatenate([kbeta, q], axis=0).astype(BF16), tile_rows(k.astype(BF16)) * m_wf,
                         (((1,), (1,)), ((), ())), preferred_element_type=F32)
    issue_one()
    mm = kq[:L] * decay * cm_ref[2]
    x = eye - mm * cm_ref[3]
    for lvl in range(4, 9):
        xb = x.astype(BF16)
        t = jnp.dot(xb, tile_rows((mm * cm_ref[lvl]).astype(BF16)) * m_ww, preferred_element_type=F32)
        issue_one()
        x = x - jnp.dot(t.astype(BF16), tile_rows(xb) * m_ww, preferred_element_type=F32)
        issue_one()
    rhs = jnp.concatenate([v * beta_f, kbeta * eg_f], axis=1)
    sol = rhs + jnp.dot((x - eye).astype(BF16), tile_rows(rhs.astype(BF16)) * m_w2f, preferred_element_type=F32)
    while queue:
        issue_one()
    for d, o_ref in enumerate((of_ref, ob_ref)):
        s_scr[d] = val["s", d]
        o_ref[...] = val["o"][:, dir_cols[d]]
    u_st[...] = sol[:, :2 * FD]
    w_st[...] = sol[:, 2 * FD:]
    qd_st[...] = q * eg_f
    kd_st[...] = (k * er_f).astype(BF16)
    qk_st[...] = (kq[L:] * decay).astype(BF16)
    gl_st[...] = jnp.where(lane[0:1] < H, jnp.exp(last[0]), jnp.exp(last[1]))

    @pl.when(j == pl.num_programs(0) - 1)
    def _():
        sfin_ref[...] = s_scr[...]


def _dn_scan(q, k, v, aux, s0, consts):
    M = q.shape[0]
    N = M // DN_CHUNK
    FW = 2 * BRANCH_W
    fwd = lambda j: (jnp.minimum(j, N - 1), 0)
    bwd = lambda j: (jnp.maximum(N - 1 - j, 0), 0)
    fwd_o = lambda j: (jnp.maximum(j - 1, 0), 0)
    bwd_o = lambda j: (N - 1 - jnp.maximum(j - 1, 0), 0)
    cs = lambda w, im: pl.BlockSpec((DN_CHUNK, w), im)
    st = pl.BlockSpec(s0.shape, lambda j: (0, 0, 0))
    const_specs = [pl.BlockSpec(c.shape, lambda j, _n=c.ndim: (0,) * _n) for c in consts]
    return pl.pallas_call(
        _dn_scan_kernel,
        out_shape=(jax.ShapeDtypeStruct((M, BRANCH_W), F32), jax.ShapeDtypeStruct((M, BRANCH_W), F32),
                   jax.ShapeDtypeStruct(s0.shape, F32)),
        grid=(N + 1,),
        in_specs=[cs(BRANCH_W, fwd), cs(BRANCH_W, fwd), cs(BRANCH_W, fwd), cs(LANES, fwd),
                  cs(BRANCH_W, bwd), cs(BRANCH_W, bwd), cs(BRANCH_W, bwd), cs(LANES, bwd), st] + const_specs,
        out_specs=(cs(BRANCH_W, fwd_o), cs(BRANCH_W, bwd_o), st),
        scratch_shapes=[pltpu.VMEM(s0.shape, F32),
                        pltpu.VMEM((DN_CHUNK, FW), F32), pltpu.VMEM((DN_CHUNK, FW), F32),
                        pltpu.VMEM((DN_CHUNK, FW), F32), pltpu.VMEM((DN_CHUNK, FW), BF16),
                        pltpu.VMEM((DN_CHUNK, DN_STATES * DN_CHUNK), BF16), pltpu.VMEM((1, LANES), F32)],
        compiler_params=_cparams(("arbitrary",), 32),
        name="dn_scan",
    )(q, k, v, aux, q, k, v, aux, s0, *consts)


def _dn_out_kernel(of_ref, ob_ref, z_ref, g_ref, y_ref):
    o = of_ref[...] + ob_ref[...]
    z = z_ref[...].astype(F32)
    for h in range(DN_HEADS):
        sl = slice(h * DN_DV, (h + 1) * DN_DV)
        oh = o[:, sl]
        zh = z[:, sl]
        yh = oh * lax.rsqrt(jnp.mean(oh * oh, axis=-1, keepdims=True) + RMS_EPS) * g_ref[...]
        y_ref[:, sl] = (yh * (zh * jax.nn.sigmoid(zh))).astype(y_ref.dtype)


def _dn_out(o_f, o_b, proj, norm_g, tm):
    M = o_f.shape[0]
    blk = pl.BlockSpec((tm, BRANCH_W), lambda m: (m, 0))
    return pl.pallas_call(
        _dn_out_kernel,
        out_shape=jax.ShapeDtypeStruct((M, BRANCH_W), BF16),
        grid=(M // tm,),
        in_specs=[blk, blk, pl.BlockSpec((tm, BRANCH_W), lambda m: (m, COL_DZ // BRANCH_W)),
                  pl.BlockSpec((1, DN_DV), lambda m: (0, 0))],
        out_specs=blk,
        compiler_params=_cparams(("parallel",), 32),
        name="dn_out",
    )(o_f, o_b, proj, norm_g)


def _deltanet(proj, ab, projc, abc, conv_w, a_log, dt_bias, norm_g, want_ctx, tm):
    cw = jnp.pad(conv_w, ((0, 8 - DN_CONV), (0, 0)))
    alog = jnp.pad(a_log.reshape(1, DN_STATES), ((0, 0), (0, LANES - DN_STATES)))
    dtb = jnp.pad(dt_bias.reshape(1, DN_STATES), ((0, 0), (0, LANES - DN_STATES)))
    ql, kl, vl, auxl = _dn_prep(proj, ab, cw, alog, dtb, tm)
    qx, kx, vx, auxx = _dn_prep(projc, abc, cw, alog, dtb, projc.shape[0])
    consts = _dn_constants()
    s0 = jnp.zeros((2, DN_HEADS * DN_DK, DN_DV), F32)
    oc_f, oc_b, s1 = _dn_scan(qx, kx, vx, auxx, s0, consts)
    o_f, o_b, _ = _dn_scan(ql, kl, vl, auxl, s1, consts)
    y = _dn_out(o_f, o_b, proj, norm_g[None], tm)
    yc = _dn_out(oc_f, oc_b, projc, norm_g[None], projc.shape[0]) if want_ctx else None
    return y, yc


SWA_PREP_W = 8 * LANES


def _rope_tables(T):
    half = SWA_HD // 4
    inv = ROPE_BASE ** (-jnp.arange(half, dtype=F32) / half)
    t = jnp.arange(T, dtype=jnp.int32)
    ang_r = (t // GRID_W).astype(F32)[:, None] * inv
    ang_c = (t % GRID_W).astype(F32)[:, None] * inv
    cos = jnp.concatenate([jnp.cos(ang_r)] * 2 + [jnp.cos(ang_c)] * 2, axis=1)
    sin = jnp.concatenate([-jnp.sin(ang_r), jnp.sin(ang_r), -jnp.sin(ang_c), jnp.sin(ang_c)], axis=1)
    return jnp.tile(cos, (1, 2)), jnp.tile(sin, (1, 2))


def _swa_prep_kernel(q_ref, kv_ref, cos_ref, sin_ref, qo_ref, kvo_ref, *, rope):
    lane = lax.broadcasted_iota(jnp.int32, (q_ref.shape[0], LANES), 1)
    first = (lane % (SWA_HD // 2)) < (SWA_HD // 4)
    low = lane < SWA_HD

    def rot(x):
        if not rope:
            return x
        partner = jnp.where(first, pltpu.roll(x, LANES - SWA_HD // 4, 1), pltpu.roll(x, SWA_HD // 4, 1))
        return x * cos_ref[...] + partner * sin_ref[...]

    for c in range(q_ref.shape[1] // LANES):
        cols = slice(c * LANES, (c + 1) * LANES)
        qo_ref[:, cols] = (rot(q_ref[:, cols].astype(F32)) * SWA_HD ** -0.5).astype(qo_ref.dtype)
    kv = kv_ref[...].astype(F32)
    for j, x in enumerate((rot(kv[:, :LANES]), kv[:, LANES:])):
        sw = pltpu.roll(x, SWA_HD, 1)
        parts = (jnp.where(low, x, 0.0), jnp.where(low, 0.0, sw), jnp.where(low, sw, 0.0), jnp.where(low, 0.0, x))
        for i, part in enumerate(parts):
            kvo_ref[:, (4 * j + i) * LANES:(4 * j + i + 1) * LANES] = part.astype(kvo_ref.dtype)


def _swa_prep(proj, cos, sin, rope, tm):
    M = proj.shape[0]
    tab = pl.BlockSpec((tm, LANES), lambda m: (m, 0))
    return pl.pallas_call(
        functools.partial(_swa_prep_kernel, rope=rope),
        out_shape=(jax.ShapeDtypeStruct((M, BRANCH_W), BF16), jax.ShapeDtypeStruct((M, SWA_PREP_W), BF16)),
        grid=(M // tm,),
        in_specs=[pl.BlockSpec((tm, BRANCH_W), lambda m: (m, COL_SQ // BRANCH_W)),
                  pl.BlockSpec((tm, 2 * LANES), lambda m: (m, COL_SK // (2 * LANES))), tab, tab],
        out_specs=(pl.BlockSpec((tm, BRANCH_W), lambda m: (m, 0)), pl.BlockSpec((tm, SWA_PREP_W), lambda m: (m, 0))),
        compiler_params=_cparams(("parallel",), 32),
        name="swa_prep",
    )(proj, proj, cos, sin)


def _swa_masks(n_ctx):
    nk = 3 * SWA_BLOCK + n_ctx
    i = np.arange(SWA_BLOCK)[:, None]
    j = np.arange(nk)[None, :]
    lat = j < 3 * SWA_BLOCK
    band = ~lat | (np.abs(j - SWA_BLOCK - i) <= SWA_WINDOW)
    no_prev = ~(j < SWA_BLOCK) | (i < 0)
    no_next = ~((j >= 2 * SWA_BLOCK) & lat) | (i < 0)
    ctx_only = ~lat | (i < 0)
    m = np.stack([np.where(ok, 0.0, -np.inf) for ok in (band, no_prev, no_next, ctx_only)]).astype(np.float32)
    return jnp.asarray(np.concatenate([m, m], axis=2))


def _swa_attn_kernel(sink_ref, q_ref, kvp_ref, kvo_ref, kvn_ref, kvc_ref, m_ref, y_ref, *, latent_keys):
    n = pl.program_id(0)
    kv = jnp.concatenate([kvp_ref[...], kvo_ref[...], kvn_ref[...], kvc_ref[...]], axis=0)
    nk = kv.shape[0]
    if latent_keys:
        mask = (m_ref[0] + jnp.where(n == 0, m_ref[1], 0.0) + jnp.where(n == pl.num_programs(0) - 1, m_ref[2], 0.0))
    else:
        mask = m_ref[3]
    lane = lax.broadcasted_iota(jnp.int32, (q_ref.shape[0], LANES), 1)
    for c in range(SWA_Q_HEADS // 2):
        kh = (2 * c) // SWA_GROUP
        cols = slice(c * LANES, (c + 1) * LANES)
        kk = jnp.concatenate([kv[:, (2 * kh) * LANES:(2 * kh + 1) * LANES],
                              kv[:, (2 * kh + 1) * LANES:(2 * kh + 2) * LANES]], axis=0)
        vv = jnp.concatenate([kv[:, (4 + 2 * kh) * LANES:(5 + 2 * kh) * LANES],
                              kv[:, (5 + 2 * kh) * LANES:(6 + 2 * kh) * LANES]], axis=0)
        s = lax.dot_general(q_ref[:, cols], kk, (((1,), (1,)), ((), ())), preferred_element_type=F32) + mask
        ps, dens = [], []
        for half in range(2):
            sh = s[:, half * nk:(half + 1) * nk]
            sink = sink_ref[2 * c + half]
            mx = jnp.maximum(jnp.max(sh, axis=1, keepdims=True), sink)
            p = jnp.exp(sh - mx)
            dens.append(jnp.sum(p, axis=1, keepdims=True) + jnp.exp(sink - mx))
            ps.append(p.astype(BF16))
        o = jnp.dot(jnp.concatenate(ps, axis=1), vv, preferred_element_type=F32)
        y_ref[:, cols] = (o / jnp.where(lane < SWA_HD, dens[0], dens[1])).astype(y_ref.dtype)


def _swa_attn(q, kvr, kvc, sink, masks, latent_keys):
    M = q.shape[0]
    NB = M // SWA_BLOCK
    blk = lambda im: pl.BlockSpec((SWA_BLOCK, SWA_PREP_W), im)
    return pl.pallas_call(
        functools.partial(_swa_attn_kernel, latent_keys=latent_keys),
        out_shape=jax.ShapeDtypeStruct((M, BRANCH_W), BF16),
        grid=(NB,),
        in_specs=[pl.BlockSpec(memory_space=pltpu.SMEM),
                  pl.BlockSpec((SWA_BLOCK, BRANCH_W), lambda n: (n, 0)),
                  blk(lambda n: (jnp.maximum(n - 1, 0), 0)), blk(lambda n: (n, 0)),
                  blk(lambda n: (jnp.minimum(n + 1, NB - 1), 0)),
                  pl.BlockSpec(kvc.shape, lambda n: (0, 0)),
                  pl.BlockSpec(masks.shape, lambda n: (0, 0, 0))],
        out_specs=pl.BlockSpec((SWA_BLOCK, BRANCH_W), lambda n: (n, 0)),
        compiler_params=_cparams(("parallel",), 32),
        name="swa_attn",
    )(sink, q, kvr, kvr, kvr, kvc, masks)


def _swa(proj, projc, sink, want_ctx, tm):
    T, Lc = proj.shape[0], projc.shape[0]
    cos, sin = _rope_tables(T)
    q, kvr = _swa_prep(proj, cos, sin, True, tm)
    none = jnp.zeros((Lc, LANES), F32)
    qc, kvc = _swa_prep(projc, none, none, False, Lc)
    masks = _swa_masks(Lc)
    y = _swa_attn(q, kvr, kvc, sink, masks, True)
    yc = _swa_attn(qc, kvc, kvc, sink, masks, False) if want_ctx else None
    return y, yc


def kernel(x, c, ctx, c_ctx, w_ada, b_ada, norm1_g, norm2_g, w_in, pool_w, pool_scale, dn_conv_w, dn_a_log, dn_dt_bias, dn_norm_g, swa_sink, sgu_norm_g, sgu_w, sgu_b, w_gate, w_branch, w_out, w_router, b_router, w_e_gate, w_e_up, w_e_down, final_g):
    B, T, D = x.shape
    assert B == 1 and D == D_MODEL
    Lc = ctx.shape[1]
    depth = w_ada.shape[0]

    cond8 = jnp.zeros((8, D), F32).at[0].set(c[0]).at[1].set(c_ctx)
    mods = _ada(cond8, w_ada, b_ada)

    xl = x[0]
    xc = ctx[0]
    tm_l = 512
    tm_c = Lc
    for i in range(depth):
        want_ctx = i < depth - 1
        ml = [mods[i, 0:1, j * D:(j + 1) * D] for j in range(6)]
        mc = [mods[i, 1:2, j * D:(j + 1) * D] for j in range(6)]
        n1 = norm1_g[i][None]
        n2 = norm2_g[i][None]
        w_main, w_ab = _w_in_prep(w_in[i])
        wgt = w_gate[i].astype(BF16)
        wbr = w_branch[i].astype(BF16)
        wo = w_out[i].astype(BF16)
        wr_t = w_router[i].T.astype(BF16)
        br = b_router[i][:, None]
        weg, weu, wed = w_e_gate[i], w_e_up[i], w_e_down[i]

        proj, ab, h = _inproj(xl, n1, ml[0], ml[1], w_main, w_ab, tm_l)
        projc, abc, hc = _inproj(xc, n1, mc[0], mc[1], w_main, w_ab, tm_c)
        y_pool = _pool(proj, pool_w[i], pool_scale[i][None], tm_l)
        y_dn, yc_dn = _deltanet(proj, ab, projc, abc, dn_conv_w[i], dn_a_log[i], dn_dt_bias[i], dn_norm_g[i],
                                want_ctx, tm_l)
        y_swa, yc_swa = _swa(proj, projc, swa_sink[i], want_ctx, tm_l)
        y_sgu = _sgu(proj, sgu_norm_g[i][None], sgu_w[i], sgu_b[i], tm_l)
        ys = (y_pool, y_dn, y_swa, y_sgu)

        acc = _merge(h, ys, wgt, wbr, tm_l)
        xl, h2, aff_t = _outproj(acc, xl, wo, ml[2], n2, ml[3], ml[4], wr_t, br, tm_l)
        if want_ctx:
            yc_pool = _pool(projc, pool_w[i], pool_scale[i][None], tm_c)
            yc_sgu = _sgu(projc, sgu_norm_g[i][None], sgu_w[i], sgu_b[i], tm_c)
            accc = _merge(hc, (yc_pool, yc_dn, yc_swa, yc_sgu), wgt, wbr, tm_c)
            xc, hc2, affc_t = _outproj(accc, xc, wo, mc[2], n2, mc[3], mc[4], wr_t, br, tm_c)
            xl, xc = _moe(xl, h2, aff_t, ml[5], xc, hc2, affc_t, mc[5], weg, weu, wed, final_g[None], False)
        else:
            xl, _ = _moe(xl, h2, aff_t, ml[5], None, None, None, None, weg, weu, wed, final_g[None],
                         i == depth - 1)

    return xl[None]
```

```python
import functools
import math

import numpy as np
import jax
import jax.numpy as jnp
from jax import lax
from jax.experimental import pallas as pl
from jax.experimental.pallas import tpu as pltpu

F32 = jnp.float32
BF16 = jnp.bfloat16

D_MODEL = 2048
GRID_W = 64
N_BRANCH = 4
BRANCH_W = D_MODEL // N_BRANCH
POOL_WINDOWS = (2, 4, 8, 16)
POOL_GROUP_W = BRANCH_W // len(POOL_WINDOWS)
DN_DK = 128
DN_DV = 128
DN_HEADS = BRANCH_W // DN_DV
DN_CHUNK = 64
SWA_HD = 64
SWA_Q_HEADS = BRANCH_W // SWA_HD
SWA_KV_HEADS = SWA_Q_HEADS // 4
SWA_GROUP = SWA_Q_HEADS // SWA_KV_HEADS
SWA_WINDOW = 128
SWA_BLOCK = 128
ROPE_BASE = 10000.0
SGU_CHUNK = 128
SGU_GROUPS = 4
SGU_GROUP_W = BRANCH_W // SGU_GROUPS
N_EXPERTS = 16
EC_FACTOR = 2
RMS_EPS = 1e-6

IN_SPLITS = (BRANCH_W,
             DN_HEADS * DN_DK, DN_HEADS * DN_DK,
             DN_HEADS * DN_DV, DN_HEADS * DN_DV,
             2 * DN_HEADS, 2 * DN_HEADS,
             SWA_Q_HEADS * SWA_HD, SWA_KV_HEADS * SWA_HD, SWA_KV_HEADS * SWA_HD,
             BRANCH_W, BRANCH_W)
IN_OFFSETS = tuple(int(o) for o in np.cumsum(IN_SPLITS)[:-1])
AB_LO = IN_OFFSETS[4]
AB_W = 4 * DN_HEADS
MAIN_W = sum(IN_SPLITS) - AB_W
COL_DZ, COL_P, COL_SQ, COL_GU, COL_GV, COL_SK, COL_SV = 1536, 2048, 2560, 3072, 3584, 4096, 4224
LANES = 128
MOE_TILE = 256
MOE_ALIGN = 16
MOE_WIN = 64
MIB = 2 ** 20


def _cparams(sem, vmem_mib):
    return pltpu.CompilerParams(dimension_semantics=sem, vmem_limit_bytes=vmem_mib * MIB)


def _ada_kernel(c_ref, w_ref, b_ref, o_ref):
    c = c_ref[...]
    s = (c * jax.nn.sigmoid(c)).astype(BF16)
    o_ref[0] = jnp.dot(s, w_ref[0].astype(BF16), preferred_element_type=F32) + b_ref[0]


def _ada(cond8, w_ada, b_ada):
    L, D, N = w_ada.shape
    tn = 1024
    return pl.pallas_call(
        _ada_kernel,
        out_shape=jax.ShapeDtypeStruct((L, 8, N), F32),
        grid=(L, N // tn),
        in_specs=[pl.BlockSpec((8, D), lambda l, n: (0, 0)),
                  pl.BlockSpec((1, D, tn), lambda l, n: (l, 0, n)),
                  pl.BlockSpec((1, 1, tn), lambda l, n: (l, 0, n))],
        out_specs=pl.BlockSpec((1, 8, tn), lambda l, n: (l, 0, n)),
        compiler_params=_cparams(("parallel", "parallel"), 32),
        name="ada_mod",
    )(cond8, w_ada, b_ada.reshape(L, 1, N))


def _w_in_prep_kernel(w_ref, main_ref, ab_ref):
    o = (0,) + IN_OFFSETS + (sum(IN_SPLITS),)
    pos = 0
    for lo, hi in ((o[1], o[5]), (o[0], o[1]), (o[7], o[8]), (o[10], o[12]), (o[8], o[10])):
        main_ref[:, pos:pos + hi - lo] = w_ref[:, lo:hi].astype(main_ref.dtype)
        pos += hi - lo
    ab_ref[...] = jnp.zeros(ab_ref.shape, ab_ref.dtype)
    ab_ref[:, 0:AB_W] = w_ref[:, AB_LO:AB_LO + AB_W].astype(ab_ref.dtype)


def _w_in_prep(w_in, layer):
    _, D, N = w_in.shape
    tk = 256
    return pl.pallas_call(
        _w_in_prep_kernel,
        out_shape=(jax.ShapeDtypeStruct((D, MAIN_W), BF16), jax.ShapeDtypeStruct((D, LANES), BF16)),
        grid=(D // tk,),
        in_specs=[pl.BlockSpec((None, tk, N), lambda r: (layer, r, 0))],
        out_specs=(pl.BlockSpec((tk, MAIN_W), lambda r: (r, 0)), pl.BlockSpec((tk, LANES), lambda r: (r, 0))),
        compiler_params=_cparams(("parallel",), 32),
        name="w_in_prep",
    )(w_in)


def _inproj_kernel(x_ref, g_ref, sh_ref, sc_ref, w_ref, wab_ref, proj_ref, ab_ref, h_ref, hs_ref):
    @pl.when(pl.program_id(1) == 0)
    def _():
        x = x_ref[...]
        y = x * lax.rsqrt(jnp.mean(x * x, axis=-1, keepdims=True) + RMS_EPS) * g_ref[...]
        hb = (y * (1.0 + sc_ref[...]) + sh_ref[...]).astype(BF16)
        hs_ref[...] = hb
        h_ref[...] = hb
        ab_ref[...] = jnp.dot(hb, wab_ref[...], preferred_element_type=F32)

    proj_ref[...] = jnp.dot(hs_ref[...], w_ref[...], preferred_element_type=F32).astype(proj_ref.dtype)


def _inproj(x, g, sh, sc, w_main, w_ab, tm):
    M, D = x.shape
    N = w_main.shape[1]
    tn = N // 2
    vec = pl.BlockSpec((1, D), lambda m, n: (0, 0))
    return pl.pallas_call(
        _inproj_kernel,
        out_shape=(jax.ShapeDtypeStruct((M, N), BF16),
                   jax.ShapeDtypeStruct((M, LANES), F32),
                   jax.ShapeDtypeStruct((M, D), BF16)),
        grid=(M // tm, N // tn),
        in_specs=[pl.BlockSpec((tm, D), lambda m, n: (m, 0)), vec, vec, vec,
                  pl.BlockSpec((D, tn), lambda m, n: (0, n)),
                  pl.BlockSpec((D, LANES), lambda m, n: (0, 0))],
        out_specs=(pl.BlockSpec((tm, tn), lambda m, n: (m, n)),
                   pl.BlockSpec((tm, LANES), lambda m, n: (m, 0)),
                   pl.BlockSpec((tm, D), lambda m, n: (m, 0))),
        scratch_shapes=[pltpu.VMEM((tm, D), BF16)],
        compiler_params=_cparams(("parallel", "arbitrary"), 48),
        name="inproj",
    )(x, g, sh, sc, w_main, w_ab)


def _merge_kernel(h_ref, y0_ref, y1_ref, y2_ref, y3_ref, wg_ref, wb_ref, o_ref):
    h = h_ref[...]
    acc = None
    for i, y_ref in enumerate((y0_ref, y1_ref, y2_ref, y3_ref)):
        gate = jnp.dot(h, wg_ref[i], preferred_element_type=F32)
        br = jnp.dot(y_ref[...], wb_ref[i], preferred_element_type=F32)
        t = jax.nn.sigmoid(gate) * br
        acc = t if acc is None else acc + t
    o_ref[...] = acc.astype(o_ref.dtype)


def _merge(h, ys, w_gate, w_branch, tm):
    M, D = h.shape
    tn = 512
    ybs = pl.BlockSpec((tm, BRANCH_W), lambda m, n: (m, 0))
    return pl.pallas_call(
        _merge_kernel,
        out_shape=jax.ShapeDtypeStruct((M, D), BF16),
        grid=(M // tm, D // tn),
        in_specs=[pl.BlockSpec((tm, D), lambda m, n: (m, 0)), ybs, ybs, ybs, ybs,
                  pl.BlockSpec((N_BRANCH, D, tn), lambda m, n: (0, 0, n)),
                  pl.BlockSpec((N_BRANCH, BRANCH_W, tn), lambda m, n: (0, 0, n))],
        out_specs=pl.BlockSpec((tm, tn), lambda m, n: (m, n)),
        compiler_params=_cparams(("parallel", "arbitrary"), 48),
        name="merge",
    )(h, *ys, w_gate, w_branch)


def _outproj_kernel(acc_ref, x_ref, w_ref, g1_ref, ng_ref, sh_ref, sc_ref, wr_ref, br_ref,
                    xo_ref, h2_ref, aff_ref):
    out = jnp.dot(acc_ref[...], w_ref[...], preferred_element_type=F32)
    xn = x_ref[...] + g1_ref[...] * out
    xo_ref[...] = xn
    y = xn * lax.rsqrt(jnp.mean(xn * xn, axis=-1, keepdims=True) + RMS_EPS) * ng_ref[...]
    h2 = (y * (1.0 + sc_ref[...]) + sh_ref[...]).astype(BF16)
    h2_ref[...] = h2
    logits = lax.dot_general(wr_ref[...], h2, (((1,), (1,)), ((), ())), preferred_element_type=F32)
    logits = logits + br_ref[...]
    e = jnp.exp(logits - jnp.max(logits, axis=0, keepdims=True))
    aff = e / jnp.sum(e, axis=0, keepdims=True)
    for j in range(aff_ref.shape[0]):
        aff_ref[j] = aff[:, j * MOE_TILE:(j + 1) * MOE_TILE]


def _outproj(acc, x, w_out, g1, ng, sh, sc, w_router_t, b_router, tm):
    M, D = x.shape
    E = w_router_t.shape[0]
    vec = pl.BlockSpec((1, D), lambda m: (0, 0))
    return pl.pallas_call(
        _outproj_kernel,
        out_shape=(jax.ShapeDtypeStruct((M, D), F32),
                   jax.ShapeDtypeStruct((M, D), BF16),
                   jax.ShapeDtypeStruct((M // MOE_TILE, E, MOE_TILE), F32)),
        grid=(M // tm,),
        in_specs=[pl.BlockSpec((tm, D), lambda m: (m, 0)),
                  pl.BlockSpec((tm, D), lambda m: (m, 0)),
                  pl.BlockSpec((D, D), lambda m: (0, 0)),
                  vec, vec, vec, vec,
                  pl.BlockSpec((E, D), lambda m: (0, 0)),
                  pl.BlockSpec((E, 1), lambda m: (0, 0))],
        out_specs=(pl.BlockSpec((tm, D), lambda m: (m, 0)),
                   pl.BlockSpec((tm, D), lambda m: (m, 0)),
                   pl.BlockSpec((tm // MOE_TILE, E, MOE_TILE), lambda m: (m, 0, 0))),
        compiler_params=_cparams(("parallel",), 48),
        name="outproj",
    )(acc, x, w_out, g1, ng, sh, sc, w_router_t, b_router)


def _route_kernel(aff_ref, u_ref, l_ref, rank_ref, base_ref, tot_ref, *, cap):
    aff = aff_ref[...]
    nt, E, W = aff.shape

    def count(mask):
        return jnp.sum(jnp.sum(mask.astype(jnp.int32), axis=0), axis=1, keepdims=True)

    def search(it, thr):
        cand = thr | lax.shift_left(jnp.int32(1), 30 - it)
        c = count(aff >= lax.bitcast_convert_type(cand, F32)[None])
        return jnp.where(c >= cap, cand, thr)

    thr = lax.bitcast_convert_type(lax.fori_loop(0, 31, search, jnp.zeros((E, 1), jnp.int32)), F32)[None]
    gt = aff > thr
    eq = aff == thr
    need = (cap - count(gt)).astype(F32)[None]

    def prefix(mask):
        m2 = jnp.where(mask, 1.0, 0.0).astype(BF16).reshape(nt * E, W)
        incl = jnp.dot(m2, u_ref[...], preferred_element_type=F32)
        tot = jnp.broadcast_to(incl[:, W - 1:W], (nt * E, LANES))
        base = jnp.dot(l_ref[...], tot.astype(BF16), preferred_element_type=F32)
        return (incl - m2.astype(F32) + base[:, 0:1]).reshape(nt, E, W), base, tot

    eq_rank, _, _ = prefix(eq)
    sel = gt | (eq & (eq_rank < need))
    rank, base, tot = prefix(sel)
    rank_ref[...] = jnp.where(sel, rank.astype(jnp.int32), -1)
    base_ref[...] = base.astype(jnp.int32)
    tot_ref[...] = tot.astype(jnp.int32)


def _route(aff_tm, cap):
    nt, E, W = aff_tm.shape
    upper = np.triu(np.ones((W, W), np.float32))
    idx = np.arange(nt * E)
    lower = ((idx[:, None] % E) == (idx[None, :] % E)) & ((idx[None, :] // E) < (idx[:, None] // E))
    full = lambda a: pl.BlockSpec(a.shape, lambda: (0,) * a.ndim)
    consts = (jnp.asarray(upper, BF16), jnp.asarray(lower, BF16))
    return pl.pallas_call(
        functools.partial(_route_kernel, cap=cap),
        out_shape=(jax.ShapeDtypeStruct((nt, E, W), jnp.int32), jax.ShapeDtypeStruct((nt * E, LANES), jnp.int32),
                   jax.ShapeDtypeStruct((nt * E, LANES), jnp.int32)),
        in_specs=[full(aff_tm), full(consts[0]), full(consts[1])],
        out_specs=(pl.BlockSpec((nt, E, W), lambda: (0, 0, 0)), pl.BlockSpec((nt * E, LANES), lambda: (0, 0)),
                   pl.BlockSpec((nt * E, LANES), lambda: (0, 0))),
        compiler_params=pltpu.CompilerParams(vmem_limit_bytes=40 * MIB),
        name="moe_route",
    )(aff_tm, *consts)


def _slot_onehots(rank, cur, lo_ref, cnt_ref, b, R, clamp_hi):
    E = rank.shape[0]
    riota = lax.broadcasted_iota(jnp.int32, (R, rank.shape[1]), 0)
    ohs, starts, news = [], [], []
    for e in range(E):
        c = cur[e]
        end = lo_ref[b * E + e] + cnt_ref[b * E + e]
        w = (c // MOE_ALIGN) * MOE_ALIGN
        if clamp_hi is not None:
            w = jnp.minimum(w, clamp_hi)
        rk = rank[e:e + 1, :]
        ohs.append(((rk - w) == riota) & (rk >= c))
        starts.append(w)
        news.append(jnp.minimum(end, w + R))
    return ohs, starts, news


def _gather_kernel(lo_ref, cnt_ref, h_ref, rank_ref, aff_ref, xe_ref, xg_ref,
                   stage, gstage, carry, gcarry, cur, npass, sem, *, R, cap):
    b = pl.program_id(0)
    E = rank_ref.shape[1]
    A = MOE_ALIGN

    @pl.when(b == 0)
    def _():
        carry[...] = jnp.zeros_like(carry)
        gcarry[...] = jnp.zeros_like(gcarry)
        npass[0] = 0
        stage[1, 0:R, :] = jnp.zeros((R, stage.shape[2]), stage.dtype)
        gstage[1, 0:R, :] = jnp.zeros((R, LANES), F32)
        pads = []
        for e in range(E):
            pads.append(pltpu.make_async_copy(stage.at[1, pl.ds(0, R)], xe_ref.at[e, pl.ds(cap, R)], sem.at[1]))
            pads.append(pltpu.make_async_copy(gstage.at[1, pl.ds(0, R)], xg_ref.at[e, pl.ds(cap, R)], sem.at[1]))
        for cp in pads:
            cp.start()
        for cp in pads:
            cp.wait()

    for e in range(E):
        cur[e] = lo_ref[b * E + e]
    rank = rank_ref[0]
    aff = aff_ref[0]
    h = h_ref[...]
    srow = lax.broadcasted_iota(jnp.int32, (A, 1), 0)

    def copies(slot, starts):
        cps = []
        for e in range(E):
            w = pl.multiple_of(starts[e], A)
            cps.append(pltpu.make_async_copy(stage.at[slot, pl.ds(e * R, R)], xe_ref.at[e, pl.ds(w, R)], sem.at[slot]))
            cps.append(pltpu.make_async_copy(gstage.at[slot, pl.ds(e * R, R)], xg_ref.at[e, pl.ds(w, R)], sem.at[slot]))
        return cps

    def one_pass(more):
        g = npass[0]
        slot = g % 2
        ohs, starts, news = _slot_onehots(rank, cur, lo_ref, cnt_ref, b, R, None)
        rows = jnp.dot(jnp.concatenate([jnp.where(oh, 1.0, 0.0).astype(BF16) for oh in ohs], axis=0), h,
                       preferred_element_type=F32)
        more = jnp.int32(0)
        for e in range(E):
            gate = jnp.sum(jnp.where(ohs[e], aff[e:e + 1, :], 0.0), axis=1, keepdims=True)
            gate = jnp.broadcast_to(gate, (R, LANES))
            stage[slot, e * R:e * R + A, :] = (rows[e * R:e * R + A] + carry[e]).astype(stage.dtype)
            stage[slot, e * R + A:(e + 1) * R, :] = rows[e * R + A:(e + 1) * R].astype(stage.dtype)
            gstage[slot, e * R:e * R + A, :] = gate[:A] + gcarry[e]
            gstage[slot, e * R + A:(e + 1) * R, :] = gate[A:]
            new = news[e]
            grp = jnp.minimum((new - starts[e]) // A, R // A - 1)
            keep = srow < (new - (new // A) * A)
            off = pl.multiple_of(e * R + grp * A, A)
            carry[e] = jnp.where(keep, stage[slot, pl.ds(off, A), :].astype(F32), 0.0)
            gcarry[e] = jnp.where(keep, gstage[slot, pl.ds(off, A), :], 0.0)
            cur[e] = new
            more = jnp.maximum(more, (new < lo_ref[b * E + e] + cnt_ref[b * E + e]).astype(jnp.int32))

        @pl.when(g > 0)
        def _():
            for cp in copies(1 - slot, [0] * E):
                cp.wait()

        for cp in copies(slot, starts):
            cp.start()
        npass[0] = g + 1
        return more

    lax.while_loop(lambda more: more > 0, one_pass, jnp.int32(1))

    @pl.when(b == pl.num_programs(0) - 1)
    def _():
        for cp in copies((npass[0] - 1) % 2, [0] * E):
            cp.wait()


def _gather(h2, rank, aff_tm, lo, cnt, cap, R):
    n, D = h2.shape
    nt, E, W = rank.shape
    grid_spec = pltpu.PrefetchScalarGridSpec(
        num_scalar_prefetch=2,
        grid=(nt,),
        in_specs=[pl.BlockSpec((W, D), lambda b, lo, cnt: (b, 0)),
                  pl.BlockSpec((1, E, W), lambda b, lo, cnt: (b, 0, 0)),
                  pl.BlockSpec((1, E, W), lambda b, lo, cnt: (b, 0, 0))],
        out_specs=(pl.BlockSpec(memory_space=pl.ANY), pl.BlockSpec(memory_space=pl.ANY)),
        scratch_shapes=[pltpu.VMEM((2, E * R, D), BF16), pltpu.VMEM((2, E * R, LANES), F32),
                        pltpu.VMEM((E, MOE_ALIGN, D), F32), pltpu.VMEM((E, MOE_ALIGN, LANES), F32),
                        pltpu.SMEM((E,), jnp.int32), pltpu.SMEM((1,), jnp.int32),
                        pltpu.SemaphoreType.DMA((2,))])
    return pl.pallas_call(
        functools.partial(_gather_kernel, R=R, cap=cap),
        out_shape=(jax.ShapeDtypeStruct((E, cap + R, D), BF16), jax.ShapeDtypeStruct((E, cap + R, LANES), F32)),
        grid_spec=grid_spec,
        compiler_params=_cparams(("arbitrary",), 48),
        name="moe_gather",
    )(lo, cnt, h2, rank, aff_tm)


FFN_STEPS = 4


def _ffn_kernel(*refs, has_ctx):
    if has_ctx:
        x_ref, g_ref, xc_ref, gc_ref, wg_ref, wu_ref, wd_ref, o_ref, oc_ref, wgs, wus, wds = refs
    else:
        x_ref, g_ref, wg_ref, wu_ref, wd_ref, o_ref, wgs, wus, wds = refs
    e = pl.program_id(0)
    s = pl.program_id(1)
    n_exp = pl.num_programs(0) - 1
    fq = wg_ref.shape[2]

    @pl.when(e < n_exp)
    def _():
        slot = e % 2
        for j in range(FFN_STEPS):

            @pl.when(s == j)
            def _():
                wgs[slot, :, j * fq:(j + 1) * fq] = wg_ref[0].astype(BF16)
                wus[slot, :, j * fq:(j + 1) * fq] = wu_ref[0].astype(BF16)
                wds[slot, j * fq:(j + 1) * fq, :] = wd_ref[0].astype(BF16)

    def expert(x, gate, slot):
        a = jnp.dot(x, wgs[slot], preferred_element_type=F32)
        b = jnp.dot(x, wus[slot], preferred_element_type=F32)
        hid = (a * jax.nn.sigmoid(a) * b).astype(BF16)
        return jnp.dot(hid, wds[slot], preferred_element_type=F32) * gate[:, 0:1]

    @pl.when(e >= 1)
    def _():
        slot = (e - 1) % 2
        o_ref[0] = expert(x_ref[0], g_ref[0], slot).astype(o_ref.dtype)
        if has_ctx:

            @pl.when(s == 0)
            def _():
                oc_ref[0] = expert(xc_ref[0], gc_ref[0], slot).astype(oc_ref.dtype)


def _ffn(xe, xg, wg, wu, wd, layer, cap, ctx=None):
    E, _, D = xe.shape
    F = wg.shape[3]
    S = FFN_STEPS
    M = S if cap % (S * LANES) == 0 else 1
    tm = cap // M
    fq = F // S
    rows = lambda e, s: (jnp.maximum(e - 1, 0), jnp.where(e == 0, 0, s % M), 0)
    crow = lambda e, s: (jnp.maximum(e - 1, 0), 0, 0)
    in_specs = [pl.BlockSpec((1, tm, D), rows), pl.BlockSpec((1, tm, LANES), rows)]
    out_shape = [jax.ShapeDtypeStruct((E, cap, D), BF16)]
    out_specs = [pl.BlockSpec((1, tm, D), rows)]
    args = [xe, xg]
    if ctx is not None:
        xec, xgc, capc = ctx
        in_specs += [pl.BlockSpec((1, capc, D), crow), pl.BlockSpec((1, capc, LANES), crow)]
        out_shape.append(jax.ShapeDtypeStruct((E, capc, D), BF16))
        out_specs.append(pl.BlockSpec((1, capc, D), crow))
        args += [xec, xgc]
    in_specs += [pl.BlockSpec((None, 1, D, fq), lambda e, s: (layer, jnp.minimum(e, E - 1), 0, s)),
                 pl.BlockSpec((None, 1, D, fq), lambda e, s: (layer, jnp.minimum(e, E - 1), 0, s)),
                 pl.BlockSpec((None, 1, fq, D), lambda e, s: (layer, jnp.minimum(e, E - 1), s, 0))]
    return pl.pallas_call(
        functools.partial(_ffn_kernel, has_ctx=ctx is not None),
        out_shape=tuple(out_shape),
        grid=(E + 1, S),
        in_specs=in_specs,
        out_specs=tuple(out_specs),
        scratch_shapes=[pltpu.VMEM((2, D, F), BF16), pltpu.VMEM((2, D, F), BF16), pltpu.VMEM((2, F, D), BF16)],
        compiler_params=_cparams(("arbitrary", "arbitrary"), 52),
        name="expert_ffn",
    )(*args, wg, wu, wd)


def _combine_kernel(lo_ref, cnt_ref, rank_ref, x_ref, g2_ref, ng_ref, y_ref, o_ref, ystage, yovf, acc, cur, sem, osem,
                    *, R, cap, final_norm):
    b = pl.program_id(0)
    nb = pl.num_programs(0)
    E = rank_ref.shape[1]
    A = MOE_ALIGN

    def window(c):
        return pl.multiple_of(jnp.minimum((c // A) * A, cap - R), A)

    def main_copies(tile, slot):
        return [pltpu.make_async_copy(y_ref.at[e, pl.ds(window(lo_ref[tile * E + e]), R)],
                                      ystage.at[slot, pl.ds(e * R, R)], sem.at[slot]) for e in range(E)]

    @pl.when(b == 0)
    def _():
        for cp in main_copies(0, 0):
            cp.start()

    @pl.when(b + 1 < nb)
    def _():
        for cp in main_copies(b + 1, (b + 1) % 2):
            cp.start()

    for e in range(E):
        cur[e] = lo_ref[b * E + e]
    rank = rank_ref[0]

    def expand(y_rows):
        ohs, _, news = _slot_onehots(rank, cur, lo_ref, cnt_ref, b, R, cap - R)
        more = jnp.int32(0)
        for e in range(E):
            cur[e] = news[e]
            more = jnp.maximum(more, (news[e] < lo_ref[b * E + e] + cnt_ref[b * E + e]).astype(jnp.int32))
        oh = jnp.concatenate([jnp.where(o, 1.0, 0.0).astype(BF16) for o in ohs], axis=0)
        return lax.dot_general(oh, y_rows, (((0,), (0,)), ((), ())), preferred_element_type=F32), more

    for cp in main_copies(b, b % 2):
        cp.wait()
    first, more = expand(ystage[b % 2])
    acc[...] = first

    def extra_pass(more):
        cps = [pltpu.make_async_copy(y_ref.at[e, pl.ds(window(cur[e]), R)], yovf.at[pl.ds(e * R, R)], osem.at[0])
               for e in range(E)]
        for cp in cps:
            cp.start()
        for cp in cps:
            cp.wait()
        part, more = expand(yovf[...])
        acc[...] += part
        return more

    lax.while_loop(lambda more: more > 0, extra_pass, more)
    out = x_ref[...] + g2_ref[...] * acc[...]
    if final_norm:
        out = out * lax.rsqrt(jnp.mean(out * out, axis=-1, keepdims=True) + RMS_EPS) * ng_ref[...]
    o_ref[...] = out


def _combine(y, rank, lo, cnt, x, g2, norm_g, final_norm, cap, R):
    n, D = x.shape
    nt, E, W = rank.shape
    grid_spec = pltpu.PrefetchScalarGridSpec(
        num_scalar_prefetch=2,
        grid=(nt,),
        in_specs=[pl.BlockSpec((1, E, W), lambda b, lo, cnt: (b, 0, 0)),
                  pl.BlockSpec((W, D), lambda b, lo, cnt: (b, 0)),
                  pl.BlockSpec((1, D), lambda b, lo, cnt: (0, 0)),
                  pl.BlockSpec((1, D), lambda b, lo, cnt: (0, 0)),
                  pl.BlockSpec(memory_space=pl.ANY)],
        out_specs=pl.BlockSpec((W, D), lambda b, lo, cnt: (b, 0)),
        scratch_shapes=[pltpu.VMEM((2, E * R, D), BF16), pltpu.VMEM((E * R, D), BF16), pltpu.VMEM((W, D), F32),
                        pltpu.SMEM((E,), jnp.int32), pltpu.SemaphoreType.DMA((2,)), pltpu.SemaphoreType.DMA((1,))])
    return pl.pallas_call(
        functools.partial(_combine_kernel, R=R, cap=cap, final_norm=final_norm),
        out_shape=jax.ShapeDtypeStruct((n, D), F32),
        grid_spec=grid_spec,
        compiler_params=_cparams(("arbitrary",), 48),
        name="moe_combine",
    )(lo, cnt, rank, x, g2, norm_g, y)


def _moe_dispatch(h2, aff_tm):
    n = h2.shape[0]
    cap = EC_FACTOR * n // aff_tm.shape[1]
    R = min(MOE_WIN, cap)
    rank, base, tot = _route(aff_tm, cap)
    lo, cnt = base[:, 0], tot[:, 0]
    xe, xg = _gather(h2, rank, aff_tm, lo, cnt, cap, R)
    return xe, xg, (rank, lo, cnt, cap, R)


def _moe(xl, h2, aff_tm, g2, xc, hc2, affc_tm, gc2, wg, wu, wd, layer, norm_g, final_norm):
    xe, xg, (rank, lo, cnt, cap, R) = _moe_dispatch(h2, aff_tm)
    if xc is None:
        (y,) = _ffn(xe, xg, wg, wu, wd, layer, cap)
        return _combine(y, rank, lo, cnt, xl, g2, norm_g, final_norm, cap, R), None
    xec, xgc, (rankc, loc, cntc, capc, Rc) = _moe_dispatch(hc2, affc_tm)
    y, yc = _ffn(xe, xg, wg, wu, wd, layer, cap, (xec, xgc, capc))
    return (_combine(y, rank, lo, cnt, xl, g2, norm_g, final_norm, cap, R),
            _combine(yc, rankc, loc, cntc, xc, gc2, norm_g, False, capc, Rc))


POOL_HALO = 16


def _pool_kernel(cur_ref, prev_ref, next_ref, w_ref, sc_ref, y_ref, scr, *, seq_len):
    m = pl.program_id(0)
    tm = cur_ref.shape[0]
    H = POOL_HALO
    scr[0:H, :] = jnp.where(m > 0, prev_ref[...].astype(F32), 0.0)
    scr[H:H + tm, :] = cur_ref[...].astype(F32)
    scr[H + tm:2 * H + tm, :] = jnp.where(m < pl.num_programs(0) - 1, next_ref[...].astype(F32), 0.0)
    t = m * tm + lax.broadcasted_iota(jnp.int32, (tm, 1), 0)
    for gi, w in enumerate(POOL_WINDOWS):
        cols = slice(gi * POOL_GROUP_W, (gi + 1) * POOL_GROUP_W)
        total = None
        for k in range(-(w // 2), w // 2):
            term = scr[H + k:H + k + tm, cols]
            total = term if total is None else total + term
        n = jnp.minimum(t + (w // 2 - 1), seq_len - 1) - jnp.maximum(t - w // 2, 0) + 1
        centred = total / n.astype(F32) - scr[H:H + tm, cols]
        y = jnp.dot(centred.astype(BF16), w_ref[gi].astype(BF16), preferred_element_type=F32)
        y_ref[:, cols] = (y * sc_ref[:, cols]).astype(y_ref.dtype)


def _pool(proj, pool_w, pool_scale, tm):
    M = proj.shape[0]
    hb = tm // POOL_HALO
    nhb = M // POOL_HALO
    cb = COL_P // BRANCH_W
    return pl.pallas_call(
        functools.partial(_pool_kernel, seq_len=M),
        out_shape=jax.ShapeDtypeStruct((M, BRANCH_W), BF16),
        grid=(M // tm,),
        in_specs=[pl.BlockSpec((tm, BRANCH_W), lambda m: (m, cb)),
                  pl.BlockSpec((POOL_HALO, BRANCH_W), lambda m: (jnp.maximum(m * hb - 1, 0), cb)),
                  pl.BlockSpec((POOL_HALO, BRANCH_W), lambda m: (jnp.minimum((m + 1) * hb, nhb - 1), cb)),
                  pl.BlockSpec(pool_w.shape, lambda m: (0, 0, 0)),
                  pl.BlockSpec((1, BRANCH_W), lambda m: (0, 0))],
        out_specs=pl.BlockSpec((tm, BRANCH_W), lambda m: (m, 0)),
        scratch_shapes=[pltpu.VMEM((tm + 2 * POOL_HALO, BRANCH_W), F32)],
        compiler_params=_cparams(("parallel",), 32),
        name="pool_mix",
    )(proj, proj, proj, pool_w, pool_scale)


def _gelu(x):
    return 0.5 * x * (1.0 + lax.erf(x * (2.0 ** -0.5)))


def _sgu_kernel(u_ref, v_ref, g_ref, w_ref, b_ref, y_ref):
    tm = u_ref.shape[0]
    v = _gelu(v_ref[...].astype(F32))
    v = (v * lax.rsqrt(jnp.mean(v * v, axis=-1, keepdims=True) + RMS_EPS) * g_ref[...]).astype(BF16)
    u = _gelu(u_ref[...].astype(F32))
    for c in range(tm // SGU_CHUNK):
        rows = slice(c * SGU_CHUNK, (c + 1) * SGU_CHUNK)
        for g in range(SGU_GROUPS):
            cols = slice(g * SGU_GROUP_W, (g + 1) * SGU_GROUP_W)
            mixed = jnp.dot(w_ref[g].astype(BF16), v[rows, cols], preferred_element_type=F32) + b_ref[:, g:g + 1]
            y_ref[rows, cols] = (u[rows, cols] * mixed).astype(y_ref.dtype)


def _sgu(proj, norm_g, w_s, b_s, tm):
    M = proj.shape[0]
    bt = jnp.pad(b_s.T, ((0, 0), (0, LANES - SGU_GROUPS)))
    return pl.pallas_call(
        _sgu_kernel,
        out_shape=jax.ShapeDtypeStruct((M, BRANCH_W), BF16),
        grid=(M // tm,),
        in_specs=[pl.BlockSpec((tm, BRANCH_W), lambda m: (m, COL_GU // BRANCH_W)),
                  pl.BlockSpec((tm, BRANCH_W), lambda m: (m, COL_GV // BRANCH_W)),
                  pl.BlockSpec((1, BRANCH_W), lambda m: (0, 0)),
                  pl.BlockSpec(w_s.shape, lambda m: (0, 0, 0)),
                  pl.BlockSpec((SGU_CHUNK, LANES), lambda m: (0, 0))],
        out_specs=pl.BlockSpec((tm, BRANCH_W), lambda m: (m, 0)),
        compiler_params=_cparams(("parallel",), 32),
        name="sgu_mix",
    )(proj, proj, norm_g, w_s, bt)


DN_CONV = 5
DN_HALO = 16
DN_QKV_W = 3 * BRANCH_W
DN_STATES = 2 * DN_HEADS


def _dn_prep_kernel(cur_ref, prev_ref, next_ref, ab_ref, cw_ref, alog_ref, dtb_ref,
                    q_ref, k_ref, v_ref, aux_ref, scr):
    m = pl.program_id(0)
    tm = cur_ref.shape[0]
    scr[0:DN_HALO, :] = jnp.where(m > 0, prev_ref[...].astype(F32), 0.0)
    scr[DN_HALO:DN_HALO + tm, :] = cur_ref[...].astype(F32)
    scr[DN_HALO + tm:2 * DN_HALO + tm, :] = jnp.where(m < pl.num_programs(0) - 1, next_ref[...].astype(F32), 0.0)
    acc = None
    for t in range(DN_CONV):
        lo = DN_HALO - DN_CONV // 2 + t
        term = scr[lo:lo + tm, :] * cw_ref[t:t + 1, :]
        acc = term if acc is None else acc + term
    y = acc * jax.nn.sigmoid(acc)
    for h in range(DN_HEADS):
        qh = y[:, h * DN_DK:(h + 1) * DN_DK]
        q_ref[:, h * DN_DK:(h + 1) * DN_DK] = (
            qh * lax.rsqrt(jnp.sum(qh * qh, axis=-1, keepdims=True) + 1e-6) * DN_DK ** -0.5)
        kh = y[:, BRANCH_W + h * DN_DK:BRANCH_W + (h + 1) * DN_DK]
        k_ref[:, h * DN_DK:(h + 1) * DN_DK] = kh * lax.rsqrt(jnp.sum(kh * kh, axis=-1, keepdims=True) + 1e-6)
    v_ref[...] = y[:, 2 * BRANCH_W:]

    ab = ab_ref[...]
    lane = lax.broadcasted_iota(jnp.int32, ab.shape, 1)
    sp = ab + dtb_ref[...]
    softplus = jnp.maximum(sp, 0.0) + jnp.log(1.0 + jnp.exp(-jnp.abs(sp)))
    g = jnp.where(lane < DN_STATES, -jnp.exp(alog_ref[...]) * softplus, 0.0)
    beta = jax.nn.sigmoid(ab)
    g1 = g.astype(BF16)
    r1 = g - g1.astype(F32)
    g2 = r1.astype(BF16)
    g3 = (r1 - g2.astype(F32)).astype(BF16)
    gs = jnp.concatenate([g1, g2, g3], axis=1)
    row = lax.broadcasted_iota(jnp.int32, (tm, tm), 0)
    col = lax.broadcasted_iota(jnp.int32, (tm, tm), 1)
    same = (row // DN_CHUNK) == (col // DN_CHUNK)
    tri_f = jnp.where(same & (col <= row), 1.0, 0.0).astype(BF16)
    tri_b = jnp.where(same & (col >= row), 1.0, 0.0).astype(BF16)
    cf = jnp.dot(tri_f, gs, preferred_element_type=F32)
    cb = jnp.dot(tri_b, gs, preferred_element_type=F32)
    cf = cf[:, :LANES] + cf[:, LANES:2 * LANES] + cf[:, 2 * LANES:]
    cb = cb[:, :LANES] + cb[:, LANES:2 * LANES] + cb[:, 2 * LANES:]
    gc = jnp.where(lane < DN_HEADS, cf, cb)
    aux_ref[...] = jnp.where(lane < DN_STATES, gc, beta)


def _dn_prep(proj, ab, cw, alog, dtb, tm):
    M = proj.shape[0]
    hb = tm // DN_HALO
    nhb = M // DN_HALO
    row = pl.BlockSpec((1, LANES), lambda m: (0, 0))
    return pl.pallas_call(
        _dn_prep_kernel,
        out_shape=(jax.ShapeDtypeStruct((M, BRANCH_W), F32),) * 3 + (jax.ShapeDtypeStruct((M, LANES), F32),),
        grid=(M // tm,),
        in_specs=[pl.BlockSpec((tm, DN_QKV_W), lambda m: (m, 0)),
                  pl.BlockSpec((DN_HALO, DN_QKV_W), lambda m: (jnp.maximum(m * hb - 1, 0), 0)),
                  pl.BlockSpec((DN_HALO, DN_QKV_W), lambda m: (jnp.minimum((m + 1) * hb, nhb - 1), 0)),
                  pl.BlockSpec((tm, LANES), lambda m: (m, 0)),
                  pl.BlockSpec((8, DN_QKV_W), lambda m: (0, 0)), row, row],
        out_specs=(pl.BlockSpec((tm, BRANCH_W), lambda m: (m, 0)),) * 3 + (pl.BlockSpec((tm, LANES), lambda m: (m, 0)),),
        scratch_shapes=[pltpu.VMEM((tm + 2 * DN_HALO, DN_QKV_W), F32)],
        compiler_params=_cparams(("parallel",), 40),
        name="dn_prep",
    )(proj, proj, proj, ab, cw, alog, dtb)


def _dn_constants():
    L, H = DN_CHUNK, DN_HEADS
    WW, FW = DN_STATES * L, DN_STATES * DN_DK
    r = np.arange(L)[:, None]
    lane = np.arange(WW)[None, :]
    c = lane % L
    lower = lane < H * L
    cm = np.zeros((9, L, WW), np.float32)
    cm[0] = r == c
    cm[1] = np.where(np.where(lower, r >= c, r <= c), 0.0, -np.inf)
    cm[2] = np.where(lower, r > c, r < c)
    hi, lo = np.where(lower, r, c), np.where(lower, c, r)
    for j in range(6):
        s = 2 ** j
        cm[3 + j] = ((r // (2 * s)) == (c // (2 * s))) & (hi % (2 * s) >= s) & (lo % (2 * s) < s)
    rw = np.arange(WW)[:, None]
    rf = np.arange(H * DN_DK)[:, None]
    m_ww = rw // L == np.arange(WW)[None] // L
    m_wf = rw // L == np.arange(FW)[None] // DN_DK
    m_w2f = rw // L == (np.arange(2 * FW)[None] % FW) // DN_DK
    m_ff = rf // DN_DK == np.arange(H * DN_DK)[None] // DN_DK
    return (jnp.asarray(cm),) + tuple(jnp.asarray(m, BF16) for m in (m_ww, m_wf, m_w2f, m_ff))


def _dn_scan_kernel(qf_ref, kf_ref, vf_ref, af_ref, qb_ref, kb_ref, vb_ref, ab_ref, s0_ref,
                    cm_ref, mww_ref, mwf_ref, mw2f_ref, mff_ref, of_ref, ob_ref, sfin_ref,
                    s_scr, u_st, w_st, qd_st, kd_st, qk_st, gl_st):
    j = pl.program_id(0)
    L, H = DN_CHUNK, DN_HEADS
    FD = H * DN_DK

    @pl.when(j == 0)
    def _():
        s_scr[...] = s0_ref[...]
        for st in (u_st, w_st, qd_st, kd_st, qk_st, gl_st):
            st[...] = jnp.zeros(st.shape, st.dtype)

    lane = lax.broadcasted_iota(jnp.int32, (L, LANES), 1)
    rowblk = lax.broadcasted_iota(jnp.int32, (FD, DN_DV), 0) // DN_DK
    m_ww, m_wf, m_w2f, m_ff = mww_ref[...], mwf_ref[...], mw2f_ref[...], mff_ref[...]

    def tile_rows(a):
        return jnp.concatenate([a] * DN_STATES, axis=0)

    def col(arr, c):
        return jnp.broadcast_to(arr[:, c:c + 1], (arr.shape[0], LANES))

    def feat(arrs, base):
        return jnp.concatenate([col(arrs[d], base + d * H + h) for d in range(2) for h in range(H)], axis=1)

    def wide(arrs, base):
        return jnp.concatenate([jnp.where(lane < L, col(arrs[d], base + d * H + 2 * p),
                                          col(arrs[d], base + d * H + 2 * p + 1))
                                for d in range(2) for p in range(H // 2)], axis=1)

    u, w, qd, kd, qk = u_st[...], w_st[...], qd_st[...], kd_st[...], qk_st[...]
    g_prev = gl_st[...]
    live = j > 0
    dir_cols = [slice(d * FD, (d + 1) * FD) for d in range(2)]
    s_old = [s_scr[d] for d in range(2)]
    val = {}

    def ws_qs(d):
        s_bd = jnp.concatenate([s_old[d].astype(BF16)] * H, axis=1) * m_ff
        val["ws", d] = jnp.dot(jnp.concatenate([w[:, dir_cols[d]], qd[:, dir_cols[d]]], axis=0).astype(BF16), s_bd,
                               preferred_element_type=F32)

    def outputs():
        val["v"] = (u - jnp.concatenate([val["ws", 0][:L], val["ws", 1][:L]], axis=1)).astype(BF16)
        val["o"] = (jnp.concatenate([val["ws", 0][L:], val["ws", 1][L:]], axis=1)
                    + jnp.dot(qk, tile_rows(val["v"]) * m_wf, preferred_element_type=F32))

    def state(d):
        cross = lax.dot_general(kd[:, dir_cols[d]], val["v"][:, dir_cols[d]], (((0,), (0,)), ((), ())),
                                preferred_element_type=F32)
        fold = None
        g_last = None
        for h in range(H):
            part = jnp.where(rowblk == h, cross[:, h * DN_DV:(h + 1) * DN_DV], 0.0)
            gl = jnp.where(rowblk == h, g_prev[:, d * H + h:d * H + h + 1], 0.0)
            fold = part if fold is None else fold + part
            g_last = gl if g_last is None else g_last + gl
        val["s", d] = jnp.where(live, g_last * s_old[d] + fold, s_old[d])

    queue = [lambda: ws_qs(0), lambda: ws_qs(1), outputs, lambda: state(0), lambda: state(1)]

    def issue_one():
        if queue:
            queue.pop(0)()

    aux = (af_ref[...], ab_ref[...])
    last = (aux[0][L - 1:L, :], aux[1][0:1, :])
    e_gc = tuple(jnp.exp(a) for a in aux)
    e_rem = tuple(jnp.exp(l - a) for l, a in zip(last, aux))
    beta_f = feat(aux, DN_STATES)
    eg_f = feat(e_gc, 0)
    er_f = feat(e_rem, 0)
    gcw = wide(aux, 0)
    eye = cm_ref[0]
    gc_row = jnp.sum(eye * gcw, axis=0, keepdims=True)
    decay = jnp.exp(gcw - gc_row + cm_ref[1])
    q = jnp.concatenate([qf_ref[...], qb_ref[...]], axis=1)
    k = jnp.concatenate([kf_ref[...], kb_ref[...]], axis=1)
    v = jnp.concatenate([vf_ref[...], vb_ref[...]], axis=1)
    kbeta = k * beta_f
    kq = lax.dot_general(jnp.concatenate([kbeta, q], axis=0).astype(BF16), tile_rows(k.astype(BF16)) * m_wf,
                         (((1,), (1,)), ((), ())), preferred_element_type=F32)
    issue_one()
    mm = kq[:L] * decay * cm_ref[2]
    x = eye - mm * cm_ref[3]
    for lvl in range(4, 9):
        xb = x.astype(BF16)
        t = jnp.dot(xb, tile_rows((mm * cm_ref[lvl]).astype(BF16)) * m_ww, preferred_element_type=F32)
        issue_one()
        x = x - jnp.dot(t.astype(BF16), tile_rows(xb) * m_ww, preferred_element_type=F32)
        issue_one()
    rhs = jnp.concatenate([v * beta_f, kbeta * eg_f], axis=1)
    sol = rhs + jnp.dot((x - eye).astype(BF16), tile_rows(rhs.astype(BF16)) * m_w2f, preferred_element_type=F32)
    while queue:
        issue_one()
    for d, o_ref in enumerate((of_ref, ob_ref)):
        s_scr[d] = val["s", d]
        o_ref[...] = val["o"][:, dir_cols[d]]
    u_st[...] = sol[:, :2 * FD]
    w_st[...] = sol[:, 2 * FD:]
    qd_st[...] = q * eg_f
    kd_st[...] = (k * er_f).astype(BF16)
    qk_st[...] = (kq[L:] * decay).astype(BF16)
    gl_st[...] = jnp.where(lane[0:1] < H, jnp.exp(last[0]), jnp.exp(last[1]))

    @pl.when(j == pl.num_programs(0) - 1)
    def _():
        sfin_ref[...] = s_scr[...]


def _dn_scan(q, k, v, aux, s0, consts):
    M = q.shape[0]
    N = M // DN_CHUNK
    FW = 2 * BRANCH_W
    fwd = lambda j: (jnp.minimum(j, N - 1), 0)
    bwd = lambda j: (jnp.maximum(N - 1 - j, 0), 0)
    fwd_o = lambda j: (jnp.maximum(j - 1, 0), 0)
    bwd_o = lambda j: (N - 1 - jnp.maximum(j - 1, 0), 0)
    cs = lambda w, im: pl.BlockSpec((DN_CHUNK, w), im)
    st = pl.BlockSpec(s0.shape, lambda j: (0, 0, 0))
    const_specs = [pl.BlockSpec(c.shape, lambda j, _n=c.ndim: (0,) * _n) for c in consts]
    return pl.pallas_call(
        _dn_scan_kernel,
        out_shape=(jax.ShapeDtypeStruct((M, BRANCH_W), F32), jax.ShapeDtypeStruct((M, BRANCH_W), F32),
                   jax.ShapeDtypeStruct(s0.shape, F32)),
        grid=(N + 1,),
        in_specs=[cs(BRANCH_W, fwd), cs(BRANCH_W, fwd), cs(BRANCH_W, fwd), cs(LANES, fwd),
                  cs(BRANCH_W, bwd), cs(BRANCH_W, bwd), cs(BRANCH_W, bwd), cs(LANES, bwd), st] + const_specs,
        out_specs=(cs(BRANCH_W, fwd_o), cs(BRANCH_W, bwd_o), st),
        scratch_shapes=[pltpu.VMEM(s0.shape, F32),
                        pltpu.VMEM((DN_CHUNK, FW), F32), pltpu.VMEM((DN_CHUNK, FW), F32),
                        pltpu.VMEM((DN_CHUNK, FW), F32), pltpu.VMEM((DN_CHUNK, FW), BF16),
                        pltpu.VMEM((DN_CHUNK, DN_STATES * DN_CHUNK), BF16), pltpu.VMEM((1, LANES), F32)],
        compiler_params=_cparams(("arbitrary",), 32),
        name="dn_scan",
    )(q, k, v, aux, q, k, v, aux, s0, *consts)


def _dn_out_kernel(of_ref, ob_ref, z_ref, g_ref, y_ref):
    o = of_ref[...] + ob_ref[...]
    z = z_ref[...].astype(F32)
    for h in range(DN_HEADS):
        sl = slice(h * DN_DV, (h + 1) * DN_DV)
        oh = o[:, sl]
        zh = z[:, sl]
        yh = oh * lax.rsqrt(jnp.mean(oh * oh, axis=-1, keepdims=True) + RMS_EPS) * g_ref[...]
        y_ref[:, sl] = (yh * (zh * jax.nn.sigmoid(zh))).astype(y_ref.dtype)


def _dn_out(o_f, o_b, proj, norm_g, tm):
    M = o_f.shape[0]
    blk = pl.BlockSpec((tm, BRANCH_W), lambda m: (m, 0))
    return pl.pallas_call(
        _dn_out_kernel,
        out_shape=jax.ShapeDtypeStruct((M, BRANCH_W), BF16),
        grid=(M // tm,),
        in_specs=[blk, blk, pl.BlockSpec((tm, BRANCH_W), lambda m: (m, COL_DZ // BRANCH_W)),
                  pl.BlockSpec((1, DN_DV), lambda m: (0, 0))],
        out_specs=blk,
        compiler_params=_cparams(("parallel",), 32),
        name="dn_out",
    )(o_f, o_b, proj, norm_g)


def _deltanet(proj, ab, projc, abc, conv_w, a_log, dt_bias, norm_g, want_ctx, tm):
    cw = jnp.pad(conv_w, ((0, 8 - DN_CONV), (0, 0)))
    alog = jnp.pad(a_log.reshape(1, DN_STATES), ((0, 0), (0, LANES - DN_STATES)))
    dtb = jnp.pad(dt_bias.reshape(1, DN_STATES), ((0, 0), (0, LANES - DN_STATES)))
    ql, kl, vl, auxl = _dn_prep(proj, ab, cw, alog, dtb, tm)
    qx, kx, vx, auxx = _dn_prep(projc, abc, cw, alog, dtb, projc.shape[0])
    consts = _dn_constants()
    s0 = jnp.zeros((2, DN_HEADS * DN_DK, DN_DV), F32)
    oc_f, oc_b, s1 = _dn_scan(qx, kx, vx, auxx, s0, consts)
    o_f, o_b, _ = _dn_scan(ql, kl, vl, auxl, s1, consts)
    y = _dn_out(o_f, o_b, proj, norm_g[None], tm)
    yc = _dn_out(oc_f, oc_b, projc, norm_g[None], projc.shape[0]) if want_ctx else None
    return y, yc


SWA_PREP_W = 8 * LANES


def _rope_tables(T):
    half = SWA_HD // 4
    inv = ROPE_BASE ** (-jnp.arange(half, dtype=F32) / half)
    t = jnp.arange(T, dtype=jnp.int32)
    ang_r = (t // GRID_W).astype(F32)[:, None] * inv
    ang_c = (t % GRID_W).astype(F32)[:, None] * inv
    cos = jnp.concatenate([jnp.cos(ang_r)] * 2 + [jnp.cos(ang_c)] * 2, axis=1)
    sin = jnp.concatenate([-jnp.sin(ang_r), jnp.sin(ang_r), -jnp.sin(ang_c), jnp.sin(ang_c)], axis=1)
    return jnp.tile(cos, (1, 2)), jnp.tile(sin, (1, 2))


def _swa_prep_kernel(q_ref, kv_ref, cos_ref, sin_ref, qo_ref, kvo_ref, *, rope):
    lane = lax.broadcasted_iota(jnp.int32, (q_ref.shape[0], LANES), 1)
    first = (lane % (SWA_HD // 2)) < (SWA_HD // 4)
    low = lane < SWA_HD

    def rot(x):
        if not rope:
            return x
        partner = jnp.where(first, pltpu.roll(x, LANES - SWA_HD // 4, 1), pltpu.roll(x, SWA_HD // 4, 1))
        return x * cos_ref[...] + partner * sin_ref[...]

    for c in range(q_ref.shape[1] // LANES):
        cols = slice(c * LANES, (c + 1) * LANES)
        qo_ref[:, cols] = (rot(q_ref[:, cols].astype(F32)) * SWA_HD ** -0.5).astype(qo_ref.dtype)
    kv = kv_ref[...].astype(F32)
    for j, x in enumerate((rot(kv[:, :LANES]), kv[:, LANES:])):
        sw = pltpu.roll(x, SWA_HD, 1)
        parts = (jnp.where(low, x, 0.0), jnp.where(low, 0.0, sw), jnp.where(low, sw, 0.0), jnp.where(low, 0.0, x))
        for i, part in enumerate(parts):
            kvo_ref[:, (4 * j + i) * LANES:(4 * j + i + 1) * LANES] = part.astype(kvo_ref.dtype)


def _swa_prep(proj, cos, sin, rope, tm):
    M = proj.shape[0]
    tab = pl.BlockSpec((tm, LANES), lambda m: (m, 0))
    return pl.pallas_call(
        functools.partial(_swa_prep_kernel, rope=rope),
        out_shape=(jax.ShapeDtypeStruct((M, BRANCH_W), BF16), jax.ShapeDtypeStruct((M, SWA_PREP_W), BF16)),
        grid=(M // tm,),
        in_specs=[pl.BlockSpec((tm, BRANCH_W), lambda m: (m, COL_SQ // BRANCH_W)),
                  pl.BlockSpec((tm, 2 * LANES), lambda m: (m, COL_SK // (2 * LANES))), tab, tab],
        out_specs=(pl.BlockSpec((tm, BRANCH_W), lambda m: (m, 0)), pl.BlockSpec((tm, SWA_PREP_W), lambda m: (m, 0))),
        compiler_params=_cparams(("parallel",), 32),
        name="swa_prep",
    )(proj, proj, cos, sin)


def _swa_masks(n_ctx):
    nk = 3 * SWA_BLOCK + n_ctx
    i = np.arange(SWA_BLOCK)[:, None]
    j = np.arange(nk)[None, :]
    lat = j < 3 * SWA_BLOCK
    band = ~lat | (np.abs(j - SWA_BLOCK - i) <= SWA_WINDOW)
    no_prev = ~(j < SWA_BLOCK) | (i < 0)
    no_next = ~((j >= 2 * SWA_BLOCK) & lat) | (i < 0)
    ctx_only = ~lat | (i < 0)
    m = np.stack([np.where(ok, 0.0, -np.inf) for ok in (band, no_prev, no_next, ctx_only)]).astype(np.float32)
    return jnp.asarray(np.concatenate([m, m], axis=2))


def _swa_attn_kernel(sink_ref, q_ref, kvp_ref, kvo_ref, kvn_ref, kvc_ref, m_ref, y_ref, *, latent_keys):
    n = pl.program_id(0)
    kv = jnp.concatenate([kvp_ref[...], kvo_ref[...], kvn_ref[...], kvc_ref[...]], axis=0)
    nk = kv.shape[0]
    if latent_keys:
        mask = (m_ref[0] + jnp.where(n == 0, m_ref[1], 0.0) + jnp.where(n == pl.num_programs(0) - 1, m_ref[2], 0.0))
    else:
        mask = m_ref[3]
    lane = lax.broadcasted_iota(jnp.int32, (q_ref.shape[0], LANES), 1)
    for c in range(SWA_Q_HEADS // 2):
        kh = (2 * c) // SWA_GROUP
        cols = slice(c * LANES, (c + 1) * LANES)
        kk = jnp.concatenate([kv[:, (2 * kh) * LANES:(2 * kh + 1) * LANES],
                              kv[:, (2 * kh + 1) * LANES:(2 * kh + 2) * LANES]], axis=0)
        vv = jnp.concatenate([kv[:, (4 + 2 * kh) * LANES:(5 + 2 * kh) * LANES],
                              kv[:, (5 + 2 * kh) * LANES:(6 + 2 * kh) * LANES]], axis=0)
        s = lax.dot_general(q_ref[:, cols], kk, (((1,), (1,)), ((), ())), preferred_element_type=F32) + mask
        ps, dens = [], []
        for half in range(2):
            sh = s[:, half * nk:(half + 1) * nk]
            sink = sink_ref[2 * c + half]
            mx = jnp.maximum(jnp.max(sh, axis=1, keepdims=True), sink)
            p = jnp.exp(sh - mx)
            dens.append(jnp.sum(p, axis=1, keepdims=True) + jnp.exp(sink - mx))
            ps.append(p.astype(BF16))
        o = jnp.dot(jnp.concatenate(ps, axis=1), vv, preferred_element_type=F32)
        y_ref[:, cols] = (o / jnp.where(lane < SWA_HD, dens[0], dens[1])).astype(y_ref.dtype)


def _swa_attn(q, kvr, kvc, sink, masks, latent_keys):
    M = q.shape[0]
    NB = M // SWA_BLOCK
    blk = lambda im: pl.BlockSpec((SWA_BLOCK, SWA_PREP_W), im)
    return pl.pallas_call(
        functools.partial(_swa_attn_kernel, latent_keys=latent_keys),
        out_shape=jax.ShapeDtypeStruct((M, BRANCH_W), BF16),
        grid=(NB,),
        in_specs=[pl.BlockSpec(memory_space=pltpu.SMEM),
                  pl.BlockSpec((SWA_BLOCK, BRANCH_W), lambda n: (n, 0)),
                  blk(lambda n: (jnp.maximum(n - 1, 0), 0)), blk(lambda n: (n, 0)),
                  blk(lambda n: (jnp.minimum(n + 1, NB - 1), 0)),
                  pl.BlockSpec(kvc.shape, lambda n: (0, 0)),
                  pl.BlockSpec(masks.shape, lambda n: (0, 0, 0))],
        out_specs=pl.BlockSpec((SWA_BLOCK, BRANCH_W), lambda n: (n, 0)),
        compiler_params=_cparams(("parallel",), 32),
        name="swa_attn",
    )(sink, q, kvr, kvr, kvr, kvc, masks)


def _swa(proj, projc, sink, want_ctx, tm):
    T, Lc = proj.shape[0], projc.shape[0]
    cos, sin = _rope_tables(T)
    q, kvr = _swa_prep(proj, cos, sin, True, tm)
    none = jnp.zeros((Lc, LANES), F32)
    qc, kvc = _swa_prep(projc, none, none, False, Lc)
    masks = _swa_masks(Lc)
    y = _swa_attn(q, kvr, kvc, sink, masks, True)
    yc = _swa_attn(qc, kvc, kvc, sink, masks, False) if want_ctx else None
    return y, yc


def kernel(x, c, ctx, c_ctx, w_ada, b_ada, norm1_g, norm2_g, w_in, pool_w, pool_scale, dn_conv_w, dn_a_log, dn_dt_bias, dn_norm_g, swa_sink, sgu_norm_g, sgu_w, sgu_b, w_gate, w_branch, w_out, w_router, b_router, w_e_gate, w_e_up, w_e_down, final_g):
    B, T, D = x.shape
    assert B == 1 and D == D_MODEL
    Lc = ctx.shape[1]
    depth = w_ada.shape[0]

    cond8 = jnp.zeros((8, D), F32).at[0].set(c[0]).at[1].set(c_ctx)
    mods = _ada(cond8, w_ada, b_ada)

    xl = x[0]
    xc = ctx[0]
    tm_l = 512
    tm_c = Lc
    for i in range(depth):
        want_ctx = i < depth - 1
        ml = [mods[i, 0:1, j * D:(j + 1) * D] for j in range(6)]
        mc = [mods[i, 1:2, j * D:(j + 1) * D] for j in range(6)]
        n1 = norm1_g[i][None]
        n2 = norm2_g[i][None]
        w_main, w_ab = _w_in_prep(w_in, i)
        wgt = w_gate[i].astype(BF16)
        wbr = w_branch[i].astype(BF16)
        wo = w_out[i].astype(BF16)
        wr_t = w_router[i].T.astype(BF16)
        br = b_router[i][:, None]

        proj, ab, h = _inproj(xl, n1, ml[0], ml[1], w_main, w_ab, tm_l)
        projc, abc, hc = _inproj(xc, n1, mc[0], mc[1], w_main, w_ab, tm_c)
        y_pool = _pool(proj, pool_w[i], pool_scale[i][None], tm_l)
        y_dn, yc_dn = _deltanet(proj, ab, projc, abc, dn_conv_w[i], dn_a_log[i], dn_dt_bias[i], dn_norm_g[i],
                                want_ctx, tm_l)
        y_swa, yc_swa = _swa(proj, projc, swa_sink[i], want_ctx, tm_l)
        y_sgu = _sgu(proj, sgu_norm_g[i][None], sgu_w[i], sgu_b[i], tm_l)
        ys = (y_pool, y_dn, y_swa, y_sgu)

        acc = _merge(h, ys, wgt, wbr, tm_l)
        xl, h2, aff_t = _outproj(acc, xl, wo, ml[2], n2, ml[3], ml[4], wr_t, br, tm_l)
        if want_ctx:
            yc_pool = _pool(projc, pool_w[i], pool_scale[i][None], tm_c)
            yc_sgu = _sgu(projc, sgu_norm_g[i][None], sgu_w[i], sgu_b[i], tm_c)
            accc = _merge(hc, (yc_pool, yc_dn, yc_swa, yc_sgu), wgt, wbr, tm_c)
            xc, hc2, affc_t = _outproj(accc, xc, wo, mc[2], n2, mc[3], mc[4], wr_t, br, tm_c)
            xl, xc = _moe(xl, h2, aff_t, ml[5], xc, hc2, affc_t, mc[5], w_e_gate, w_e_up, w_e_down, i,
                          final_g[None], False)
        else:
            xl, _ = _moe(xl, h2, aff_t, ml[5], None, None, None, None, w_e_gate, w_e_up, w_e_down, i,
                         final_g[None], i == depth - 1)

    return xl[None]
```

```python
import functools
import math

import numpy as np
import jax
import jax.numpy as jnp
from jax import lax
from jax.experimental import pallas as pl
from jax.experimental.pallas import tpu as pltpu

F32 = jnp.float32
BF16 = jnp.bfloat16

D_MODEL = 2048
GRID_W = 64
N_BRANCH = 4
BRANCH_W = D_MODEL // N_BRANCH
POOL_WINDOWS = (2, 4, 8, 16)
POOL_GROUP_W = BRANCH_W // len(POOL_WINDOWS)
DN_DK = 128
DN_DV = 128
DN_HEADS = BRANCH_W // DN_DV
DN_CHUNK = 64
SWA_HD = 64
SWA_Q_HEADS = BRANCH_W // SWA_HD
SWA_KV_HEADS = SWA_Q_HEADS // 4
SWA_GROUP = SWA_Q_HEADS // SWA_KV_HEADS
SWA_WINDOW = 128
SWA_BLOCK = 128
ROPE_BASE = 10000.0
SGU_CHUNK = 128
SGU_GROUPS = 4
SGU_GROUP_W = BRANCH_W // SGU_GROUPS
N_EXPERTS = 16
EC_FACTOR = 2
RMS_EPS = 1e-6

IN_SPLITS = (BRANCH_W,
             DN_HEADS * DN_DK, DN_HEADS * DN_DK,
             DN_HEADS * DN_DV, DN_HEADS * DN_DV,
             2 * DN_HEADS, 2 * DN_HEADS,
             SWA_Q_HEADS * SWA_HD, SWA_KV_HEADS * SWA_HD, SWA_KV_HEADS * SWA_HD,
             BRANCH_W, BRANCH_W)
IN_OFFSETS = tuple(int(o) for o in np.cumsum(IN_SPLITS)[:-1])
AB_LO = IN_OFFSETS[4]
AB_W = 4 * DN_HEADS
MAIN_W = sum(IN_SPLITS) - AB_W
COL_DZ, COL_P, COL_SQ, COL_GU, COL_GV, COL_SK, COL_SV = 1536, 2048, 2560, 3072, 3584, 4096, 4224
LANES = 128
MOE_TILE = 256
MOE_ALIGN = 16
MOE_WIN = 64
MIB = 2 ** 20


def _cparams(sem, vmem_mib):
    return pltpu.CompilerParams(dimension_semantics=sem, vmem_limit_bytes=vmem_mib * MIB)


def _ada_kernel(c_ref, w_ref, b_ref, o_ref):
    c = c_ref[...]
    s = (c * jax.nn.sigmoid(c)).astype(BF16)
    o_ref[0] = jnp.dot(s, w_ref[0].astype(BF16), preferred_element_type=F32) + b_ref[0]


def _ada(cond8, w_ada, b_ada):
    L, D, N = w_ada.shape
    tn = 1024
    return pl.pallas_call(
        _ada_kernel,
        out_shape=jax.ShapeDtypeStruct((L, 8, N), F32),
        grid=(L, N // tn),
        in_specs=[pl.BlockSpec((8, D), lambda l, n: (0, 0)),
                  pl.BlockSpec((1, D, tn), lambda l, n: (l, 0, n)),
                  pl.BlockSpec((1, 1, tn), lambda l, n: (l, 0, n))],
        out_specs=pl.BlockSpec((1, 8, tn), lambda l, n: (l, 0, n)),
        compiler_params=_cparams(("parallel", "parallel"), 32),
        name="ada_mod",
    )(cond8, w_ada, b_ada.reshape(L, 1, N))


def _w_in_prep_kernel(w_ref, main_ref, ab_ref):
    o = (0,) + IN_OFFSETS + (sum(IN_SPLITS),)
    pos = 0
    for lo, hi in ((o[1], o[5]), (o[0], o[1]), (o[7], o[8]), (o[10], o[12]), (o[8], o[10])):
        main_ref[:, pos:pos + hi - lo] = w_ref[:, lo:hi].astype(main_ref.dtype)
        pos += hi - lo
    ab_ref[...] = jnp.zeros(ab_ref.shape, ab_ref.dtype)
    ab_ref[:, 0:AB_W] = w_ref[:, AB_LO:AB_LO + AB_W].astype(ab_ref.dtype)


def _w_in_prep(w_in, layer):
    _, D, N = w_in.shape
    tk = 256
    return pl.pallas_call(
        _w_in_prep_kernel,
        out_shape=(jax.ShapeDtypeStruct((D, MAIN_W), BF16), jax.ShapeDtypeStruct((D, LANES), BF16)),
        grid=(D // tk,),
        in_specs=[pl.BlockSpec((None, tk, N), lambda r: (layer, r, 0))],
        out_specs=(pl.BlockSpec((tk, MAIN_W), lambda r: (r, 0)), pl.BlockSpec((tk, LANES), lambda r: (r, 0))),
        compiler_params=_cparams(("parallel",), 32),
        name="w_in_prep",
    )(w_in)


def _inproj_kernel(x_ref, g_ref, sh_ref, sc_ref, w_ref, wab_ref, proj_ref, ab_ref, h_ref, hs_ref):
    @pl.when(pl.program_id(1) == 0)
    def _():
        x = x_ref[...]
        y = x * lax.rsqrt(jnp.mean(x * x, axis=-1, keepdims=True) + RMS_EPS) * g_ref[...]
        hb = (y * (1.0 + sc_ref[...]) + sh_ref[...]).astype(BF16)
        hs_ref[...] = hb
        h_ref[...] = hb
        ab_ref[...] = jnp.dot(hb, wab_ref[...], preferred_element_type=F32)

    proj_ref[...] = jnp.dot(hs_ref[...], w_ref[...], preferred_element_type=F32).astype(proj_ref.dtype)


def _inproj(x, g, sh, sc, w_main, w_ab, tm):
    M, D = x.shape
    N = w_main.shape[1]
    tn = N // 2
    vec = pl.BlockSpec((1, D), lambda m, n: (0, 0))
    return pl.pallas_call(
        _inproj_kernel,
        out_shape=(jax.ShapeDtypeStruct((M, N), BF16),
                   jax.ShapeDtypeStruct((M, LANES), F32),
                   jax.ShapeDtypeStruct((M, D), BF16)),
        grid=(M // tm, N // tn),
        in_specs=[pl.BlockSpec((tm, D), lambda m, n: (m, 0)), vec, vec, vec,
                  pl.BlockSpec((D, tn), lambda m, n: (0, n)),
                  pl.BlockSpec((D, LANES), lambda m, n: (0, 0))],
        out_specs=(pl.BlockSpec((tm, tn), lambda m, n: (m, n)),
                   pl.BlockSpec((tm, LANES), lambda m, n: (m, 0)),
                   pl.BlockSpec((tm, D), lambda m, n: (m, 0))),
        scratch_shapes=[pltpu.VMEM((tm, D), BF16)],
        compiler_params=_cparams(("parallel", "arbitrary"), 48),
        name="inproj",
    )(x, g, sh, sc, w_main, w_ab)


def _merge_kernel(h_ref, y0_ref, y1_ref, y2_ref, y3_ref, wg_ref, wb_ref, o_ref):
    h = h_ref[...]
    acc = None
    for i, y_ref in enumerate((y0_ref, y1_ref, y2_ref, y3_ref)):
        gate = jnp.dot(h, wg_ref[i], preferred_element_type=F32)
        br = jnp.dot(y_ref[...], wb_ref[i], preferred_element_type=F32)
        t = jax.nn.sigmoid(gate) * br
        acc = t if acc is None else acc + t
    o_ref[...] = acc.astype(o_ref.dtype)


def _merge(h, ys, w_gate, w_branch, tm):
    M, D = h.shape
    tn = 512
    ybs = pl.BlockSpec((tm, BRANCH_W), lambda m, n: (m, 0))
    return pl.pallas_call(
        _merge_kernel,
        out_shape=jax.ShapeDtypeStruct((M, D), BF16),
        grid=(M // tm, D // tn),
        in_specs=[pl.BlockSpec((tm, D), lambda m, n: (m, 0)), ybs, ybs, ybs, ybs,
                  pl.BlockSpec((N_BRANCH, D, tn), lambda m, n: (0, 0, n)),
                  pl.BlockSpec((N_BRANCH, BRANCH_W, tn), lambda m, n: (0, 0, n))],
        out_specs=pl.BlockSpec((tm, tn), lambda m, n: (m, n)),
        compiler_params=_cparams(("parallel", "arbitrary"), 48),
        name="merge",
    )(h, *ys, w_gate, w_branch)


def _outproj_kernel(acc_ref, x_ref, w_ref, g1_ref, ng_ref, sh_ref, sc_ref, wr_ref, br_ref,
                    xo_ref, h2_ref, aff_ref):
    out = jnp.dot(acc_ref[...], w_ref[...], preferred_element_type=F32)
    xn = x_ref[...] + g1_ref[...] * out
    xo_ref[...] = xn
    y = xn * lax.rsqrt(jnp.mean(xn * xn, axis=-1, keepdims=True) + RMS_EPS) * ng_ref[...]
    h2 = (y * (1.0 + sc_ref[...]) + sh_ref[...]).astype(BF16)
    h2_ref[...] = h2
    logits = lax.dot_general(wr_ref[...], h2, (((1,), (1,)), ((), ())), preferred_element_type=F32)
    logits = logits + br_ref[...]
    e = jnp.exp(logits - jnp.max(logits, axis=0, keepdims=True))
    aff = e / jnp.sum(e, axis=0, keepdims=True)
    for j in range(aff_ref.shape[0]):
        aff_ref[j] = aff[:, j * MOE_TILE:(j + 1) * MOE_TILE]


def _outproj(acc, x, w_out, g1, ng, sh, sc, w_router_t, b_router, tm):
    M, D = x.shape
    E = w_router_t.shape[0]
    vec = pl.BlockSpec((1, D), lambda m: (0, 0))
    return pl.pallas_call(
        _outproj_kernel,
        out_shape=(jax.ShapeDtypeStruct((M, D), F32),
                   jax.ShapeDtypeStruct((M, D), BF16),
                   jax.ShapeDtypeStruct((M // MOE_TILE, E, MOE_TILE), F32)),
        grid=(M // tm,),
        in_specs=[pl.BlockSpec((tm, D), lambda m: (m, 0)),
                  pl.BlockSpec((tm, D), lambda m: (m, 0)),
                  pl.BlockSpec((D, D), lambda m: (0, 0)),
                  vec, vec, vec, vec,
                  pl.BlockSpec((E, D), lambda m: (0, 0)),
                  pl.BlockSpec((E, 1), lambda m: (0, 0))],
        out_specs=(pl.BlockSpec((tm, D), lambda m: (m, 0)),
                   pl.BlockSpec((tm, D), lambda m: (m, 0)),
                   pl.BlockSpec((tm // MOE_TILE, E, MOE_TILE), lambda m: (m, 0, 0))),
        compiler_params=_cparams(("parallel",), 48),
        name="outproj",
    )(acc, x, w_out, g1, ng, sh, sc, w_router_t, b_router)


def _route_kernel(aff_ref, u_ref, l_ref, rank_ref, base_ref, tot_ref, *, cap):
    aff = aff_ref[...]
    nt, E, W = aff.shape

    def count(mask):
        return jnp.sum(jnp.sum(mask.astype(jnp.int32), axis=0), axis=1, keepdims=True)

    def search(it, thr):
        cand = thr | lax.shift_left(jnp.int32(1), 30 - it)
        c = count(aff >= lax.bitcast_convert_type(cand, F32)[None])
        return jnp.where(c >= cap, cand, thr)

    thr = lax.bitcast_convert_type(lax.fori_loop(0, 31, search, jnp.zeros((E, 1), jnp.int32)), F32)[None]
    gt = aff > thr
    eq = aff == thr
    need = (cap - count(gt)).astype(F32)[None]

    def prefix(mask):
        m2 = jnp.where(mask, 1.0, 0.0).astype(BF16).reshape(nt * E, W)
        incl = jnp.dot(m2, u_ref[...], preferred_element_type=F32)
        tot = jnp.broadcast_to(incl[:, W - 1:W], (nt * E, LANES))
        base = jnp.dot(l_ref[...], tot.astype(BF16), preferred_element_type=F32)
        return (incl - m2.astype(F32) + base[:, 0:1]).reshape(nt, E, W), base, tot

    eq_rank, _, _ = prefix(eq)
    sel = gt | (eq & (eq_rank < need))
    rank, base, tot = prefix(sel)
    rank_ref[...] = jnp.where(sel, rank.astype(jnp.int32), -1)
    base_ref[...] = base.astype(jnp.int32)
    tot_ref[...] = tot.astype(jnp.int32)


def _route(aff_tm, cap):
    nt, E, W = aff_tm.shape
    upper = np.triu(np.ones((W, W), np.float32))
    idx = np.arange(nt * E)
    lower = ((idx[:, None] % E) == (idx[None, :] % E)) & ((idx[None, :] // E) < (idx[:, None] // E))
    full = lambda a: pl.BlockSpec(a.shape, lambda: (0,) * a.ndim)
    consts = (jnp.asarray(upper, BF16), jnp.asarray(lower, BF16))
    return pl.pallas_call(
        functools.partial(_route_kernel, cap=cap),
        out_shape=(jax.ShapeDtypeStruct((nt, E, W), jnp.int32), jax.ShapeDtypeStruct((nt * E, LANES), jnp.int32),
                   jax.ShapeDtypeStruct((nt * E, LANES), jnp.int32)),
        in_specs=[full(aff_tm), full(consts[0]), full(consts[1])],
        out_specs=(pl.BlockSpec((nt, E, W), lambda: (0, 0, 0)), pl.BlockSpec((nt * E, LANES), lambda: (0, 0)),
                   pl.BlockSpec((nt * E, LANES), lambda: (0, 0))),
        compiler_params=pltpu.CompilerParams(vmem_limit_bytes=40 * MIB),
        name="moe_route",
    )(aff_tm, *consts)


def _slot_onehots(rank, cur, lo_ref, cnt_ref, b, R, clamp_hi):
    E = rank.shape[0]
    riota = lax.broadcasted_iota(jnp.int32, (R, rank.shape[1]), 0)
    ohs, starts, news = [], [], []
    for e in range(E):
        c = cur[e]
        end = lo_ref[b * E + e] + cnt_ref[b * E + e]
        w = (c // MOE_ALIGN) * MOE_ALIGN
        if clamp_hi is not None:
            w = jnp.minimum(w, clamp_hi)
        rk = rank[e:e + 1, :]
        ohs.append(((rk - w) == riota) & (rk >= c))
        starts.append(w)
        news.append(jnp.minimum(end, w + R))
    return ohs, starts, news


def _gather_kernel(lo_ref, cnt_ref, h_ref, rank_ref, aff_ref, xe_ref, xg_ref,
                   stage, gstage, carry, gcarry, cur, npass, sem, *, R, cap):
    b = pl.program_id(0)
    E = rank_ref.shape[1]
    A = MOE_ALIGN

    @pl.when(b == 0)
    def _():
        carry[...] = jnp.zeros_like(carry)
        gcarry[...] = jnp.zeros_like(gcarry)
        npass[0] = 0
        stage[1, 0:R, :] = jnp.zeros((R, stage.shape[2]), stage.dtype)
        gstage[1, 0:R, :] = jnp.zeros((R, LANES), F32)
        pads = []
        for e in range(E):
            pads.append(pltpu.make_async_copy(stage.at[1, pl.ds(0, R)], xe_ref.at[e, pl.ds(cap, R)], sem.at[1]))
            pads.append(pltpu.make_async_copy(gstage.at[1, pl.ds(0, R)], xg_ref.at[e, pl.ds(cap, R)], sem.at[1]))
        for cp in pads:
            cp.start()
        for cp in pads:
            cp.wait()

    for e in range(E):
        cur[e] = lo_ref[b * E + e]
    rank = rank_ref[0]
    aff = aff_ref[0]
    h = h_ref[...]
    srow = lax.broadcasted_iota(jnp.int32, (A, 1), 0)

    def copies(slot, starts):
        cps = []
        for e in range(E):
            w = pl.multiple_of(starts[e], A)
            cps.append(pltpu.make_async_copy(stage.at[slot, pl.ds(e * R, R)], xe_ref.at[e, pl.ds(w, R)], sem.at[slot]))
            cps.append(pltpu.make_async_copy(gstage.at[slot, pl.ds(e * R, R)], xg_ref.at[e, pl.ds(w, R)], sem.at[slot]))
        return cps

    def one_pass(more):
        g = npass[0]
        slot = g % 2
        ohs, starts, news = _slot_onehots(rank, cur, lo_ref, cnt_ref, b, R, None)
        rows = jnp.dot(jnp.concatenate([jnp.where(oh, 1.0, 0.0).astype(BF16) for oh in ohs], axis=0), h,
                       preferred_element_type=F32)
        more = jnp.int32(0)
        for e in range(E):
            gate = jnp.sum(jnp.where(ohs[e], aff[e:e + 1, :], 0.0), axis=1, keepdims=True)
            gate = jnp.broadcast_to(gate, (R, LANES))
            stage[slot, e * R:e * R + A, :] = (rows[e * R:e * R + A] + carry[e]).astype(stage.dtype)
            stage[slot, e * R + A:(e + 1) * R, :] = rows[e * R + A:(e + 1) * R].astype(stage.dtype)
            gstage[slot, e * R:e * R + A, :] = gate[:A] + gcarry[e]
            gstage[slot, e * R + A:(e + 1) * R, :] = gate[A:]
            new = news[e]
            grp = jnp.minimum((new - starts[e]) // A, R // A - 1)
            keep = srow < (new - (new // A) * A)
            off = pl.multiple_of(e * R + grp * A, A)
            carry[e] = jnp.where(keep, stage[slot, pl.ds(off, A), :].astype(F32), 0.0)
            gcarry[e] = jnp.where(keep, gstage[slot, pl.ds(off, A), :], 0.0)
            cur[e] = new
            more = jnp.maximum(more, (new < lo_ref[b * E + e] + cnt_ref[b * E + e]).astype(jnp.int32))

        @pl.when(g > 0)
        def _():
            for cp in copies(1 - slot, [0] * E):
                cp.wait()

        for cp in copies(slot, starts):
            cp.start()
        npass[0] = g + 1
        return more

    lax.while_loop(lambda more: more > 0, one_pass, jnp.int32(1))

    @pl.when(b == pl.num_programs(0) - 1)
    def _():
        for cp in copies((npass[0] - 1) % 2, [0] * E):
            cp.wait()


def _gather(h2, rank, aff_tm, lo, cnt, cap, R):
    n, D = h2.shape
    nt, E, W = rank.shape
    grid_spec = pltpu.PrefetchScalarGridSpec(
        num_scalar_prefetch=2,
        grid=(nt,),
        in_specs=[pl.BlockSpec((W, D), lambda b, lo, cnt: (b, 0)),
                  pl.BlockSpec((1, E, W), lambda b, lo, cnt: (b, 0, 0)),
                  pl.BlockSpec((1, E, W), lambda b, lo, cnt: (b, 0, 0))],
        out_specs=(pl.BlockSpec(memory_space=pl.ANY), pl.BlockSpec(memory_space=pl.ANY)),
        scratch_shapes=[pltpu.VMEM((2, E * R, D), BF16), pltpu.VMEM((2, E * R, LANES), F32),
                        pltpu.VMEM((E, MOE_ALIGN, D), F32), pltpu.VMEM((E, MOE_ALIGN, LANES), F32),
                        pltpu.SMEM((E,), jnp.int32), pltpu.SMEM((1,), jnp.int32),
                        pltpu.SemaphoreType.DMA((2,))])
    return pl.pallas_call(
        functools.partial(_gather_kernel, R=R, cap=cap),
        out_shape=(jax.ShapeDtypeStruct((E, cap + R, D), BF16), jax.ShapeDtypeStruct((E, cap + R, LANES), F32)),
        grid_spec=grid_spec,
        compiler_params=_cparams(("arbitrary",), 48),
        name="moe_gather",
    )(lo, cnt, h2, rank, aff_tm)


FFN_STEPS = 4


def _ffn_kernel(*refs, has_ctx):
    if has_ctx:
        x_ref, g_ref, xc_ref, gc_ref, wg_ref, wu_ref, wd_ref, o_ref, oc_ref, wgs, wus, wds = refs
    else:
        x_ref, g_ref, wg_ref, wu_ref, wd_ref, o_ref, wgs, wus, wds = refs
    e = pl.program_id(0)
    s = pl.program_id(1)
    n_exp = pl.num_programs(0) - 1
    fq = wg_ref.shape[2]

    @pl.when(e < n_exp)
    def _():
        slot = e % 2
        for j in range(FFN_STEPS):

            @pl.when(s == j)
            def _():
                wgs[slot, :, j * fq:(j + 1) * fq] = wg_ref[0].astype(BF16)
                wus[slot, :, j * fq:(j + 1) * fq] = wu_ref[0].astype(BF16)
                wds[slot, j * fq:(j + 1) * fq, :] = wd_ref[0].astype(BF16)

    def expert(x, gate, slot):
        a = jnp.dot(x, wgs[slot], preferred_element_type=F32)
        b = jnp.dot(x, wus[slot], preferred_element_type=F32)
        hid = (a * jax.nn.sigmoid(a) * b).astype(BF16)
        return jnp.dot(hid, wds[slot], preferred_element_type=F32) * gate[:, 0:1]

    @pl.when(e >= 1)
    def _():
        slot = (e - 1) % 2
        o_ref[0] = expert(x_ref[0], g_ref[0], slot).astype(o_ref.dtype)
        if has_ctx:

            @pl.when(s == 0)
            def _():
                oc_ref[0] = expert(xc_ref[0], gc_ref[0], slot).astype(oc_ref.dtype)


def _ffn(xe, xg, wg, wu, wd, layer, cap, ctx=None):
    E, _, D = xe.shape
    F = wg.shape[3]
    S = FFN_STEPS
    M = S if cap % (S * LANES) == 0 else 1
    tm = cap // M
    fq = F // S
    rows = lambda e, s: (jnp.maximum(e - 1, 0), jnp.where(e == 0, 0, s % M), 0)
    crow = lambda e, s: (jnp.maximum(e - 1, 0), 0, 0)
    in_specs = [pl.BlockSpec((1, tm, D), rows), pl.BlockSpec((1, tm, LANES), rows)]
    out_shape = [jax.ShapeDtypeStruct((E, cap, D), BF16)]
    out_specs = [pl.BlockSpec((1, tm, D), rows)]
    args = [xe, xg]
    if ctx is not None:
        xec, xgc, capc = ctx
        in_specs += [pl.BlockSpec((1, capc, D), crow), pl.BlockSpec((1, capc, LANES), crow)]
        out_shape.append(jax.ShapeDtypeStruct((E, capc, D), BF16))
        out_specs.append(pl.BlockSpec((1, capc, D), crow))
        args += [xec, xgc]
    in_specs += [pl.BlockSpec((None, 1, D, fq), lambda e, s: (layer, jnp.minimum(e, E - 1), 0, s)),
                 pl.BlockSpec((None, 1, D, fq), lambda e, s: (layer, jnp.minimum(e, E - 1), 0, s)),
                 pl.BlockSpec((None, 1, fq, D), lambda e, s: (layer, jnp.minimum(e, E - 1), s, 0))]
    return pl.pallas_call(
        functools.partial(_ffn_kernel, has_ctx=ctx is not None),
        out_shape=tuple(out_shape),
        grid=(E + 1, S),
        in_specs=in_specs,
        out_specs=tuple(out_specs),
        scratch_shapes=[pltpu.VMEM((2, D, F), BF16), pltpu.VMEM((2, D, F), BF16), pltpu.VMEM((2, F, D), BF16)],
        compiler_params=_cparams(("arbitrary", "arbitrary"), 52),
        name="expert_ffn",
    )(*args, wg, wu, wd)


def _combine_kernel(lo_ref, cnt_ref, rank_ref, x_ref, g2_ref, ng_ref, y_ref, o_ref, ystage, yovf, acc, cur, sem, osem,
                    *, R, cap, final_norm):
    b = pl.program_id(0)
    nb = pl.num_programs(0)
    E = rank_ref.shape[1]
    A = MOE_ALIGN

    def window(c):
        return pl.multiple_of(jnp.minimum((c // A) * A, cap - R), A)

    def main_copies(tile, slot):
        return [pltpu.make_async_copy(y_ref.at[e, pl.ds(window(lo_ref[tile * E + e]), R)],
                                      ystage.at[slot, pl.ds(e * R, R)], sem.at[slot]) for e in range(E)]

    @pl.when(b == 0)
    def _():
        for cp in main_copies(0, 0):
            cp.start()

    @pl.when(b + 1 < nb)
    def _():
        for cp in main_copies(b + 1, (b + 1) % 2):
            cp.start()

    for e in range(E):
        cur[e] = lo_ref[b * E + e]
    rank = rank_ref[0]

    def expand(y_rows):
        ohs, _, news = _slot_onehots(rank, cur, lo_ref, cnt_ref, b, R, cap - R)
        more = jnp.int32(0)
        for e in range(E):
            cur[e] = news[e]
            more = jnp.maximum(more, (news[e] < lo_ref[b * E + e] + cnt_ref[b * E + e]).astype(jnp.int32))
        oh = jnp.concatenate([jnp.where(o, 1.0, 0.0).astype(BF16) for o in ohs], axis=0)
        return lax.dot_general(oh, y_rows, (((0,), (0,)), ((), ())), preferred_element_type=F32), more

    for cp in main_copies(b, b % 2):
        cp.wait()
    first, more = expand(ystage[b % 2])
    acc[...] = first

    def extra_pass(more):
        cps = [pltpu.make_async_copy(y_ref.at[e, pl.ds(window(cur[e]), R)], yovf.at[pl.ds(e * R, R)], osem.at[0])
               for e in range(E)]
        for cp in cps:
            cp.start()
        for cp in cps:
            cp.wait()
        part, more = expand(yovf[...])
        acc[...] += part
        return more

    lax.while_loop(lambda more: more > 0, extra_pass, more)
    out = x_ref[...] + g2_ref[...] * acc[...]
    if final_norm:
        out = out * lax.rsqrt(jnp.mean(out * out, axis=-1, keepdims=True) + RMS_EPS) * ng_ref[...]
    o_ref[...] = out


def _combine(y, rank, lo, cnt, x, g2, norm_g, final_norm, cap, R):
    n, D = x.shape
    nt, E, W = rank.shape
    grid_spec = pltpu.PrefetchScalarGridSpec(
        num_scalar_prefetch=2,
        grid=(nt,),
        in_specs=[pl.BlockSpec((1, E, W), lambda b, lo, cnt: (b, 0, 0)),
                  pl.BlockSpec((W, D), lambda b, lo, cnt: (b, 0)),
                  pl.BlockSpec((1, D), lambda b, lo, cnt: (0, 0)),
                  pl.BlockSpec((1, D), lambda b, lo, cnt: (0, 0)),
                  pl.BlockSpec(memory_space=pl.ANY)],
        out_specs=pl.BlockSpec((W, D), lambda b, lo, cnt: (b, 0)),
        scratch_shapes=[pltpu.VMEM((2, E * R, D), BF16), pltpu.VMEM((E * R, D), BF16), pltpu.VMEM((W, D), F32),
                        pltpu.SMEM((E,), jnp.int32), pltpu.SemaphoreType.DMA((2,)), pltpu.SemaphoreType.DMA((1,))])
    return pl.pallas_call(
        functools.partial(_combine_kernel, R=R, cap=cap, final_norm=final_norm),
        out_shape=jax.ShapeDtypeStruct((n, D), F32),
        grid_spec=grid_spec,
        compiler_params=_cparams(("arbitrary",), 48),
        name="moe_combine",
    )(lo, cnt, rank, x, g2, norm_g, y)


def _moe_dispatch(h2, aff_tm):
    n = h2.shape[0]
    cap = EC_FACTOR * n // aff_tm.shape[1]
    R = min(MOE_WIN, cap)
    rank, base, tot = _route(aff_tm, cap)
    lo, cnt = base[:, 0], tot[:, 0]
    xe, xg = _gather(h2, rank, aff_tm, lo, cnt, cap, R)
    return xe, xg, (rank, lo, cnt, cap, R)


def _moe(xl, h2, aff_tm, g2, xc, hc2, affc_tm, gc2, wg, wu, wd, layer, norm_g, final_norm):
    xe, xg, (rank, lo, cnt, cap, R) = _moe_dispatch(h2, aff_tm)
    if xc is None:
        (y,) = _ffn(xe, xg, wg, wu, wd, layer, cap)
        return _combine(y, rank, lo, cnt, xl, g2, norm_g, final_norm, cap, R), None
    xec, xgc, (rankc, loc, cntc, capc, Rc) = _moe_dispatch(hc2, affc_tm)
    y, yc = _ffn(xe, xg, wg, wu, wd, layer, cap, (xec, xgc, capc))
    return (_combine(y, rank, lo, cnt, xl, g2, norm_g, final_norm, cap, R),
            _combine(yc, rankc, loc, cntc, xc, gc2, norm_g, False, capc, Rc))


POOL_HALO = 16


def _pool_kernel(cur_ref, prev_ref, next_ref, w_ref, sc_ref, y_ref, scr, *, seq_len):
    m = pl.program_id(0)
    tm = cur_ref.shape[0]
    H = POOL_HALO
    scr[0:H, :] = jnp.where(m > 0, prev_ref[...].astype(F32), 0.0)
    scr[H:H + tm, :] = cur_ref[...].astype(F32)
    scr[H + tm:2 * H + tm, :] = jnp.where(m < pl.num_programs(0) - 1, next_ref[...].astype(F32), 0.0)
    t = m * tm + lax.broadcasted_iota(jnp.int32, (tm, 1), 0)
    for gi, w in enumerate(POOL_WINDOWS):
        cols = slice(gi * POOL_GROUP_W, (gi + 1) * POOL_GROUP_W)
        total = None
        for k in range(-(w // 2), w // 2):
            term = scr[H + k:H + k + tm, cols]
            total = term if total is None else total + term
        n = jnp.minimum(t + (w // 2 - 1), seq_len - 1) - jnp.maximum(t - w // 2, 0) + 1
        centred = total / n.astype(F32) - scr[H:H + tm, cols]
        y = jnp.dot(centred.astype(BF16), w_ref[gi].astype(BF16), preferred_element_type=F32)
        y_ref[:, cols] = (y * sc_ref[:, cols]).astype(y_ref.dtype)


def _pool(proj, pool_w, pool_scale, tm):
    M = proj.shape[0]
    hb = tm // POOL_HALO
    nhb = M // POOL_HALO
    cb = COL_P // BRANCH_W
    return pl.pallas_call(
        functools.partial(_pool_kernel, seq_len=M),
        out_shape=jax.ShapeDtypeStruct((M, BRANCH_W), BF16),
        grid=(M // tm,),
        in_specs=[pl.BlockSpec((tm, BRANCH_W), lambda m: (m, cb)),
                  pl.BlockSpec((POOL_HALO, BRANCH_W), lambda m: (jnp.maximum(m * hb - 1, 0), cb)),
                  pl.BlockSpec((POOL_HALO, BRANCH_W), lambda m: (jnp.minimum((m + 1) * hb, nhb - 1), cb)),
                  pl.BlockSpec(pool_w.shape, lambda m: (0, 0, 0)),
                  pl.BlockSpec((1, BRANCH_W), lambda m: (0, 0))],
        out_specs=pl.BlockSpec((tm, BRANCH_W), lambda m: (m, 0)),
        scratch_shapes=[pltpu.VMEM((tm + 2 * POOL_HALO, BRANCH_W), F32)],
        compiler_params=_cparams(("parallel",), 32),
        name="pool_mix",
    )(proj, proj, proj, pool_w, pool_scale)


def _gelu(x):
    return 0.5 * x * (1.0 + lax.erf(x * (2.0 ** -0.5)))


def _sgu_kernel(u_ref, v_ref, g_ref, w_ref, b_ref, y_ref):
    tm = u_ref.shape[0]
    v = _gelu(v_ref[...].astype(F32))
    v = (v * lax.rsqrt(jnp.mean(v * v, axis=-1, keepdims=True) + RMS_EPS) * g_ref[...]).astype(BF16)
    u = _gelu(u_ref[...].astype(F32))
    for c in range(tm // SGU_CHUNK):
        rows = slice(c * SGU_CHUNK, (c + 1) * SGU_CHUNK)
        for g in range(SGU_GROUPS):
            cols = slice(g * SGU_GROUP_W, (g + 1) * SGU_GROUP_W)
            mixed = jnp.dot(w_ref[g].astype(BF16), v[rows, cols], preferred_element_type=F32) + b_ref[:, g:g + 1]
            y_ref[rows, cols] = (u[rows, cols] * mixed).astype(y_ref.dtype)


def _sgu(proj, norm_g, w_s, b_s, tm):
    M = proj.shape[0]
    bt = jnp.pad(b_s.T, ((0, 0), (0, LANES - SGU_GROUPS)))
    return pl.pallas_call(
        _sgu_kernel,
        out_shape=jax.ShapeDtypeStruct((M, BRANCH_W), BF16),
        grid=(M // tm,),
        in_specs=[pl.BlockSpec((tm, BRANCH_W), lambda m: (m, COL_GU // BRANCH_W)),
                  pl.BlockSpec((tm, BRANCH_W), lambda m: (m, COL_GV // BRANCH_W)),
                  pl.BlockSpec((1, BRANCH_W), lambda m: (0, 0)),
                  pl.BlockSpec(w_s.shape, lambda m: (0, 0, 0)),
                  pl.BlockSpec((SGU_CHUNK, LANES), lambda m: (0, 0))],
        out_specs=pl.BlockSpec((tm, BRANCH_W), lambda m: (m, 0)),
        compiler_params=_cparams(("parallel",), 32),
        name="sgu_mix",
    )(proj, proj, norm_g, w_s, bt)


DN_CONV = 5
DN_HALO = 16
DN_QKV_W = 3 * BRANCH_W
DN_STATES = 2 * DN_HEADS


def _dn_prep_kernel(cur_ref, prev_ref, next_ref, ab_ref, cw_ref, alog_ref, dtb_ref,
                    q_ref, k_ref, v_ref, aux_ref, scr):
    m = pl.program_id(0)
    tm = cur_ref.shape[0]
    scr[0:DN_HALO, :] = jnp.where(m > 0, prev_ref[...].astype(F32), 0.0)
    scr[DN_HALO:DN_HALO + tm, :] = cur_ref[...].astype(F32)
    scr[DN_HALO + tm:2 * DN_HALO + tm, :] = jnp.where(m < pl.num_programs(0) - 1, next_ref[...].astype(F32), 0.0)
    acc = None
    for t in range(DN_CONV):
        lo = DN_HALO - DN_CONV // 2 + t
        term = scr[lo:lo + tm, :] * cw_ref[t:t + 1, :]
        acc = term if acc is None else acc + term
    y = acc * jax.nn.sigmoid(acc)
    for h in range(DN_HEADS):
        qh = y[:, h * DN_DK:(h + 1) * DN_DK]
        q_ref[:, h * DN_DK:(h + 1) * DN_DK] = (
            qh * lax.rsqrt(jnp.sum(qh * qh, axis=-1, keepdims=True) + 1e-6) * DN_DK ** -0.5).astype(q_ref.dtype)
        kh = y[:, BRANCH_W + h * DN_DK:BRANCH_W + (h + 1) * DN_DK]
        k_ref[:, h * DN_DK:(h + 1) * DN_DK] = (
            kh * lax.rsqrt(jnp.sum(kh * kh, axis=-1, keepdims=True) + 1e-6)).astype(k_ref.dtype)
    v_ref[...] = y[:, 2 * BRANCH_W:].astype(v_ref.dtype)

    ab = ab_ref[...]
    lane = lax.broadcasted_iota(jnp.int32, ab.shape, 1)
    sp = ab + dtb_ref[...]
    softplus = jnp.maximum(sp, 0.0) + jnp.log(1.0 + jnp.exp(-jnp.abs(sp)))
    g = jnp.where(lane < DN_STATES, -jnp.exp(alog_ref[...]) * softplus, 0.0)
    beta = jax.nn.sigmoid(ab)
    g1 = g.astype(BF16)
    r1 = g - g1.astype(F32)
    g2 = r1.astype(BF16)
    g3 = (r1 - g2.astype(F32)).astype(BF16)
    gs = jnp.concatenate([g1, g2, g3], axis=1)
    row = lax.broadcasted_iota(jnp.int32, (tm, tm), 0)
    col = lax.broadcasted_iota(jnp.int32, (tm, tm), 1)
    same = (row // DN_CHUNK) == (col // DN_CHUNK)
    tri_f = jnp.where(same & (col <= row), 1.0, 0.0).astype(BF16)
    tri_b = jnp.where(same & (col >= row), 1.0, 0.0).astype(BF16)
    cf = jnp.dot(tri_f, gs, preferred_element_type=F32)
    cb = jnp.dot(tri_b, gs, preferred_element_type=F32)
    cf = cf[:, :LANES] + cf[:, LANES:2 * LANES] + cf[:, 2 * LANES:]
    cb = cb[:, :LANES] + cb[:, LANES:2 * LANES] + cb[:, 2 * LANES:]
    gc = jnp.where(lane < DN_HEADS, cf, cb)
    aux_ref[...] = jnp.where(lane < DN_STATES, gc, beta)


def _dn_prep(proj, ab, cw, alog, dtb, tm):
    M = proj.shape[0]
    hb = tm // DN_HALO
    nhb = M // DN_HALO
    row = pl.BlockSpec((1, LANES), lambda m: (0, 0))
    return pl.pallas_call(
        _dn_prep_kernel,
        out_shape=(jax.ShapeDtypeStruct((M, BRANCH_W), BF16),) * 3 + (jax.ShapeDtypeStruct((M, LANES), F32),),
        grid=(M // tm,),
        in_specs=[pl.BlockSpec((tm, DN_QKV_W), lambda m: (m, 0)),
                  pl.BlockSpec((DN_HALO, DN_QKV_W), lambda m: (jnp.maximum(m * hb - 1, 0), 0)),
                  pl.BlockSpec((DN_HALO, DN_QKV_W), lambda m: (jnp.minimum((m + 1) * hb, nhb - 1), 0)),
                  pl.BlockSpec((tm, LANES), lambda m: (m, 0)),
                  pl.BlockSpec((8, DN_QKV_W), lambda m: (0, 0)), row, row],
        out_specs=(pl.BlockSpec((tm, BRANCH_W), lambda m: (m, 0)),) * 3 + (pl.BlockSpec((tm, LANES), lambda m: (m, 0)),),
        scratch_shapes=[pltpu.VMEM((tm + 2 * DN_HALO, DN_QKV_W), F32)],
        compiler_params=_cparams(("parallel",), 40),
        name="dn_prep",
    )(proj, proj, proj, ab, cw, alog, dtb)


def _dn_constants():
    L, H = DN_CHUNK, DN_HEADS
    WW, FW = DN_STATES * L, DN_STATES * DN_DK
    r = np.arange(L)[:, None]
    lane = np.arange(WW)[None, :]
    c = lane % L
    lower = lane < H * L
    cm = np.zeros((9, L, WW), np.float32)
    cm[0] = r == c
    cm[1] = np.where(np.where(lower, r >= c, r <= c), 0.0, -np.inf)
    cm[2] = np.where(lower, r > c, r < c)
    hi, lo = np.where(lower, r, c), np.where(lower, c, r)
    for j in range(6):
        s = 2 ** j
        cm[3 + j] = ((r // (2 * s)) == (c // (2 * s))) & (hi % (2 * s) >= s) & (lo % (2 * s) < s)
    rw = np.arange(WW)[:, None]
    rf = np.arange(H * DN_DK)[:, None]
    m_ww = rw // L == np.arange(WW)[None] // L
    m_wf = rw // L == np.arange(FW)[None] // DN_DK
    m_w2f = rw // L == (np.arange(2 * FW)[None] % FW) // DN_DK
    m_ff = rf // DN_DK == np.arange(H * DN_DK)[None] // DN_DK
    return (jnp.asarray(cm),) + tuple(jnp.asarray(m, BF16) for m in (m_ww, m_wf, m_w2f, m_ff))


def _dn_scan_kernel(qf_ref, kf_ref, vf_ref, af_ref, qb_ref, kb_ref, vb_ref, ab_ref, s0_ref,
                    cm_ref, mww_ref, mwf_ref, mw2f_ref, mff_ref, of_ref, ob_ref, sfin_ref,
                    s_scr, u_st, w_st, qd_st, kd_st, qk_st, gl_st):
    j = pl.program_id(0)
    L, H = DN_CHUNK, DN_HEADS
    FD = H * DN_DK

    @pl.when(j == 0)
    def _():
        s_scr[...] = s0_ref[...]
        for st in (u_st, w_st, qd_st, kd_st, qk_st, gl_st):
            st[...] = jnp.zeros(st.shape, st.dtype)

    lane = lax.broadcasted_iota(jnp.int32, (L, LANES), 1)
    rowblk = lax.broadcasted_iota(jnp.int32, (FD, DN_DV), 0) // DN_DK
    m_ww, m_wf, m_w2f, m_ff = mww_ref[...], mwf_ref[...], mw2f_ref[...], mff_ref[...]

    def tile_rows(a):
        return jnp.concatenate([a] * DN_STATES, axis=0)

    def col(arr, c):
        return jnp.broadcast_to(arr[:, c:c + 1], (arr.shape[0], LANES))

    def feat(arrs, base):
        return jnp.concatenate([col(arrs[d], base + d * H + h) for d in range(2) for h in range(H)], axis=1)

    def wide(arrs, base):
        return jnp.concatenate([jnp.where(lane < L, col(arrs[d], base + d * H + 2 * p),
                                          col(arrs[d], base + d * H + 2 * p + 1))
                                for d in range(2) for p in range(H // 2)], axis=1)

    u, w, qd, kd, qk = u_st[...], w_st[...], qd_st[...], kd_st[...], qk_st[...]
    g_prev = gl_st[...]
    live = j > 0
    dir_cols = [slice(d * FD, (d + 1) * FD) for d in range(2)]
    s_old = [s_scr[d] for d in range(2)]
    val = {}

    def ws_qs(d):
        s_bd = jnp.concatenate([s_old[d].astype(BF16)] * H, axis=1) * m_ff
        val["ws", d] = jnp.dot(jnp.concatenate([w[:, dir_cols[d]], qd[:, dir_cols[d]]], axis=0).astype(BF16), s_bd,
                               preferred_element_type=F32)

    def outputs():
        val["v"] = (u - jnp.concatenate([val["ws", 0][:L], val["ws", 1][:L]], axis=1)).astype(BF16)
        val["o"] = (jnp.concatenate([val["ws", 0][L:], val["ws", 1][L:]], axis=1)
                    + jnp.dot(qk, tile_rows(val["v"]) * m_wf, preferred_element_type=F32))

    def state(d):
        cross = lax.dot_general(kd[:, dir_cols[d]], val["v"][:, dir_cols[d]], (((0,), (0,)), ((), ())),
                                preferred_element_type=F32)
        fold = None
        g_last = None
        for h in range(H):
            part = jnp.where(rowblk == h, cross[:, h * DN_DV:(h + 1) * DN_DV], 0.0)
            gl = jnp.where(rowblk == h, g_prev[:, d * H + h:d * H + h + 1], 0.0)
            fold = part if fold is None else fold + part
            g_last = gl if g_last is None else g_last + gl
        val["s", d] = jnp.where(live, g_last * s_old[d] + fold, s_old[d])

    queue = [lambda: ws_qs(0), lambda: ws_qs(1), outputs, lambda: state(0), lambda: state(1)]

    def issue_one():
        if queue:
            queue.pop(0)()

    aux = (af_ref[...], ab_ref[...])
    last = (aux[0][L - 1:L, :], aux[1][0:1, :])
    e_gc = tuple(jnp.exp(a) for a in aux)
    e_rem = tuple(jnp.exp(l - a) for l, a in zip(last, aux))
    beta_f = feat(aux, DN_STATES)
    eg_f = feat(e_gc, 0)
    er_f = feat(e_rem, 0)
    gcw = wide(aux, 0)
    eye = cm_ref[0]
    gc_row = jnp.sum(eye * gcw, axis=0, keepdims=True)
    decay = jnp.exp(gcw - gc_row + cm_ref[1])
    q = jnp.concatenate([qf_ref[...], qb_ref[...]], axis=1).astype(F32)
    k = jnp.concatenate([kf_ref[...], kb_ref[...]], axis=1).astype(F32)
    v = jnp.concatenate([vf_ref[...], vb_ref[...]], axis=1).astype(F32)
    kbeta = k * beta_f
    kq = lax.dot_general(jnp.concatenate([kbeta, q], axis=0).astype(BF16), tile_rows(k.astype(BF16)) * m_wf,
                         (((1,), (1,)), ((), ())), preferred_element_type=F32)
    issue_one()
    mm = kq[:L] * decay * cm_ref[2]
    x = eye - mm * cm_ref[3]
    for lvl in range(4, 9):
        xb = x.astype(BF16)
        t = jnp.dot(xb, tile_rows((mm * cm_ref[lvl]).astype(BF16)) * m_ww, preferred_element_type=F32)
        issue_one()
        x = x - jnp.dot(t.astype(BF16), tile_rows(xb) * m_ww, preferred_element_type=F32)
        issue_one()
    rhs = jnp.concatenate([v * beta_f, kbeta * eg_f], axis=1)
    sol = rhs + jnp.dot((x - eye).astype(BF16), tile_rows(rhs.astype(BF16)) * m_w2f, preferred_element_type=F32)
    while queue:
        issue_one()
    for d, o_ref in enumerate((of_ref, ob_ref)):
        s_scr[d] = val["s", d]
        o_ref[...] = val["o"][:, dir_cols[d]]
    u_st[...] = sol[:, :2 * FD]
    w_st[...] = sol[:, 2 * FD:]
    qd_st[...] = q * eg_f
    kd_st[...] = (k * er_f).astype(BF16)
    qk_st[...] = (kq[L:] * decay).astype(BF16)
    gl_st[...] = jnp.where(lane[0:1] < H, jnp.exp(last[0]), jnp.exp(last[1]))

    @pl.when(j == pl.num_programs(0) - 1)
    def _():
        sfin_ref[...] = s_scr[...]


def _dn_scan(q, k, v, aux, s0, consts):
    M = q.shape[0]
    N = M // DN_CHUNK
    FW = 2 * BRANCH_W
    fwd = lambda j: (jnp.minimum(j, N - 1), 0)
    bwd = lambda j: (jnp.maximum(N - 1 - j, 0), 0)
    fwd_o = lambda j: (jnp.maximum(j - 1, 0), 0)
    bwd_o = lambda j: (N - 1 - jnp.maximum(j - 1, 0), 0)
    cs = lambda w, im: pl.BlockSpec((DN_CHUNK, w), im)
    st = pl.BlockSpec(s0.shape, lambda j: (0, 0, 0))
    const_specs = [pl.BlockSpec(c.shape, lambda j, _n=c.ndim: (0,) * _n) for c in consts]
    return pl.pallas_call(
        _dn_scan_kernel,
        out_shape=(jax.ShapeDtypeStruct((M, BRANCH_W), F32), jax.ShapeDtypeStruct((M, BRANCH_W), F32),
                   jax.ShapeDtypeStruct(s0.shape, F32)),
        grid=(N + 1,),
        in_specs=[cs(BRANCH_W, fwd), cs(BRANCH_W, fwd), cs(BRANCH_W, fwd), cs(LANES, fwd),
                  cs(BRANCH_W, bwd), cs(BRANCH_W, bwd), cs(BRANCH_W, bwd), cs(LANES, bwd), st] + const_specs,
        out_specs=(cs(BRANCH_W, fwd_o), cs(BRANCH_W, bwd_o), st),
        scratch_shapes=[pltpu.VMEM(s0.shape, F32),
                        pltpu.VMEM((DN_CHUNK, FW), F32), pltpu.VMEM((DN_CHUNK, FW), F32),
                        pltpu.VMEM((DN_CHUNK, FW), F32), pltpu.VMEM((DN_CHUNK, FW), BF16),
                        pltpu.VMEM((DN_CHUNK, DN_STATES * DN_CHUNK), BF16), pltpu.VMEM((1, LANES), F32)],
        compiler_params=_cparams(("arbitrary",), 32),
        name="dn_scan",
    )(q, k, v, aux, q, k, v, aux, s0, *consts)


def _dn_out_kernel(of_ref, ob_ref, z_ref, g_ref, y_ref):
    o = of_ref[...] + ob_ref[...]
    z = z_ref[...].astype(F32)
    for h in range(DN_HEADS):
        sl = slice(h * DN_DV, (h + 1) * DN_DV)
        oh = o[:, sl]
        zh = z[:, sl]
        yh = oh * lax.rsqrt(jnp.mean(oh * oh, axis=-1, keepdims=True) + RMS_EPS) * g_ref[...]
        y_ref[:, sl] = (yh * (zh * jax.nn.sigmoid(zh))).astype(y_ref.dtype)


def _dn_out(o_f, o_b, proj, norm_g, tm):
    M = o_f.shape[0]
    blk = pl.BlockSpec((tm, BRANCH_W), lambda m: (m, 0))
    return pl.pallas_call(
        _dn_out_kernel,
        out_shape=jax.ShapeDtypeStruct((M, BRANCH_W), BF16),
        grid=(M // tm,),
        in_specs=[blk, blk, pl.BlockSpec((tm, BRANCH_W), lambda m: (m, COL_DZ // BRANCH_W)),
                  pl.BlockSpec((1, DN_DV), lambda m: (0, 0))],
        out_specs=blk,
        compiler_params=_cparams(("parallel",), 32),
        name="dn_out",
    )(o_f, o_b, proj, norm_g)


def _deltanet(proj, ab, projc, abc, conv_w, a_log, dt_bias, norm_g, want_ctx, tm):
    cw = jnp.pad(conv_w, ((0, 8 - DN_CONV), (0, 0)))
    alog = jnp.pad(a_log.reshape(1, DN_STATES), ((0, 0), (0, LANES - DN_STATES)))
    dtb = jnp.pad(dt_bias.reshape(1, DN_STATES), ((0, 0), (0, LANES - DN_STATES)))
    ql, kl, vl, auxl = _dn_prep(proj, ab, cw, alog, dtb, tm)
    qx, kx, vx, auxx = _dn_prep(projc, abc, cw, alog, dtb, projc.shape[0])
    consts = _dn_constants()
    s0 = jnp.zeros((2, DN_HEADS * DN_DK, DN_DV), F32)
    oc_f, oc_b, s1 = _dn_scan(qx, kx, vx, auxx, s0, consts)
    o_f, o_b, _ = _dn_scan(ql, kl, vl, auxl, s1, consts)
    y = _dn_out(o_f, o_b, proj, norm_g[None], tm)
    yc = _dn_out(oc_f, oc_b, projc, norm_g[None], projc.shape[0]) if want_ctx else None
    return y, yc


SWA_PREP_W = 8 * LANES


def _rope_tables(T):
    half = SWA_HD // 4
    inv = ROPE_BASE ** (-jnp.arange(half, dtype=F32) / half)
    t = jnp.arange(T, dtype=jnp.int32)
    ang_r = (t // GRID_W).astype(F32)[:, None] * inv
    ang_c = (t % GRID_W).astype(F32)[:, None] * inv
    cos = jnp.concatenate([jnp.cos(ang_r)] * 2 + [jnp.cos(ang_c)] * 2, axis=1)
    sin = jnp.concatenate([-jnp.sin(ang_r), jnp.sin(ang_r), -jnp.sin(ang_c), jnp.sin(ang_c)], axis=1)
    return jnp.tile(cos, (1, 2)), jnp.tile(sin, (1, 2))


def _swa_prep_kernel(q_ref, kv_ref, cos_ref, sin_ref, qo_ref, kvo_ref, *, rope):
    lane = lax.broadcasted_iota(jnp.int32, (q_ref.shape[0], LANES), 1)
    first = (lane % (SWA_HD // 2)) < (SWA_HD // 4)
    low = lane < SWA_HD

    def rot(x):
        if not rope:
            return x
        partner = jnp.where(first, pltpu.roll(x, LANES - SWA_HD // 4, 1), pltpu.roll(x, SWA_HD // 4, 1))
        return x * cos_ref[...] + partner * sin_ref[...]

    for c in range(q_ref.shape[1] // LANES):
        cols = slice(c * LANES, (c + 1) * LANES)
        qo_ref[:, cols] = (rot(q_ref[:, cols].astype(F32)) * SWA_HD ** -0.5).astype(qo_ref.dtype)
    kv = kv_ref[...].astype(F32)
    for j, x in enumerate((rot(kv[:, :LANES]), kv[:, LANES:])):
        sw = pltpu.roll(x, SWA_HD, 1)
        parts = (jnp.where(low, x, 0.0), jnp.where(low, 0.0, sw), jnp.where(low, sw, 0.0), jnp.where(low, 0.0, x))
        for i, part in enumerate(parts):
            kvo_ref[:, (4 * j + i) * LANES:(4 * j + i + 1) * LANES] = part.astype(kvo_ref.dtype)


def _swa_prep(proj, cos, sin, rope, tm):
    M = proj.shape[0]
    tab = pl.BlockSpec((tm, LANES), lambda m: (m, 0))
    return pl.pallas_call(
        functools.partial(_swa_prep_kernel, rope=rope),
        out_shape=(jax.ShapeDtypeStruct((M, BRANCH_W), BF16), jax.ShapeDtypeStruct((M, SWA_PREP_W), BF16)),
        grid=(M // tm,),
        in_specs=[pl.BlockSpec((tm, BRANCH_W), lambda m: (m, COL_SQ // BRANCH_W)),
                  pl.BlockSpec((tm, 2 * LANES), lambda m: (m, COL_SK // (2 * LANES))), tab, tab],
        out_specs=(pl.BlockSpec((tm, BRANCH_W), lambda m: (m, 0)), pl.BlockSpec((tm, SWA_PREP_W), lambda m: (m, 0))),
        compiler_params=_cparams(("parallel",), 32),
        name="swa_prep",
    )(proj, proj, cos, sin)


def _swa_masks(n_ctx):
    nk = 3 * SWA_BLOCK + n_ctx
    i = np.arange(SWA_BLOCK)[:, None]
    j = np.arange(nk)[None, :]
    lat = j < 3 * SWA_BLOCK
    band = ~lat | (np.abs(j - SWA_BLOCK - i) <= SWA_WINDOW)
    no_prev = ~(j < SWA_BLOCK) | (i < 0)
    no_next = ~((j >= 2 * SWA_BLOCK) & lat) | (i < 0)
    ctx_only = ~lat | (i < 0)
    m = np.stack([np.where(ok, 0.0, -np.inf) for ok in (band, no_prev, no_next, ctx_only)]).astype(np.float32)
    return jnp.asarray(np.concatenate([m, m], axis=2))


def _swa_attn_kernel(sink_ref, q_ref, kvp_ref, kvo_ref, kvn_ref, kvc_ref, m_ref, y_ref, *, latent_keys):
    n = pl.program_id(0)
    kv = jnp.concatenate([kvp_ref[...], kvo_ref[...], kvn_ref[...], kvc_ref[...]], axis=0)
    nk = kv.shape[0]
    if latent_keys:
        mask = (m_ref[0] + jnp.where(n == 0, m_ref[1], 0.0) + jnp.where(n == pl.num_programs(0) - 1, m_ref[2], 0.0))
    else:
        mask = m_ref[3]
    lane = lax.broadcasted_iota(jnp.int32, (q_ref.shape[0], LANES), 1)
    for c in range(SWA_Q_HEADS // 2):
        kh = (2 * c) // SWA_GROUP
        cols = slice(c * LANES, (c + 1) * LANES)
        kk = jnp.concatenate([kv[:, (2 * kh) * LANES:(2 * kh + 1) * LANES],
                              kv[:, (2 * kh + 1) * LANES:(2 * kh + 2) * LANES]], axis=0)
        vv = jnp.concatenate([kv[:, (4 + 2 * kh) * LANES:(5 + 2 * kh) * LANES],
                              kv[:, (5 + 2 * kh) * LANES:(6 + 2 * kh) * LANES]], axis=0)
        s = lax.dot_general(q_ref[:, cols], kk, (((1,), (1,)), ((), ())), preferred_element_type=F32) + mask
        ps, dens = [], []
        for half in range(2):
            sh = s[:, half * nk:(half + 1) * nk]
            sink = sink_ref[2 * c + half]
            mx = jnp.maximum(jnp.max(sh, axis=1, keepdims=True), sink)
            p = jnp.exp(sh - mx)
            dens.append(jnp.sum(p, axis=1, keepdims=True) + jnp.exp(sink - mx))
            ps.append(p.astype(BF16))
        o = jnp.dot(jnp.concatenate(ps, axis=1), vv, preferred_element_type=F32)
        y_ref[:, cols] = (o / jnp.where(lane < SWA_HD, dens[0], dens[1])).astype(y_ref.dtype)


def _swa_attn(q, kvr, kvc, sink, masks, latent_keys):
    M = q.shape[0]
    NB = M // SWA_BLOCK
    blk = lambda im: pl.BlockSpec((SWA_BLOCK, SWA_PREP_W), im)
    return pl.pallas_call(
        functools.partial(_swa_attn_kernel, latent_keys=latent_keys),
        out_shape=jax.ShapeDtypeStruct((M, BRANCH_W), BF16),
        grid=(NB,),
        in_specs=[pl.BlockSpec(memory_space=pltpu.SMEM),
                  pl.BlockSpec((SWA_BLOCK, BRANCH_W), lambda n: (n, 0)),
                  blk(lambda n: (jnp.maximum(n - 1, 0), 0)), blk(lambda n: (n, 0)),
                  blk(lambda n: (jnp.minimum(n + 1, NB - 1), 0)),
                  pl.BlockSpec(kvc.shape, lambda n: (0, 0)),
                  pl.BlockSpec(masks.shape, lambda n: (0, 0, 0))],
        out_specs=pl.BlockSpec((SWA_BLOCK, BRANCH_W), lambda n: (n, 0)),
        compiler_params=_cparams(("parallel",), 32),
        name="swa_attn",
    )(sink, q, kvr, kvr, kvr, kvc, masks)


def _swa(proj, projc, sink, want_ctx, tm):
    T, Lc = proj.shape[0], projc.shape[0]
    cos, sin = _rope_tables(T)
    q, kvr = _swa_prep(proj, cos, sin, True, tm)
    none = jnp.zeros((Lc, LANES), F32)
    qc, kvc = _swa_prep(projc, none, none, False, Lc)
    masks = _swa_masks(Lc)
    y = _swa_attn(q, kvr, kvc, sink, masks, True)
    yc = _swa_attn(qc, kvc, kvc, sink, masks, False) if want_ctx else None
    return y, yc


def kernel(x, c, ctx, c_ctx, w_ada, b_ada, norm1_g, norm2_g, w_in, pool_w, pool_scale, dn_conv_w, dn_a_log, dn_dt_bias, dn_norm_g, swa_sink, sgu_norm_g, sgu_w, sgu_b, w_gate, w_branch, w_out, w_router, b_router, w_e_gate, w_e_up, w_e_down, final_g):
    B, T, D = x.shape
    assert B == 1 and D == D_MODEL
    Lc = ctx.shape[1]
    depth = w_ada.shape[0]

    cond8 = jnp.zeros((8, D), F32).at[0].set(c[0]).at[1].set(c_ctx)
    mods = _ada(cond8, w_ada, b_ada)

    xl = x[0]
    xc = ctx[0]
    tm_l = 512
    tm_c = Lc
    for i in range(depth):
        want_ctx = i < depth - 1
        ml = [mods[i, 0:1, j * D:(j + 1) * D] for j in range(6)]
        mc = [mods[i, 1:2, j * D:(j + 1) * D] for j in range(6)]
        n1 = norm1_g[i][None]
        n2 = norm2_g[i][None]
        w_main, w_ab = _w_in_prep(w_in, i)
        wgt = w_gate[i].astype(BF16)
        wbr = w_branch[i].astype(BF16)
        wo = w_out[i].astype(BF16)
        wr_t = w_router[i].T.astype(BF16)
        br = b_router[i][:, None]

        proj, ab, h = _inproj(xl, n1, ml[0], ml[1], w_main, w_ab, tm_l)
        projc, abc, hc = _inproj(xc, n1, mc[0], mc[1], w_main, w_ab, tm_c)
        y_pool = _pool(proj, pool_w[i], pool_scale[i][None], tm_l)
        y_dn, yc_dn = _deltanet(proj, ab, projc, abc, dn_conv_w[i], dn_a_log[i], dn_dt_bias[i], dn_norm_g[i],
                                want_ctx, tm_l)
        y_swa, yc_swa = _swa(proj, projc, swa_sink[i], want_ctx, tm_l)
        y_sgu = _sgu(proj, sgu_norm_g[i][None], sgu_w[i], sgu_b[i], tm_l)
        ys = (y_pool, y_dn, y_swa, y_sgu)

        acc = _merge(h, ys, wgt, wbr, tm_l)
        xl, h2, aff_t = _outproj(acc, xl, wo, ml[2], n2, ml[3], ml[4], wr_t, br, tm_l)
        if want_ctx:
            yc_pool = _pool(projc, pool_w[i], pool_scale[i][None], tm_c)
            yc_sgu = _sgu(projc, sgu_norm_g[i][None], sgu_w[i], sgu_b[i], tm_c)
            accc = _merge(hc, (yc_pool, yc_dn, yc_swa, yc_sgu), wgt, wbr, tm_c)
            xc, hc2, affc_t = _outproj(accc, xc, wo, mc[2], n2, mc[3], mc[4], wr_t, br, tm_c)
            xl, xc = _moe(xl, h2, aff_t, ml[5], xc, hc2, affc_t, mc[5], w_e_gate, w_e_up, w_e_down, i,
                          final_g[None], False)
        else:
            xl, _ = _moe(xl, h2, aff_t, ml[5], None, None, None, None, w_e_gate, w_e_up, w_e_down, i,
                         final_g[None], i == depth - 1)

    return xl[None]
```

```python
import functools
import math

import numpy as np
import jax
import jax.numpy as jnp
from jax import lax
from jax.experimental import pallas as pl
from jax.experimental.pallas import tpu as pltpu

F32 = jnp.float32
BF16 = jnp.bfloat16

D_MODEL = 2048
GRID_W = 64
N_BRANCH = 4
BRANCH_W = D_MODEL // N_BRANCH
POOL_WINDOWS = (2, 4, 8, 16)
POOL_GROUP_W = BRANCH_W // len(POOL_WINDOWS)
DN_DK = 128
DN_DV = 128
DN_HEADS = BRANCH_W // DN_DV
DN_CHUNK = 64
SWA_HD = 64
SWA_Q_HEADS = BRANCH_W // SWA_HD
SWA_KV_HEADS = SWA_Q_HEADS // 4
SWA_GROUP = SWA_Q_HEADS // SWA_KV_HEADS
SWA_WINDOW = 128
SWA_BLOCK = 128
ROPE_BASE = 10000.0
SGU_CHUNK = 128
SGU_GROUPS = 4
SGU_GROUP_W = BRANCH_W // SGU_GROUPS
N_EXPERTS = 16
EC_FACTOR = 2
RMS_EPS = 1e-6

IN_SPLITS = (BRANCH_W,
             DN_HEADS * DN_DK, DN_HEADS * DN_DK,
             DN_HEADS * DN_DV, DN_HEADS * DN_DV,
             2 * DN_HEADS, 2 * DN_HEADS,
             SWA_Q_HEADS * SWA_HD, SWA_KV_HEADS * SWA_HD, SWA_KV_HEADS * SWA_HD,
             BRANCH_W, BRANCH_W)
IN_OFFSETS = tuple(int(o) for o in np.cumsum(IN_SPLITS)[:-1])
AB_LO = IN_OFFSETS[4]
AB_W = 4 * DN_HEADS
MAIN_W = sum(IN_SPLITS) - AB_W
COL_DZ, COL_P, COL_SQ, COL_GU, COL_GV, COL_SK, COL_SV = 1536, 2048, 2560, 3072, 3584, 4096, 4224
LANES = 128
MOE_TILE = 256
MOE_ALIGN = 16
MOE_WIN = 64
MIB = 2 ** 20


def _cparams(sem, vmem_mib):
    return pltpu.CompilerParams(dimension_semantics=sem, vmem_limit_bytes=vmem_mib * MIB)


def _ada_kernel(c_ref, w_ref, b_ref, o_ref):
    c = c_ref[...]
    s = (c * jax.nn.sigmoid(c)).astype(BF16)
    o_ref[0] = jnp.dot(s, w_ref[0].astype(BF16), preferred_element_type=F32) + b_ref[0]


def _ada(cond8, w_ada, b_ada):
    L, D, N = w_ada.shape
    tn = 1024
    return pl.pallas_call(
        _ada_kernel,
        out_shape=jax.ShapeDtypeStruct((L, 8, N), F32),
        grid=(L, N // tn),
        in_specs=[pl.BlockSpec((8, D), lambda l, n: (0, 0)),
                  pl.BlockSpec((1, D, tn), lambda l, n: (l, 0, n)),
                  pl.BlockSpec((1, 1, tn), lambda l, n: (l, 0, n))],
        out_specs=pl.BlockSpec((1, 8, tn), lambda l, n: (l, 0, n)),
        compiler_params=_cparams(("parallel", "parallel"), 32),
        name="ada_mod",
    )(cond8, w_ada, b_ada.reshape(L, 1, N))


def _w_in_prep_kernel(w_ref, main_ref, ab_ref):
    o = (0,) + IN_OFFSETS + (sum(IN_SPLITS),)
    pos = 0
    for lo, hi in ((o[1], o[5]), (o[0], o[1]), (o[7], o[8]), (o[10], o[12]), (o[8], o[10])):
        main_ref[:, pos:pos + hi - lo] = w_ref[:, lo:hi].astype(main_ref.dtype)
        pos += hi - lo
    ab_ref[...] = jnp.zeros(ab_ref.shape, ab_ref.dtype)
    ab_ref[:, 0:AB_W] = w_ref[:, AB_LO:AB_LO + AB_W].astype(ab_ref.dtype)


def _w_in_prep(w_in, layer):
    _, D, N = w_in.shape
    tk = 256
    return pl.pallas_call(
        _w_in_prep_kernel,
        out_shape=(jax.ShapeDtypeStruct((D, MAIN_W), BF16), jax.ShapeDtypeStruct((D, LANES), BF16)),
        grid=(D // tk,),
        in_specs=[pl.BlockSpec((None, tk, N), lambda r: (layer, r, 0))],
        out_specs=(pl.BlockSpec((tk, MAIN_W), lambda r: (r, 0)), pl.BlockSpec((tk, LANES), lambda r: (r, 0))),
        compiler_params=_cparams(("parallel",), 32),
        name="w_in_prep",
    )(w_in)


def _inproj_kernel(x_ref, g_ref, sh_ref, sc_ref, w_ref, wab_ref, proj_ref, ab_ref, h_ref, hs_ref):
    @pl.when(pl.program_id(1) == 0)
    def _():
        x = x_ref[...]
        y = x * lax.rsqrt(jnp.mean(x * x, axis=-1, keepdims=True) + RMS_EPS) * g_ref[...]
        hb = (y * (1.0 + sc_ref[...]) + sh_ref[...]).astype(BF16)
        hs_ref[...] = hb
        h_ref[...] = hb
        ab_ref[...] = jnp.dot(hb, wab_ref[...], preferred_element_type=F32)

    proj_ref[...] = jnp.dot(hs_ref[...], w_ref[...], preferred_element_type=F32).astype(proj_ref.dtype)


def _inproj(x, g, sh, sc, w_main, w_ab, tm):
    M, D = x.shape
    N = w_main.shape[1]
    tn = N // 2
    vec = pl.BlockSpec((1, D), lambda m, n: (0, 0))
    return pl.pallas_call(
        _inproj_kernel,
        out_shape=(jax.ShapeDtypeStruct((M, N), BF16),
                   jax.ShapeDtypeStruct((M, LANES), F32),
                   jax.ShapeDtypeStruct((M, D), BF16)),
        grid=(M // tm, N // tn),
        in_specs=[pl.BlockSpec((tm, D), lambda m, n: (m, 0)), vec, vec, vec,
                  pl.BlockSpec((D, tn), lambda m, n: (0, n)),
                  pl.BlockSpec((D, LANES), lambda m, n: (0, 0))],
        out_specs=(pl.BlockSpec((tm, tn), lambda m, n: (m, n)),
                   pl.BlockSpec((tm, LANES), lambda m, n: (m, 0)),
                   pl.BlockSpec((tm, D), lambda m, n: (m, 0))),
        scratch_shapes=[pltpu.VMEM((tm, D), BF16)],
        compiler_params=_cparams(("parallel", "arbitrary"), 48),
        name="inproj",
    )(x, g, sh, sc, w_main, w_ab)


def _merge_kernel(h_ref, y0_ref, y1_ref, y2_ref, y3_ref, wg_ref, wb_ref, o_ref):
    h = h_ref[...]
    acc = None
    for i, y_ref in enumerate((y0_ref, y1_ref, y2_ref, y3_ref)):
        gate = jnp.dot(h, wg_ref[i], preferred_element_type=F32)
        br = jnp.dot(y_ref[...], wb_ref[i], preferred_element_type=F32)
        t = jax.nn.sigmoid(gate) * br
        acc = t if acc is None else acc + t
    o_ref[...] = acc.astype(o_ref.dtype)


def _merge(h, ys, w_gate, w_branch, tm):
    M, D = h.shape
    tn = 512
    ybs = pl.BlockSpec((tm, BRANCH_W), lambda m, n: (m, 0))
    return pl.pallas_call(
        _merge_kernel,
        out_shape=jax.ShapeDtypeStruct((M, D), BF16),
        grid=(M // tm, D // tn),
        in_specs=[pl.BlockSpec((tm, D), lambda m, n: (m, 0)), ybs, ybs, ybs, ybs,
                  pl.BlockSpec((N_BRANCH, D, tn), lambda m, n: (0, 0, n)),
                  pl.BlockSpec((N_BRANCH, BRANCH_W, tn), lambda m, n: (0, 0, n))],
        out_specs=pl.BlockSpec((tm, tn), lambda m, n: (m, n)),
        compiler_params=_cparams(("parallel", "arbitrary"), 48),
        name="merge",
    )(h, *ys, w_gate, w_branch)


def _outproj_kernel(acc_ref, x_ref, w_ref, g1_ref, ng_ref, sh_ref, sc_ref, wr_ref, br_ref,
                    xo_ref, h2_ref, aff_ref):
    out = jnp.dot(acc_ref[...], w_ref[...], preferred_element_type=F32)
    xn = x_ref[...] + g1_ref[...] * out
    xo_ref[...] = xn
    y = xn * lax.rsqrt(jnp.mean(xn * xn, axis=-1, keepdims=True) + RMS_EPS) * ng_ref[...]
    h2 = (y * (1.0 + sc_ref[...]) + sh_ref[...]).astype(BF16)
    h2_ref[...] = h2
    logits = lax.dot_general(wr_ref[...], h2, (((1,), (1,)), ((), ())), preferred_element_type=F32)
    logits = logits + br_ref[...]
    e = jnp.exp(logits - jnp.max(logits, axis=0, keepdims=True))
    aff = e / jnp.sum(e, axis=0, keepdims=True)
    for j in range(aff_ref.shape[0]):
        aff_ref[j] = aff[:, j * MOE_TILE:(j + 1) * MOE_TILE]


def _outproj(acc, x, w_out, g1, ng, sh, sc, w_router_t, b_router, tm):
    M, D = x.shape
    E = w_router_t.shape[0]
    vec = pl.BlockSpec((1, D), lambda m: (0, 0))
    return pl.pallas_call(
        _outproj_kernel,
        out_shape=(jax.ShapeDtypeStruct((M, D), F32),
                   jax.ShapeDtypeStruct((M, D), BF16),
                   jax.ShapeDtypeStruct((M // MOE_TILE, E, MOE_TILE), F32)),
        grid=(M // tm,),
        in_specs=[pl.BlockSpec((tm, D), lambda m: (m, 0)),
                  pl.BlockSpec((tm, D), lambda m: (m, 0)),
                  pl.BlockSpec((D, D), lambda m: (0, 0)),
                  vec, vec, vec, vec,
                  pl.BlockSpec((E, D), lambda m: (0, 0)),
                  pl.BlockSpec((E, 1), lambda m: (0, 0))],
        out_specs=(pl.BlockSpec((tm, D), lambda m: (m, 0)),
                   pl.BlockSpec((tm, D), lambda m: (m, 0)),
                   pl.BlockSpec((tm // MOE_TILE, E, MOE_TILE), lambda m: (m, 0, 0))),
        compiler_params=_cparams(("parallel",), 48),
        name="outproj",
    )(acc, x, w_out, g1, ng, sh, sc, w_router_t, b_router)


def _route_kernel(aff_ref, u_ref, l_ref, rank_ref, base_ref, tot_ref, *, cap):
    aff = aff_ref[...]
    nt, E, W = aff.shape

    def count(mask):
        return jnp.sum(jnp.sum(mask.astype(jnp.int32), axis=0), axis=1, keepdims=True)

    def search(it, thr):
        cand = thr | lax.shift_left(jnp.int32(1), 30 - it)
        c = count(aff >= lax.bitcast_convert_type(cand, F32)[None])
        return jnp.where(c >= cap, cand, thr)

    thr = lax.bitcast_convert_type(lax.fori_loop(0, 31, search, jnp.zeros((E, 1), jnp.int32)), F32)[None]
    gt = aff > thr
    eq = aff == thr
    need = (cap - count(gt)).astype(F32)[None]

    def prefix(mask):
        m2 = jnp.where(mask, 1.0, 0.0).astype(BF16).reshape(nt * E, W)
        incl = jnp.dot(m2, u_ref[...], preferred_element_type=F32)
        tot = jnp.broadcast_to(incl[:, W - 1:W], (nt * E, LANES))
        base = jnp.dot(l_ref[...], tot.astype(BF16), preferred_element_type=F32)
        return (incl - m2.astype(F32) + base[:, 0:1]).reshape(nt, E, W), base, tot

    eq_rank, _, _ = prefix(eq)
    sel = gt | (eq & (eq_rank < need))
    rank, base, tot = prefix(sel)
    rank_ref[...] = jnp.where(sel, rank.astype(jnp.int32), -1)
    base_ref[...] = base.astype(jnp.int32)
    tot_ref[...] = tot.astype(jnp.int32)


def _route(aff_tm, cap):
    nt, E, W = aff_tm.shape
    upper = np.triu(np.ones((W, W), np.float32))
    idx = np.arange(nt * E)
    lower = ((idx[:, None] % E) == (idx[None, :] % E)) & ((idx[None, :] // E) < (idx[:, None] // E))
    full = lambda a: pl.BlockSpec(a.shape, lambda: (0,) * a.ndim)
    consts = (jnp.asarray(upper, BF16), jnp.asarray(lower, BF16))
    return pl.pallas_call(
        functools.partial(_route_kernel, cap=cap),
        out_shape=(jax.ShapeDtypeStruct((nt, E, W), jnp.int32), jax.ShapeDtypeStruct((nt * E, LANES), jnp.int32),
                   jax.ShapeDtypeStruct((nt * E, LANES), jnp.int32)),
        in_specs=[full(aff_tm), full(consts[0]), full(consts[1])],
        out_specs=(pl.BlockSpec((nt, E, W), lambda: (0, 0, 0)), pl.BlockSpec((nt * E, LANES), lambda: (0, 0)),
                   pl.BlockSpec((nt * E, LANES), lambda: (0, 0))),
        compiler_params=pltpu.CompilerParams(vmem_limit_bytes=40 * MIB),
        name="moe_route",
    )(aff_tm, *consts)


def _slot_onehots(rank, cur, lo_ref, cnt_ref, b, R, clamp_hi):
    E = rank.shape[0]
    riota = lax.broadcasted_iota(jnp.int32, (R, rank.shape[1]), 0)
    ohs, starts, news = [], [], []
    for e in range(E):
        c = cur[e]
        end = lo_ref[b * E + e] + cnt_ref[b * E + e]
        w = (c // MOE_ALIGN) * MOE_ALIGN
        if clamp_hi is not None:
            w = jnp.minimum(w, clamp_hi)
        rk = rank[e:e + 1, :]
        ohs.append(((rk - w) == riota) & (rk >= c))
        starts.append(w)
        news.append(jnp.minimum(end, w + R))
    return ohs, starts, news


def _gather_kernel(lo_ref, cnt_ref, h_ref, rank_ref, aff_ref, xe_ref, xg_ref,
                   stage, gstage, carry, gcarry, cur, npass, sem, *, R, cap):
    b = pl.program_id(0)
    E = rank_ref.shape[1]
    A = MOE_ALIGN

    @pl.when(b == 0)
    def _():
        carry[...] = jnp.zeros_like(carry)
        gcarry[...] = jnp.zeros_like(gcarry)
        npass[0] = 0
        stage[1, 0:R, :] = jnp.zeros((R, stage.shape[2]), stage.dtype)
        gstage[1, 0:R, :] = jnp.zeros((R, LANES), F32)
        pads = []
        for e in range(E):
            pads.append(pltpu.make_async_copy(stage.at[1, pl.ds(0, R)], xe_ref.at[e, pl.ds(cap, R)], sem.at[1]))
            pads.append(pltpu.make_async_copy(gstage.at[1, pl.ds(0, R)], xg_ref.at[e, pl.ds(cap, R)], sem.at[1]))
        for cp in pads:
            cp.start()
        for cp in pads:
            cp.wait()

    for e in range(E):
        cur[e] = lo_ref[b * E + e]
    rank = rank_ref[0]
    aff = aff_ref[0]
    h = h_ref[...]
    srow = lax.broadcasted_iota(jnp.int32, (A, 1), 0)

    def copies(slot, starts):
        cps = []
        for e in range(E):
            w = pl.multiple_of(starts[e], A)
            cps.append(pltpu.make_async_copy(stage.at[slot, pl.ds(e * R, R)], xe_ref.at[e, pl.ds(w, R)], sem.at[slot]))
            cps.append(pltpu.make_async_copy(gstage.at[slot, pl.ds(e * R, R)], xg_ref.at[e, pl.ds(w, R)], sem.at[slot]))
        return cps

    def one_pass(more):
        g = npass[0]
        slot = g % 2
        ohs, starts, news = _slot_onehots(rank, cur, lo_ref, cnt_ref, b, R, None)
        rows = jnp.dot(jnp.concatenate([jnp.where(oh, 1.0, 0.0).astype(BF16) for oh in ohs], axis=0), h,
                       preferred_element_type=F32)
        more = jnp.int32(0)
        for e in range(E):
            gate = jnp.sum(jnp.where(ohs[e], aff[e:e + 1, :], 0.0), axis=1, keepdims=True)
            gate = jnp.broadcast_to(gate, (R, LANES))
            stage[slot, e * R:e * R + A, :] = (rows[e * R:e * R + A] + carry[e]).astype(stage.dtype)
            stage[slot, e * R + A:(e + 1) * R, :] = rows[e * R + A:(e + 1) * R].astype(stage.dtype)
            gstage[slot, e * R:e * R + A, :] = gate[:A] + gcarry[e]
            gstage[slot, e * R + A:(e + 1) * R, :] = gate[A:]
            new = news[e]
            grp = jnp.minimum((new - starts[e]) // A, R // A - 1)
            keep = srow < (new - (new // A) * A)
            off = pl.multiple_of(e * R + grp * A, A)
            carry[e] = jnp.where(keep, stage[slot, pl.ds(off, A), :].astype(F32), 0.0)
            gcarry[e] = jnp.where(keep, gstage[slot, pl.ds(off, A), :], 0.0)
            cur[e] = new
            more = jnp.maximum(more, (new < lo_ref[b * E + e] + cnt_ref[b * E + e]).astype(jnp.int32))

        @pl.when(g > 0)
        def _():
            for cp in copies(1 - slot, [0] * E):
                cp.wait()

        for n, cp in enumerate(copies(slot, starts)):
            cp.start(priority=(n // 2) % 2)
        npass[0] = g + 1
        return more

    lax.while_loop(lambda more: more > 0, one_pass, jnp.int32(1))

    @pl.when(b == pl.num_programs(0) - 1)
    def _():
        for cp in copies((npass[0] - 1) % 2, [0] * E):
            cp.wait()


def _gather(h2, rank, aff_tm, lo, cnt, cap, R):
    n, D = h2.shape
    nt, E, W = rank.shape
    grid_spec = pltpu.PrefetchScalarGridSpec(
        num_scalar_prefetch=2,
        grid=(nt,),
        in_specs=[pl.BlockSpec((W, D), lambda b, lo, cnt: (b, 0)),
                  pl.BlockSpec((1, E, W), lambda b, lo, cnt: (b, 0, 0)),
                  pl.BlockSpec((1, E, W), lambda b, lo, cnt: (b, 0, 0))],
        out_specs=(pl.BlockSpec(memory_space=pl.ANY), pl.BlockSpec(memory_space=pl.ANY)),
        scratch_shapes=[pltpu.VMEM((2, E * R, D), BF16), pltpu.VMEM((2, E * R, LANES), F32),
                        pltpu.VMEM((E, MOE_ALIGN, D), F32), pltpu.VMEM((E, MOE_ALIGN, LANES), F32),
                        pltpu.SMEM((E,), jnp.int32), pltpu.SMEM((1,), jnp.int32),
                        pltpu.SemaphoreType.DMA((2,))])
    return pl.pallas_call(
        functools.partial(_gather_kernel, R=R, cap=cap),
        out_shape=(jax.ShapeDtypeStruct((E, cap + R, D), BF16), jax.ShapeDtypeStruct((E, cap + R, LANES), F32)),
        grid_spec=grid_spec,
        compiler_params=_cparams(("arbitrary",), 48),
        name="moe_gather",
    )(lo, cnt, h2, rank, aff_tm)


FFN_STEPS = 4


def _ffn_kernel(*refs, has_ctx):
    if has_ctx:
        x_ref, g_ref, xc_ref, gc_ref, wg_ref, wu_ref, wd_ref, o_ref, oc_ref, wgs, wus, wds = refs
    else:
        x_ref, g_ref, wg_ref, wu_ref, wd_ref, o_ref, wgs, wus, wds = refs
    e = pl.program_id(0)
    s = pl.program_id(1)
    n_exp = pl.num_programs(0) - 1
    fq = wg_ref.shape[2]

    @pl.when(e < n_exp)
    def _():
        slot = e % 2
        for j in range(FFN_STEPS):

            @pl.when(s == j)
            def _():
                wgs[slot, :, j * fq:(j + 1) * fq] = wg_ref[0].astype(BF16)
                wus[slot, :, j * fq:(j + 1) * fq] = wu_ref[0].astype(BF16)
                wds[slot, j * fq:(j + 1) * fq, :] = wd_ref[0].astype(BF16)

    def expert(x, gate, slot):
        a = jnp.dot(x, wgs[slot], preferred_element_type=F32)
        b = jnp.dot(x, wus[slot], preferred_element_type=F32)
        hid = (a * jax.nn.sigmoid(a) * b).astype(BF16)
        return jnp.dot(hid, wds[slot], preferred_element_type=F32) * gate[:, 0:1]

    @pl.when(e >= 1)
    def _():
        slot = (e - 1) % 2
        o_ref[0] = expert(x_ref[0], g_ref[0], slot).astype(o_ref.dtype)
        if has_ctx:

            @pl.when(s == 0)
            def _():
                oc_ref[0] = expert(xc_ref[0], gc_ref[0], slot).astype(oc_ref.dtype)


def _ffn(xe, xg, wg, wu, wd, layer, cap, ctx=None):
    E, _, D = xe.shape
    F = wg.shape[3]
    S = FFN_STEPS
    M = S if cap % (S * LANES) == 0 else 1
    tm = cap // M
    fq = F // S
    rows = lambda e, s: (jnp.maximum(e - 1, 0), jnp.where(e == 0, 0, s % M), 0)
    crow = lambda e, s: (jnp.maximum(e - 1, 0), 0, 0)
    in_specs = [pl.BlockSpec((1, tm, D), rows), pl.BlockSpec((1, tm, LANES), rows)]
    out_shape = [jax.ShapeDtypeStruct((E, cap, D), BF16)]
    out_specs = [pl.BlockSpec((1, tm, D), rows)]
    args = [xe, xg]
    if ctx is not None:
        xec, xgc, capc = ctx
        in_specs += [pl.BlockSpec((1, capc, D), crow), pl.BlockSpec((1, capc, LANES), crow)]
        out_shape.append(jax.ShapeDtypeStruct((E, capc, D), BF16))
        out_specs.append(pl.BlockSpec((1, capc, D), crow))
        args += [xec, xgc]
    in_specs += [pl.BlockSpec((None, 1, D, fq), lambda e, s: (layer, jnp.minimum(e, E - 1), 0, s)),
                 pl.BlockSpec((None, 1, D, fq), lambda e, s: (layer, jnp.minimum(e, E - 1), 0, s)),
                 pl.BlockSpec((None, 1, fq, D), lambda e, s: (layer, jnp.minimum(e, E - 1), s, 0))]
    return pl.pallas_call(
        functools.partial(_ffn_kernel, has_ctx=ctx is not None),
        out_shape=tuple(out_shape),
        grid=(E + 1, S),
        in_specs=in_specs,
        out_specs=tuple(out_specs),
        scratch_shapes=[pltpu.VMEM((2, D, F), BF16), pltpu.VMEM((2, D, F), BF16), pltpu.VMEM((2, F, D), BF16)],
        compiler_params=_cparams(("arbitrary", "arbitrary"), 52),
        name="expert_ffn",
    )(*args, wg, wu, wd)


def _combine_kernel(lo_ref, cnt_ref, rank_ref, x_ref, g2_ref, ng_ref, y_ref, o_ref, ystage, yovf, acc, cur, sem, osem,
                    *, R, cap, final_norm):
    b = pl.program_id(0)
    nb = pl.num_programs(0)
    E = rank_ref.shape[1]
    A = MOE_ALIGN

    def window(c):
        return pl.multiple_of(jnp.minimum((c // A) * A, cap - R), A)

    def main_copies(tile, slot):
        return [pltpu.make_async_copy(y_ref.at[e, pl.ds(window(lo_ref[tile * E + e]), R)],
                                      ystage.at[slot, pl.ds(e * R, R)], sem.at[slot]) for e in range(E)]

    @pl.when(b == 0)
    def _():
        for cp in main_copies(0, 0):
            cp.start()

    @pl.when(b + 1 < nb)
    def _():
        for n, cp in enumerate(main_copies(b + 1, (b + 1) % 2)):
            cp.start(priority=n % 2)

    for e in range(E):
        cur[e] = lo_ref[b * E + e]
    rank = rank_ref[0]

    def expand(y_rows):
        ohs, _, news = _slot_onehots(rank, cur, lo_ref, cnt_ref, b, R, cap - R)
        more = jnp.int32(0)
        for e in range(E):
            cur[e] = news[e]
            more = jnp.maximum(more, (news[e] < lo_ref[b * E + e] + cnt_ref[b * E + e]).astype(jnp.int32))
        oh = jnp.concatenate([jnp.where(o, 1.0, 0.0).astype(BF16) for o in ohs], axis=0)
        return lax.dot_general(oh, y_rows, (((0,), (0,)), ((), ())), preferred_element_type=F32), more

    for cp in main_copies(b, b % 2):
        cp.wait()
    first, more = expand(ystage[b % 2])
    acc[...] = first

    def extra_pass(more):
        cps = [pltpu.make_async_copy(y_ref.at[e, pl.ds(window(cur[e]), R)], yovf.at[pl.ds(e * R, R)], osem.at[0])
               for e in range(E)]
        for cp in cps:
            cp.start()
        for cp in cps:
            cp.wait()
        part, more = expand(yovf[...])
        acc[...] += part
        return more

    lax.while_loop(lambda more: more > 0, extra_pass, more)
    out = x_ref[...] + g2_ref[...] * acc[...]
    if final_norm:
        out = out * lax.rsqrt(jnp.mean(out * out, axis=-1, keepdims=True) + RMS_EPS) * ng_ref[...]
    o_ref[...] = out


def _combine(y, rank, lo, cnt, x, g2, norm_g, final_norm, cap, R):
    n, D = x.shape
    nt, E, W = rank.shape
    grid_spec = pltpu.PrefetchScalarGridSpec(
        num_scalar_prefetch=2,
        grid=(nt,),
        in_specs=[pl.BlockSpec((1, E, W), lambda b, lo, cnt: (b, 0, 0)),
                  pl.BlockSpec((W, D), lambda b, lo, cnt: (b, 0)),
                  pl.BlockSpec((1, D), lambda b, lo, cnt: (0, 0)),
                  pl.BlockSpec((1, D), lambda b, lo, cnt: (0, 0)),
                  pl.BlockSpec(memory_space=pl.ANY)],
        out_specs=pl.BlockSpec((W, D), lambda b, lo, cnt: (b, 0)),
        scratch_shapes=[pltpu.VMEM((2, E * R, D), BF16), pltpu.VMEM((E * R, D), BF16), pltpu.VMEM((W, D), F32),
                        pltpu.SMEM((E,), jnp.int32), pltpu.SemaphoreType.DMA((2,)), pltpu.SemaphoreType.DMA((1,))])
    return pl.pallas_call(
        functools.partial(_combine_kernel, R=R, cap=cap, final_norm=final_norm),
        out_shape=jax.ShapeDtypeStruct((n, D), F32),
        grid_spec=grid_spec,
        compiler_params=_cparams(("arbitrary",), 48),
        name="moe_combine",
    )(lo, cnt, rank, x, g2, norm_g, y)


def _moe_dispatch(h2, aff_tm):
    n = h2.shape[0]
    cap = EC_FACTOR * n // aff_tm.shape[1]
    R = min(MOE_WIN, cap)
    rank, base, tot = _route(aff_tm, cap)
    lo, cnt = base[:, 0], tot[:, 0]
    xe, xg = _gather(h2, rank, aff_tm, lo, cnt, cap, R)
    return xe, xg, (rank, lo, cnt, cap, R)


def _moe(xl, h2, aff_tm, g2, xc, hc2, affc_tm, gc2, wg, wu, wd, layer, norm_g, final_norm):
    xe, xg, (rank, lo, cnt, cap, R) = _moe_dispatch(h2, aff_tm)
    if xc is None:
        (y,) = _ffn(xe, xg, wg, wu, wd, layer, cap)
        return _combine(y, rank, lo, cnt, xl, g2, norm_g, final_norm, cap, R), None
    xec, xgc, (rankc, loc, cntc, capc, Rc) = _moe_dispatch(hc2, affc_tm)
    y, yc = _ffn(xe, xg, wg, wu, wd, layer, cap, (xec, xgc, capc))
    return (_combine(y, rank, lo, cnt, xl, g2, norm_g, final_norm, cap, R),
            _combine(yc, rankc, loc, cntc, xc, gc2, norm_g, False, capc, Rc))


POOL_HALO = 16


def _pool_kernel(cur_ref, prev_ref, next_ref, w_ref, sc_ref, y_ref, scr, *, seq_len):
    m = pl.program_id(0)
    tm = cur_ref.shape[0]
    H = POOL_HALO
    scr[0:H, :] = jnp.where(m > 0, prev_ref[...].astype(F32), 0.0)
    scr[H:H + tm, :] = cur_ref[...].astype(F32)
    scr[H + tm:2 * H + tm, :] = jnp.where(m < pl.num_programs(0) - 1, next_ref[...].astype(F32), 0.0)
    t = m * tm + lax.broadcasted_iota(jnp.int32, (tm, 1), 0)
    for gi, w in enumerate(POOL_WINDOWS):
        cols = slice(gi * POOL_GROUP_W, (gi + 1) * POOL_GROUP_W)
        total = None
        for k in range(-(w // 2), w // 2):
            term = scr[H + k:H + k + tm, cols]
            total = term if total is None else total + term
        n = jnp.minimum(t + (w // 2 - 1), seq_len - 1) - jnp.maximum(t - w // 2, 0) + 1
        centred = total / n.astype(F32) - scr[H:H + tm, cols]
        y = jnp.dot(centred.astype(BF16), w_ref[gi].astype(BF16), preferred_element_type=F32)
        y_ref[:, cols] = (y * sc_ref[:, cols]).astype(y_ref.dtype)


def _pool(proj, pool_w, pool_scale, tm):
    M = proj.shape[0]
    hb = tm // POOL_HALO
    nhb = M // POOL_HALO
    cb = COL_P // BRANCH_W
    return pl.pallas_call(
        functools.partial(_pool_kernel, seq_len=M),
        out_shape=jax.ShapeDtypeStruct((M, BRANCH_W), BF16),
        grid=(M // tm,),
        in_specs=[pl.BlockSpec((tm, BRANCH_W), lambda m: (m, cb)),
                  pl.BlockSpec((POOL_HALO, BRANCH_W), lambda m: (jnp.maximum(m * hb - 1, 0), cb)),
                  pl.BlockSpec((POOL_HALO, BRANCH_W), lambda m: (jnp.minimum((m + 1) * hb, nhb - 1), cb)),
                  pl.BlockSpec(pool_w.shape, lambda m: (0, 0, 0)),
                  pl.BlockSpec((1, BRANCH_W), lambda m: (0, 0))],
        out_specs=pl.BlockSpec((tm, BRANCH_W), lambda m: (m, 0)),
        scratch_shapes=[pltpu.VMEM((tm + 2 * POOL_HALO, BRANCH_W), F32)],
        compiler_params=_cparams(("parallel",), 32),
        name="pool_mix",
    )(proj, proj, proj, pool_w, pool_scale)


def _gelu(x):
    return 0.5 * x * (1.0 + lax.erf(x * (2.0 ** -0.5)))


def _sgu_kernel(u_ref, v_ref, g_ref, w_ref, b_ref, y_ref):
    tm = u_ref.shape[0]
    v = _gelu(v_ref[...].astype(F32))
    v = (v * lax.rsqrt(jnp.mean(v * v, axis=-1, keepdims=True) + RMS_EPS) * g_ref[...]).astype(BF16)
    u = _gelu(u_ref[...].astype(F32))
    for c in range(tm // SGU_CHUNK):
        rows = slice(c * SGU_CHUNK, (c + 1) * SGU_CHUNK)
        for g in range(SGU_GROUPS):
            cols = slice(g * SGU_GROUP_W, (g + 1) * SGU_GROUP_W)
            mixed = jnp.dot(w_ref[g].astype(BF16), v[rows, cols], preferred_element_type=F32) + b_ref[:, g:g + 1]
            y_ref[rows, cols] = (u[rows, cols] * mixed).astype(y_ref.dtype)


def _sgu(proj, norm_g, w_s, b_s, tm):
    M = proj.shape[0]
    bt = jnp.pad(b_s.T, ((0, 0), (0, LANES - SGU_GROUPS)))
    return pl.pallas_call(
        _sgu_kernel,
        out_shape=jax.ShapeDtypeStruct((M, BRANCH_W), BF16),
        grid=(M // tm,),
        in_specs=[pl.BlockSpec((tm, BRANCH_W), lambda m: (m, COL_GU // BRANCH_W)),
                  pl.BlockSpec((tm, BRANCH_W), lambda m: (m, COL_GV // BRANCH_W)),
                  pl.BlockSpec((1, BRANCH_W), lambda m: (0, 0)),
                  pl.BlockSpec(w_s.shape, lambda m: (0, 0, 0)),
                  pl.BlockSpec((SGU_CHUNK, LANES), lambda m: (0, 0))],
        out_specs=pl.BlockSpec((tm, BRANCH_W), lambda m: (m, 0)),
        compiler_params=_cparams(("parallel",), 32),
        name="sgu_mix",
    )(proj, proj, norm_g, w_s, bt)


DN_CONV = 5
DN_HALO = 16
DN_QKV_W = 3 * BRANCH_W
DN_STATES = 2 * DN_HEADS


def _dn_prep_kernel(cur_ref, prev_ref, next_ref, ab_ref, cw_ref, alog_ref, dtb_ref,
                    q_ref, k_ref, v_ref, aux_ref, scr):
    m = pl.program_id(0)
    tm = cur_ref.shape[0]
    scr[0:DN_HALO, :] = jnp.where(m > 0, prev_ref[...].astype(F32), 0.0)
    scr[DN_HALO:DN_HALO + tm, :] = cur_ref[...].astype(F32)
    scr[DN_HALO + tm:2 * DN_HALO + tm, :] = jnp.where(m < pl.num_programs(0) - 1, next_ref[...].astype(F32), 0.0)
    acc = None
    for t in range(DN_CONV):
        lo = DN_HALO - DN_CONV // 2 + t
        term = scr[lo:lo + tm, :] * cw_ref[t:t + 1, :]
        acc = term if acc is None else acc + term
    y = acc * jax.nn.sigmoid(acc)
    for h in range(DN_HEADS):
        qh = y[:, h * DN_DK:(h + 1) * DN_DK]
        q_ref[:, h * DN_DK:(h + 1) * DN_DK] = (
            qh * lax.rsqrt(jnp.sum(qh * qh, axis=-1, keepdims=True) + 1e-6) * DN_DK ** -0.5)
        kh = y[:, BRANCH_W + h * DN_DK:BRANCH_W + (h + 1) * DN_DK]
        k_ref[:, h * DN_DK:(h + 1) * DN_DK] = kh * lax.rsqrt(jnp.sum(kh * kh, axis=-1, keepdims=True) + 1e-6)
    v_ref[...] = y[:, 2 * BRANCH_W:]

    ab = ab_ref[...]
    lane = lax.broadcasted_iota(jnp.int32, ab.shape, 1)
    sp = ab + dtb_ref[...]
    softplus = jnp.maximum(sp, 0.0) + jnp.log(1.0 + jnp.exp(-jnp.abs(sp)))
    g = jnp.where(lane < DN_STATES, -jnp.exp(alog_ref[...]) * softplus, 0.0)
    beta = jax.nn.sigmoid(ab)
    g1 = g.astype(BF16)
    r1 = g - g1.astype(F32)
    g2 = r1.astype(BF16)
    g3 = (r1 - g2.astype(F32)).astype(BF16)
    gs = jnp.concatenate([g1, g2, g3], axis=1)
    row = lax.broadcasted_iota(jnp.int32, (tm, tm), 0)
    col = lax.broadcasted_iota(jnp.int32, (tm, tm), 1)
    same = (row // DN_CHUNK) == (col // DN_CHUNK)
    tri_f = jnp.where(same & (col <= row), 1.0, 0.0).astype(BF16)
    tri_b = jnp.where(same & (col >= row), 1.0, 0.0).astype(BF16)
    cf = jnp.dot(tri_f, gs, preferred_element_type=F32)
    cb = jnp.dot(tri_b, gs, preferred_element_type=F32)
    cf = cf[:, :LANES] + cf[:, LANES:2 * LANES] + cf[:, 2 * LANES:]
    cb = cb[:, :LANES] + cb[:, LANES:2 * LANES] + cb[:, 2 * LANES:]
    gc = jnp.where(lane < DN_HEADS, cf, cb)
    aux_ref[...] = jnp.where(lane < DN_STATES, gc, beta)


def _dn_prep(proj, ab, cw, alog, dtb, tm):
    M = proj.shape[0]
    hb = tm // DN_HALO
    nhb = M // DN_HALO
    row = pl.BlockSpec((1, LANES), lambda m: (0, 0))
    return pl.pallas_call(
        _dn_prep_kernel,
        out_shape=(jax.ShapeDtypeStruct((M, BRANCH_W), F32),) * 3 + (jax.ShapeDtypeStruct((M, LANES), F32),),
        grid=(M // tm,),
        in_specs=[pl.BlockSpec((tm, DN_QKV_W), lambda m: (m, 0)),
                  pl.BlockSpec((DN_HALO, DN_QKV_W), lambda m: (jnp.maximum(m * hb - 1, 0), 0)),
                  pl.BlockSpec((DN_HALO, DN_QKV_W), lambda m: (jnp.minimum((m + 1) * hb, nhb - 1), 0)),
                  pl.BlockSpec((tm, LANES), lambda m: (m, 0)),
                  pl.BlockSpec((8, DN_QKV_W), lambda m: (0, 0)), row, row],
        out_specs=(pl.BlockSpec((tm, BRANCH_W), lambda m: (m, 0)),) * 3 + (pl.BlockSpec((tm, LANES), lambda m: (m, 0)),),
        scratch_shapes=[pltpu.VMEM((tm + 2 * DN_HALO, DN_QKV_W), F32)],
        compiler_params=_cparams(("parallel",), 40),
        name="dn_prep",
    )(proj, proj, proj, ab, cw, alog, dtb)


def _dn_constants():
    L, H = DN_CHUNK, DN_HEADS
    WW, FW = DN_STATES * L, DN_STATES * DN_DK
    r = np.arange(L)[:, None]
    lane = np.arange(WW)[None, :]
    c = lane % L
    lower = lane < H * L
    cm = np.zeros((9, L, WW), np.float32)
    cm[0] = r == c
    cm[1] = np.where(np.where(lower, r >= c, r <= c), 0.0, -np.inf)
    cm[2] = np.where(lower, r > c, r < c)
    hi, lo = np.where(lower, r, c), np.where(lower, c, r)
    for j in range(6):
        s = 2 ** j
        cm[3 + j] = ((r // (2 * s)) == (c // (2 * s))) & (hi % (2 * s) >= s) & (lo % (2 * s) < s)
    rw = np.arange(WW)[:, None]
    rf = np.arange(H * DN_DK)[:, None]
    m_ww = rw // L == np.arange(WW)[None] // L
    m_wf = rw // L == np.arange(FW)[None] // DN_DK
    m_w2f = rw // L == (np.arange(2 * FW)[None] % FW) // DN_DK
    m_ff = rf // DN_DK == np.arange(H * DN_DK)[None] // DN_DK
    return (jnp.asarray(cm),) + tuple(jnp.asarray(m, BF16) for m in (m_ww, m_wf, m_w2f, m_ff))


def _dn_scan_kernel(qf_ref, kf_ref, vf_ref, af_ref, qb_ref, kb_ref, vb_ref, ab_ref, s0_ref,
                    cm_ref, mww_ref, mwf_ref, mw2f_ref, mff_ref, of_ref, ob_ref, sfin_ref,
                    s_scr, u_st, w_st, qd_st, kd_st, qk_st, gl_st):
    j = pl.program_id(0)
    L, H = DN_CHUNK, DN_HEADS
    FD = H * DN_DK

    @pl.when(j == 0)
    def _():
        s_scr[...] = s0_ref[...]
        for st in (u_st, w_st, qd_st, kd_st, qk_st, gl_st):
            st[...] = jnp.zeros(st.shape, st.dtype)

    lane = lax.broadcasted_iota(jnp.int32, (L, LANES), 1)
    rowblk = lax.broadcasted_iota(jnp.int32, (FD, DN_DV), 0) // DN_DK
    m_ww, m_wf, m_w2f, m_ff = mww_ref[...], mwf_ref[...], mw2f_ref[...], mff_ref[...]

    def tile_rows(a):
        return jnp.concatenate([a] * DN_STATES, axis=0)

    def col(arr, c):
        return jnp.broadcast_to(arr[:, c:c + 1], (arr.shape[0], LANES))

    def feat(arrs, base):
        return jnp.concatenate([col(arrs[d], base + d * H + h) for d in range(2) for h in range(H)], axis=1)

    def wide(arrs, base):
        return jnp.concatenate([jnp.where(lane < L, col(arrs[d], base + d * H + 2 * p),
                                          col(arrs[d], base + d * H + 2 * p + 1))
                                for d in range(2) for p in range(H // 2)], axis=1)

    u, w, qd, kd, qk = u_st[...], w_st[...], qd_st[...], kd_st[...], qk_st[...]
    g_prev = gl_st[...]
    live = j > 0
    dir_cols = [slice(d * FD, (d + 1) * FD) for d in range(2)]
    s_old = [s_scr[d] for d in range(2)]
    val = {}

    def ws_qs(d):
        s_bd = jnp.concatenate([s_old[d].astype(BF16)] * H, axis=1) * m_ff
        val["ws", d] = jnp.dot(jnp.concatenate([w[:, dir_cols[d]], qd[:, dir_cols[d]]], axis=0).astype(BF16), s_bd,
                               preferred_element_type=F32)

    def outputs():
        val["v"] = (u - jnp.concatenate([val["ws", 0][:L], val["ws", 1][:L]], axis=1)).astype(BF16)
        val["o"] = (jnp.concatenate([val["ws", 0][L:], val["ws", 1][L:]], axis=1)
                    + jnp.dot(qk, tile_rows(val["v"]) * m_wf, preferred_element_type=F32))

    def state(d):
        cross = lax.dot_general(kd[:, dir_cols[d]], val["v"][:, dir_cols[d]], (((0,), (0,)), ((), ())),
                                preferred_element_type=F32)
        fold = None
        g_last = None
        for h in range(H):
            part = jnp.where(rowblk == h, cross[:, h * DN_DV:(h + 1) * DN_DV], 0.0)
            gl = jnp.where(rowblk == h, g_prev[:, d * H + h:d * H + h + 1], 0.0)
            fold = part if fold is None else fold + part
            g_last = gl if g_last is None else g_last + gl
        val["s", d] = jnp.where(live, g_last * s_old[d] + fold, s_old[d])

    queue = [lambda: ws_qs(0), lambda: ws_qs(1), outputs, lambda: state(0), lambda: state(1)]

    def issue_one():
        if queue:
            queue.pop(0)()

    aux = (af_ref[...], ab_ref[...])
    last = (aux[0][L - 1:L, :], aux[1][0:1, :])
    e_gc = tuple(jnp.exp(a) for a in aux)
    e_rem = tuple(jnp.exp(l - a) for l, a in zip(last, aux))
    beta_f = feat(aux, DN_STATES)
    eg_f = feat(e_gc, 0)
    er_f = feat(e_rem, 0)
    gcw = wide(aux, 0)
    eye = cm_ref[0]
    gc_row = jnp.sum(eye * gcw, axis=0, keepdims=True)
    decay = jnp.exp(gcw - gc_row + cm_ref[1])
    q = jnp.concatenate([qf_ref[...], qb_ref[...]], axis=1)
    k = jnp.concatenate([kf_ref[...], kb_ref[...]], axis=1)
    v = jnp.concatenate([vf_ref[...], vb_ref[...]], axis=1)
    kbeta = k * beta_f
    kq = lax.dot_general(jnp.concatenate([kbeta, q], axis=0).astype(BF16), tile_rows(k.astype(BF16)) * m_wf,
                         (((1,), (1,)), ((), ())), preferred_element_type=F32)
    issue_one()
    mm = kq[:L] * decay * cm_ref[2]
    x = eye - mm * cm_ref[3]
    for lvl in range(4, 9):
        xb = x.astype(BF16)
        t = jnp.dot(xb, tile_rows((mm * cm_ref[lvl]).astype(BF16)) * m_ww, preferred_element_type=F32)
        issue_one()
        x = x - jnp.dot(t.astype(BF16), tile_rows(xb) * m_ww, preferred_element_type=F32)
        issue_one()
    rhs = jnp.concatenate([v * beta_f, kbeta * eg_f], axis=1)
    sol = rhs + jnp.dot((x - eye).astype(BF16), tile_rows(rhs.astype(BF16)) * m_w2f, preferred_element_type=F32)
    while queue:
        issue_one()
    for d, o_ref in enumerate((of_ref, ob_ref)):
        s_scr[d] = val["s", d]
        o_ref[...] = val["o"][:, dir_cols[d]]
    u_st[...] = sol[:, :2 * FD]
    w_st[...] = sol[:, 2 * FD:]
    qd_st[...] = q * eg_f
    kd_st[...] = (k * er_f).astype(BF16)
    qk_st[...] = (kq[L:] * decay).astype(BF16)
    gl_st[...] = jnp.where(lane[0:1] < H, jnp.exp(last[0]), jnp.exp(last[1]))

    @pl.when(j == pl.num_programs(0) - 1)
    def _():
        sfin_ref[...] = s_scr[...]


def _dn_scan(q, k, v, aux, s0, consts):
    M = q.shape[0]
    N = M // DN_CHUNK
    FW = 2 * BRANCH_W
    fwd = lambda j: (jnp.minimum(j, N - 1), 0)
    bwd = lambda j: (jnp.maximum(N - 1 - j, 0), 0)
    fwd_o = lambda j: (jnp.maximum(j - 1, 0), 0)
    bwd_o = lambda j: (N - 1 - jnp.maximum(j - 1, 0), 0)
    cs = lambda w, im: pl.BlockSpec((DN_CHUNK, w), im)
    st = pl.BlockSpec(s0.shape, lambda j: (0, 0, 0))
    const_specs = [pl.BlockSpec(c.shape, lambda j, _n=c.ndim: (0,) * _n) for c in consts]
    return pl.pallas_call(
        _dn_scan_kernel,
        out_shape=(jax.ShapeDtypeStruct((M, BRANCH_W), F32), jax.ShapeDtypeStruct((M, BRANCH_W), F32),
                   jax.ShapeDtypeStruct(s0.shape, F32)),
        grid=(N + 1,),
        in_specs=[cs(BRANCH_W, fwd), cs(BRANCH_W, fwd), cs(BRANCH_W, fwd), cs(LANES, fwd),
                  cs(BRANCH_W, bwd), cs(BRANCH_W, bwd), cs(BRANCH_W, bwd), cs(LANES, bwd), st] + const_specs,
        out_specs=(cs(BRANCH_W, fwd_o), cs(BRANCH_W, bwd_o), st),
        scratch_shapes=[pltpu.VMEM(s0.shape, F32),
                        pltpu.VMEM((DN_CHUNK, FW), F32), pltpu.VMEM((DN_CHUNK, FW), F32),
                        pltpu.VMEM((DN_CHUNK, FW), F32), pltpu.VMEM((DN_CHUNK, FW), BF16),
                        pltpu.VMEM((DN_CHUNK, DN_STATES * DN_CHUNK), BF16), pltpu.VMEM((1, LANES), F32)],
        compiler_params=_cparams(("arbitrary",), 32),
        name="dn_scan",
    )(q, k, v, aux, q, k, v, aux, s0, *consts)


def _dn_out_kernel(of_ref, ob_ref, z_ref, g_ref, y_ref):
    o = of_ref[...] + ob_ref[...]
    z = z_ref[...].astype(F32)
    for h in range(DN_HEADS):
        sl = slice(h * DN_DV, (h + 1) * DN_DV)
        oh = o[:, sl]
        zh = z[:, sl]
        yh = oh * lax.rsqrt(jnp.mean(oh * oh, axis=-1, keepdims=True) + RMS_EPS) * g_ref[...]
        y_ref[:, sl] = (yh * (zh * jax.nn.sigmoid(zh))).astype(y_ref.dtype)


def _dn_out(o_f, o_b, proj, norm_g, tm):
    M = o_f.shape[0]
    blk = pl.BlockSpec((tm, BRANCH_W), lambda m: (m, 0))
    return pl.pallas_call(
        _dn_out_kernel,
        out_shape=jax.ShapeDtypeStruct((M, BRANCH_W), BF16),
        grid=(M // tm,),
        in_specs=[blk, blk, pl.BlockSpec((tm, BRANCH_W), lambda m: (m, COL_DZ // BRANCH_W)),
                  pl.BlockSpec((1, DN_DV), lambda m: (0, 0))],
        out_specs=blk,
        compiler_params=_cparams(("parallel",), 32),
        name="dn_out",
    )(o_f, o_b, proj, norm_g)


def _deltanet(proj, ab, projc, abc, conv_w, a_log, dt_bias, norm_g, want_ctx, tm):
    cw = jnp.pad(conv_w, ((0, 8 - DN_CONV), (0, 0)))
    alog = jnp.pad(a_log.reshape(1, DN_STATES), ((0, 0), (0, LANES - DN_STATES)))
    dtb = jnp.pad(dt_bias.reshape(1, DN_STATES), ((0, 0), (0, LANES - DN_STATES)))
    ql, kl, vl, auxl = _dn_prep(proj, ab, cw, alog, dtb, tm)
    qx, kx, vx, auxx = _dn_prep(projc, abc, cw, alog, dtb, projc.shape[0])
    consts = _dn_constants()
    s0 = jnp.zeros((2, DN_HEADS * DN_DK, DN_DV), F32)
    oc_f, oc_b, s1 = _dn_scan(qx, kx, vx, auxx, s0, consts)
    o_f, o_b, _ = _dn_scan(ql, kl, vl, auxl, s1, consts)
    y = _dn_out(o_f, o_b, proj, norm_g[None], tm)
    yc = _dn_out(oc_f, oc_b, projc, norm_g[None], projc.shape[0]) if want_ctx else None
    return y, yc


SWA_PREP_W = 8 * LANES


def _rope_tables(T):
    half = SWA_HD // 4
    inv = ROPE_BASE ** (-jnp.arange(half, dtype=F32) / half)
    t = jnp.arange(T, dtype=jnp.int32)
    ang_r = (t // GRID_W).astype(F32)[:, None] * inv
    ang_c = (t % GRID_W).astype(F32)[:, None] * inv
    cos = jnp.concatenate([jnp.cos(ang_r)] * 2 + [jnp.cos(ang_c)] * 2, axis=1)
    sin = jnp.concatenate([-jnp.sin(ang_r), jnp.sin(ang_r), -jnp.sin(ang_c), jnp.sin(ang_c)], axis=1)
    return jnp.tile(cos, (1, 2)), jnp.tile(sin, (1, 2))


def _swa_prep_kernel(q_ref, kv_ref, cos_ref, sin_ref, qo_ref, kvo_ref, *, rope):
    lane = lax.broadcasted_iota(jnp.int32, (q_ref.shape[0], LANES), 1)
    first = (lane % (SWA_HD // 2)) < (SWA_HD // 4)
    low = lane < SWA_HD

    def rot(x):
        if not rope:
            return x
        partner = jnp.where(first, pltpu.roll(x, LANES - SWA_HD // 4, 1), pltpu.roll(x, SWA_HD // 4, 1))
        return x * cos_ref[...] + partner * sin_ref[...]

    for c in range(q_ref.shape[1] // LANES):
        cols = slice(c * LANES, (c + 1) * LANES)
        qo_ref[:, cols] = (rot(q_ref[:, cols].astype(F32)) * SWA_HD ** -0.5).astype(qo_ref.dtype)
    kv = kv_ref[...].astype(F32)
    for j, x in enumerate((rot(kv[:, :LANES]), kv[:, LANES:])):
        sw = pltpu.roll(x, SWA_HD, 1)
        parts = (jnp.where(low, x, 0.0), jnp.where(low, 0.0, sw), jnp.where(low, sw, 0.0), jnp.where(low, 0.0, x))
        for i, part in enumerate(parts):
            kvo_ref[:, (4 * j + i) * LANES:(4 * j + i + 1) * LANES] = part.astype(kvo_ref.dtype)


def _swa_prep(proj, cos, sin, rope, tm):
    M = proj.shape[0]
    tab = pl.BlockSpec((tm, LANES), lambda m: (m, 0))
    return pl.pallas_call(
        functools.partial(_swa_prep_kernel, rope=rope),
        out_shape=(jax.ShapeDtypeStruct((M, BRANCH_W), BF16), jax.ShapeDtypeStruct((M, SWA_PREP_W), BF16)),
        grid=(M // tm,),
        in_specs=[pl.BlockSpec((tm, BRANCH_W), lambda m: (m, COL_SQ // BRANCH_W)),
                  pl.BlockSpec((tm, 2 * LANES), lambda m: (m, COL_SK // (2 * LANES))), tab, tab],
        out_specs=(pl.BlockSpec((tm, BRANCH_W), lambda m: (m, 0)), pl.BlockSpec((tm, SWA_PREP_W), lambda m: (m, 0))),
        compiler_params=_cparams(("parallel",), 32),
        name="swa_prep",
    )(proj, proj, cos, sin)


def _swa_masks(n_ctx):
    nk = 3 * SWA_BLOCK + n_ctx
    i = np.arange(SWA_BLOCK)[:, None]
    j = np.arange(nk)[None, :]
    lat = j < 3 * SWA_BLOCK
    band = ~lat | (np.abs(j - SWA_BLOCK - i) <= SWA_WINDOW)
    no_prev = ~(j < SWA_BLOCK) | (i < 0)
    no_next = ~((j >= 2 * SWA_BLOCK) & lat) | (i < 0)
    ctx_only = ~lat | (i < 0)
    m = np.stack([np.where(ok, 0.0, -np.inf) for ok in (band, no_prev, no_next, ctx_only)]).astype(np.float32)
    return jnp.asarray(np.concatenate([m, m], axis=2))


def _swa_attn_kernel(sink_ref, q_ref, kvp_ref, kvo_ref, kvn_ref, kvc_ref, m_ref, y_ref, *, latent_keys):
    n = pl.program_id(0)
    kv = jnp.concatenate([kvp_ref[...], kvo_ref[...], kvn_ref[...], kvc_ref[...]], axis=0)
    nk = kv.shape[0]
    if latent_keys:
        mask = (m_ref[0] + jnp.where(n == 0, m_ref[1], 0.0) + jnp.where(n == pl.num_programs(0) - 1, m_ref[2], 0.0))
    else:
        mask = m_ref[3]
    lane = lax.broadcasted_iota(jnp.int32, (q_ref.shape[0], LANES), 1)
    for c in range(SWA_Q_HEADS // 2):
        kh = (2 * c) // SWA_GROUP
        cols = slice(c * LANES, (c + 1) * LANES)
        kk = jnp.concatenate([kv[:, (2 * kh) * LANES:(2 * kh + 1) * LANES],
                              kv[:, (2 * kh + 1) * LANES:(2 * kh + 2) * LANES]], axis=0)
        vv = jnp.concatenate([kv[:, (4 + 2 * kh) * LANES:(5 + 2 * kh) * LANES],
                              kv[:, (5 + 2 * kh) * LANES:(6 + 2 * kh) * LANES]], axis=0)
        s = lax.dot_general(q_ref[:, cols], kk, (((1,), (1,)), ((), ())), preferred_element_type=F32) + mask
        ps, dens = [], []
        for half in range(2):
            sh = s[:, half * nk:(half + 1) * nk]
            sink = sink_ref[2 * c + half]
            mx = jnp.maximum(jnp.max(sh, axis=1, keepdims=True), sink)
            p = jnp.exp(sh - mx)
            dens.append(jnp.sum(p, axis=1, keepdims=True) + jnp.exp(sink - mx))
            ps.append(p.astype(BF16))
        o = jnp.dot(jnp.concatenate(ps, axis=1), vv, preferred_element_type=F32)
        y_ref[:, cols] = (o / jnp.where(lane < SWA_HD, dens[0], dens[1])).astype(y_ref.dtype)


def _swa_attn(q, kvr, kvc, sink, masks, latent_keys):
    M = q.shape[0]
    NB = M // SWA_BLOCK
    blk = lambda im: pl.BlockSpec((SWA_BLOCK, SWA_PREP_W), im)
    return pl.pallas_call(
        functools.partial(_swa_attn_kernel, latent_keys=latent_keys),
        out_shape=jax.ShapeDtypeStruct((M, BRANCH_W), BF16),
        grid=(NB,),
        in_specs=[pl.BlockSpec(memory_space=pltpu.SMEM),
                  pl.BlockSpec((SWA_BLOCK, BRANCH_W), lambda n: (n, 0)),
                  blk(lambda n: (jnp.maximum(n - 1, 0), 0)), blk(lambda n: (n, 0)),
                  blk(lambda n: (jnp.minimum(n + 1, NB - 1), 0)),
                  pl.BlockSpec(kvc.shape, lambda n: (0, 0)),
                  pl.BlockSpec(masks.shape, lambda n: (0, 0, 0))],
        out_specs=pl.BlockSpec((SWA_BLOCK, BRANCH_W), lambda n: (n, 0)),
        compiler_params=_cparams(("parallel",), 32),
        name="swa_attn",
    )(sink, q, kvr, kvr, kvr, kvc, masks)


def _swa(proj, projc, sink, want_ctx, tm):
    T, Lc = proj.shape[0], projc.shape[0]
    cos, sin = _rope_tables(T)
    q, kvr = _swa_prep(proj, cos, sin, True, tm)
    none = jnp.zeros((Lc, LANES), F32)
    qc, kvc = _swa_prep(projc, none, none, False, Lc)
    masks = _swa_masks(Lc)
    y = _swa_attn(q, kvr, kvc, sink, masks, True)
    yc = _swa_attn(qc, kvc, kvc, sink, masks, False) if want_ctx else None
    return y, yc


def kernel(x, c, ctx, c_ctx, w_ada, b_ada, norm1_g, norm2_g, w_in, pool_w, pool_scale, dn_conv_w, dn_a_log, dn_dt_bias, dn_norm_g, swa_sink, sgu_norm_g, sgu_w, sgu_b, w_gate, w_branch, w_out, w_router, b_router, w_e_gate, w_e_up, w_e_down, final_g):
    B, T, D = x.shape
    assert B == 1 and D == D_MODEL
    Lc = ctx.shape[1]
    depth = w_ada.shape[0]

    cond8 = jnp.zeros((8, D), F32).at[0].set(c[0]).at[1].set(c_ctx)
    mods = _ada(cond8, w_ada, b_ada)

    xl = x[0]
    xc = ctx[0]
    tm_l = 512
    tm_c = Lc
    for i in range(depth):
        want_ctx = i < depth - 1
        ml = [mods[i, 0:1, j * D:(j + 1) * D] for j in range(6)]
        mc = [mods[i, 1:2, j * D:(j + 1) * D] for j in range(6)]
        n1 = norm1_g[i][None]
        n2 = norm2_g[i][None]
        w_main, w_ab = _w_in_prep(w_in, i)
        wgt = w_gate[i].astype(BF16)
        wbr = w_branch[i].astype(BF16)
        wo = w_out[i].astype(BF16)
        wr_t = w_router[i].T.astype(BF16)
        br = b_router[i][:, None]

        proj, ab, h = _inproj(xl, n1, ml[0], ml[1], w_main, w_ab, tm_l)
        projc, abc, hc = _inproj(xc, n1, mc[0], mc[1], w_main, w_ab, tm_c)
        y_pool = _pool(proj, pool_w[i], pool_scale[i][None], tm_l)
        y_dn, yc_dn = _deltanet(proj, ab, projc, abc, dn_conv_w[i], dn_a_log[i], dn_dt_bias[i], dn_norm_g[i],
                                want_ctx, tm_l)
        y_swa, yc_swa = _swa(proj, projc, swa_sink[i], want_ctx, tm_l)
        y_sgu = _sgu(proj, sgu_norm_g[i][None], sgu_w[i], sgu_b[i], tm_l)
        ys = (y_pool, y_dn, y_swa, y_sgu)

        acc = _merge(h, ys, wgt, wbr, tm_l)
        xl, h2, aff_t = _outproj(acc, xl, wo, ml[2], n2, ml[3], ml[4], wr_t, br, tm_l)
        if want_ctx:
            yc_pool = _pool(projc, pool_w[i], pool_scale[i][None], tm_c)
            yc_sgu = _sgu(projc, sgu_norm_g[i][None], sgu_w[i], sgu_b[i], tm_c)
            accc = _merge(hc, (yc_pool, yc_dn, yc_swa, yc_sgu), wgt, wbr, tm_c)
            xc, hc2, affc_t = _outproj(accc, xc, wo, mc[2], n2, mc[3], mc[4], wr_t, br, tm_c)
            xl, xc = _moe(xl, h2, aff_t, ml[5], xc, hc2, affc_t, mc[5], w_e_gate, w_e_up, w_e_down, i,
                          final_g[None], False)
        else:
            xl, _ = _moe(xl, h2, aff_t, ml[5], None, None, None, None, w_e_gate, w_e_up, w_e_down, i,
                         final_g[None], i == depth - 1)

    return xl[None]
```
